```python
import math
import jax, jax.numpy as jnp
from jax import lax
import numpy as np

D_MODEL = 1024
BATCH = 2
SEQ = 16384
DEPTH = 2

PLE_DIM = 256
HEAD_DIM = 64
DIFF_HEADS = 4
DSA_HEADS = 8
IDX_HEADS = 8
IDX_DIM = 64
TOPK_MAX = 256
N_GROUPS = 4
EXPERTS_PER_GROUP = 8
TOP_K_IN_GROUP = 2
D_EXPERT = 256
ROPE_THETA = 500000.0
ROT_DIM = HEAD_DIM // 4
Q_BLOCK = 128
LN_EPS = 1e-5
NEG_INF = -1e30
ALPHA = (2 * DEPTH) ** 0.25
BETA = (8 * DEPTH) ** -0.25
DIFF_WIDTH = DIFF_HEADS * 2 * HEAD_DIM
DSA_WIDTH = DSA_HEADS * HEAD_DIM
SPLIT_SIZES = (DIFF_WIDTH, DIFF_WIDTH, DIFF_WIDTH, DSA_WIDTH, DSA_WIDTH, DSA_WIDTH,
               IDX_HEADS * IDX_DIM, IDX_DIM, IDX_HEADS, D_MODEL, D_MODEL)
W_IN_COLS = sum(SPLIT_SIZES)

kernel_name = 'hybrid_diffattn_dsa_hiermoe_deepnorm'


def layer_norm(x, g, b):
    xf = x.astype(jnp.float32)
    mu = jnp.mean(xf, axis=-1, keepdims=True)
    var = jnp.mean(jnp.square(xf - mu), axis=-1, keepdims=True)
    y = (xf - mu) * lax.rsqrt(var + LN_EPS)
    return (y * g.astype(jnp.float32) + b.astype(jnp.float32)).astype(x.dtype)


def rms_norm(x, g):
    xf = x.astype(jnp.float32)
    y = xf * lax.rsqrt(jnp.mean(jnp.square(xf), axis=-1, keepdims=True) + LN_EPS)
    return (y * g.astype(jnp.float32)).astype(x.dtype)


def rope_tables(positions):
    inv_freq = 1.0 / (ROPE_THETA ** (jnp.arange(0, ROT_DIM, 2, dtype=jnp.float32) / ROT_DIM))
    ang = positions.astype(jnp.float32)[..., None] * inv_freq
    return jnp.cos(ang), jnp.sin(ang)


def partial_rope(t, cos, sin):
    half = ROT_DIM // 2
    shp = cos.shape[:2] + (1,) * (t.ndim - 3) + (half,)
    c = cos.reshape(shp).astype(t.dtype)
    s = sin.reshape(shp).astype(t.dtype)
    x1 = t[..., :half]
    x2 = t[..., half:ROT_DIM]
    return jnp.concatenate([x1 * c - x2 * s, x2 * c + x1 * s, t[..., ROT_DIM:]], axis=-1)


def diff_attention(q, k, v, lam):
    B, S, H, _, dk = q.shape
    nb = S // Q_BLOCK
    scale = dk ** -0.5
    qb = q.reshape(B, nb, Q_BLOCK, H, 2, dk).swapaxes(0, 1)
    key_pos = jnp.arange(S)

    def one_block(args):
        q_i, b_i = args
        q_pos = b_i * Q_BLOCK + jnp.arange(Q_BLOCK)
        s = jnp.einsum('bqhcd,bkhcd->bhcqk', q_i, k).astype(jnp.float32) * scale
        s = jnp.where(key_pos[None, :] <= q_pos[:, None], s, NEG_INF)
        pr = jax.nn.softmax(s, axis=-1)
        a = pr[:, :, 0] - lam * pr[:, :, 1]
        return jnp.einsum('bhqk,bkhe->bqhe', a.astype(v.dtype), v)

    o = lax.map(one_block, (qb, jnp.arange(nb)))
    return o.swapaxes(0, 1).reshape(B, S, H, v.shape[-1])


def dsa_attention(q, k, v, q_idx, k_idx, w_idx, topk):
    B, S, H, dh = q.shape
    nb = S // Q_BLOCK
    scale = dh ** -0.5
    to_blocks = lambda t: t.reshape((B, nb, Q_BLOCK) + t.shape[2:]).swapaxes(0, 1)
    key_pos = jnp.arange(S)

    def one_block(args):
        q_b, qi_b, wi_b, b_i = args
        q_pos = b_i * Q_BLOCK + jnp.arange(Q_BLOCK)
        dots = jnp.einsum('bqhd,bsd->bqhs', qi_b, k_idx).astype(jnp.float32) * (IDX_DIM ** -0.5)
        score = jnp.einsum('bqhs,bqh->bqs', jax.nn.relu(dots), wi_b.astype(jnp.float32))
        score = jnp.where(key_pos[None, None, :] <= q_pos[None, :, None], score, NEG_INF)
        _, sel = lax.top_k(score, topk)
        valid = sel <= q_pos[None, :, None]
        kg = jax.vmap(lambda kb, ib: kb[ib])(k, sel)
        vg = jax.vmap(lambda vb, ib: vb[ib])(v, sel)
        s = jnp.einsum('bqhd,bqkhd->bhqk', q_b, kg).astype(jnp.float32) * scale
        s = jnp.where(valid[:, None], s, NEG_INF)
        pr = jax.nn.softmax(s, axis=-1)
        return jnp.einsum('bhqk,bqkhd->bqhd', pr.astype(v.dtype), vg)

    o = lax.map(one_block, (to_blocks(q), to_blocks(q_idx), to_blocks(w_idx), jnp.arange(nb)))
    return o.swapaxes(0, 1).reshape(B, S, H * dh)


def token_mixers(x, cos, sin, w_in, lam_params, subln_g, w_bd, w_bs, w_o, lam_init):
    B, S, _ = x.shape
    h = x @ w_in
    points = [int(c) for c in np.cumsum(SPLIT_SIZES)[:-1]]
    dq, dkk, dv, sq, sk, sv, iq, ik, iw, ga, gb = jnp.split(h, points, axis=-1)

    dq = partial_rope(dq.reshape(B, S, DIFF_HEADS, 2, HEAD_DIM), cos, sin)
    dkk = partial_rope(dkk.reshape(B, S, DIFF_HEADS, 2, HEAD_DIM), cos, sin)
    dv = dv.reshape(B, S, DIFF_HEADS, 2 * HEAD_DIM)
    lp = lam_params.astype(jnp.float32)
    lam = jnp.exp(jnp.sum(lp[0] * lp[1])) - jnp.exp(jnp.sum(lp[2] * lp[3])) + lam_init
    ya = diff_attention(dq, dkk, dv, lam)
    ya = (rms_norm(ya, subln_g) * (1.0 - lam_init)).reshape(B, S, DIFF_WIDTH)

    sq = partial_rope(sq.reshape(B, S, DSA_HEADS, HEAD_DIM), cos, sin)
    sk = partial_rope(sk.reshape(B, S, DSA_HEADS, HEAD_DIM), cos, sin)
    sv = sv.reshape(B, S, DSA_HEADS, HEAD_DIM)
    iq = partial_rope(iq.reshape(B, S, IDX_HEADS, IDX_DIM), cos, sin)
    ik = partial_rope(ik, cos, sin)
    iw = iw * (IDX_HEADS ** -0.5)
    topk = min(TOPK_MAX, S // 4)
    yb = dsa_attention(sq, sk, sv, iq, ik, iw, topk)

    merged = jax.nn.sigmoid(ga) * (ya @ w_bd) + jax.nn.sigmoid(gb) * (yb @ w_bs)
    return merged @ w_o


def hierarchical_moe(x, w_rg, b_rg, w_re, b_re, w_g, w_u, w_d):
    B, S, D = x.shape
    xt = x.reshape(-1, D)
    n = xt.shape[0]
    g_prob = jax.nn.softmax((xt @ w_rg).astype(jnp.float32) + b_rg.astype(jnp.float32), axis=-1)
    g_val, g_idx = lax.top_k(g_prob, 1)
    e_logit = ((xt @ w_re).astype(jnp.float32) + b_re.astype(jnp.float32)).reshape(n, N_GROUPS, EXPERTS_PER_GROUP)
    e_logit = jnp.take_along_axis(e_logit, g_idx[:, :, None], axis=1)[:, 0]
    e_prob = jax.nn.softmax(e_logit, axis=-1)
    e_val, e_idx = lax.top_k(e_prob, TOP_K_IN_GROUP)
    e_val = e_val / jnp.sum(e_val, axis=-1, keepdims=True)
    within = jnp.sum(jax.nn.one_hot(e_idx, EXPERTS_PER_GROUP, dtype=jnp.float32) * e_val[..., None], axis=1)
    comb = jax.nn.one_hot(g_idx[:, 0], N_GROUPS, dtype=jnp.float32)[:, :, None] * (g_val * within)[:, None, :]
    y = jnp.zeros_like(xt)
    for g in range(N_GROUPS):
        hg = jax.nn.silu(jnp.einsum('nd,edf->nef', xt, w_g[g])) * jnp.einsum('nd,edf->nef', xt, w_u[g])
        y = y + jnp.einsum('nef,efd->nd', hg * comb[:, g, :, None].astype(hg.dtype), w_d[g])
    return y.reshape(B, S, D)


def setup_inputs(seed: int = 0) -> dict:
    key = jax.random.key(seed)
    ks = jax.random.split(key, 24)
    L, D = DEPTH, D_MODEL
    G, E, F = N_GROUPS, EXPERTS_PER_GROUP, D_EXPERT
    nrm = lambda k, shape, scale: jax.random.normal(k, shape, jnp.float32) * scale
    return {
        'x': nrm(ks[0], (BATCH, SEQ, D), 1.0),
        'p': nrm(ks[1], (L, BATCH, SEQ, PLE_DIM), 1.0),
        'positions': jnp.broadcast_to(jnp.arange(SEQ, dtype=jnp.int32), (BATCH, SEQ)),
        'w_in': nrm(ks[2], (L, D, W_IN_COLS), D ** -0.5),
        'diff_lambda': nrm(ks[3], (L, 4, HEAD_DIM), 0.1),
        'diff_subln_g': 1.0 + nrm(ks[4], (L, 2 * HEAD_DIM), 0.01),
        'w_branch_diff': nrm(ks[5], (L, DIFF_WIDTH, D), DIFF_WIDTH ** -0.5),
        'w_branch_dsa': nrm(ks[6], (L, DSA_WIDTH, D), DSA_WIDTH ** -0.5),
        'w_out': nrm(ks[7], (L, D, D), BETA * D ** -0.5),
        'ln1_g': 1.0 + nrm(ks[8], (L, D), 0.01),
        'ln1_b': nrm(ks[9], (L, D), 0.01),
        'w_route_group': nrm(ks[10], (L, D, G), D ** -0.5),
        'b_route_group': nrm(ks[11], (L, G), 0.01),
        'w_route_expert': nrm(ks[12], (L, D, G * E), D ** -0.5),
        'b_route_expert': nrm(ks[13], (L, G * E), 0.01),
        'w_exp_gate': nrm(ks[14], (L, G, E, D, F), D ** -0.5),
        'w_exp_up': nrm(ks[15], (L, G, E, D, F), D ** -0.5),
        'w_exp_down': nrm(ks[16], (L, G, E, F, D), BETA * F ** -0.5),
        'w_ple': nrm(ks[17], (L, PLE_DIM, D), BETA * PLE_DIM ** -0.5),
        'w_ple_gate': nrm(ks[18], (L, D, D), D ** -0.5),
        'ln2_g': 1.0 + nrm(ks[19], (L, D), 0.01),
        'ln2_b': nrm(ks[20], (L, D), 0.01),
    }


def reference(x, p, positions, w_in, diff_lambda, diff_subln_g, w_branch_diff, w_branch_dsa,
              w_out, ln1_g, ln1_b, w_route_group, b_route_group, w_route_expert, b_route_expert,
              w_exp_gate, w_exp_up, w_exp_down, w_ple, w_ple_gate, ln2_g, ln2_b):
    cos, sin = rope_tables(positions)
    for i in range(DEPTH):
        lam_init = 0.8 - 0.6 * math.exp(-0.3 * i)
        mix = token_mixers(x, cos, sin, w_in[i], diff_lambda[i], diff_subln_g[i],
                           w_branch_diff[i], w_branch_dsa[i], w_out[i], lam_init)
        x = layer_norm(ALPHA * x + mix, ln1_g[i], ln1_b[i])
        ffn = hierarchical_moe(x, w_route_group[i], b_route_group[i], w_route_expert[i],
                               b_route_expert[i], w_exp_gate[i], w_exp_up[i], w_exp_down[i])
        ple = jax.nn.sigmoid(x @ w_ple_gate[i]) * (p[i] @ w_ple[i])
        x = layer_norm(ALPHA * x + ffn + ple, ln2_g[i], ln2_b[i])
    return x
```

```python
import functools
import math

import jax
import jax.numpy as jnp
import numpy as np
from jax import lax
from jax.experimental import pallas as pl
from jax.experimental.pallas import tpu as pltpu

F32 = jnp.float32
BF16 = jnp.bfloat16

HEAD_DIM = 64
DIFF_HEADS = 4
DSA_HEADS = 8
IDX_HEADS = 8
TOPK_MAX = 256
N_GROUPS = 4
EXPERTS_PER_GROUP = 8
N_EXPERTS = N_GROUPS * EXPERTS_PER_GROUP
ROPE_THETA = 500000.0
ROT_DIM = HEAD_DIM // 4
ROT_HALF = ROT_DIM // 2
LN_EPS = 1e-5
NEG_INF = -1e30
LANES = 128
BRANCH_WIDTH = 512
VMEM_LIMIT = 62 * 1024 * 1024


def _float_key(v):
    bits = int(np.float32(v).view(np.int32))
    return bits if bits >= 0 else bits ^ 0x7FFFFFFF


KEY_LO_INIT = _float_key(NEG_INF)
KEY_HI_INIT = _float_key(np.inf)


def _dot_nt(a, b):
    return lax.dot_general(a, b, (((1,), (1,)), ((), ())), preferred_element_type=F32)


def _dot(a, b):
    return jnp.dot(a, b, preferred_element_type=F32)


def _sigmoid(v):
    return 1.0 / (1.0 + jnp.exp(-v))


def _layer_norm(v, g, b):
    mu = jnp.mean(v, axis=-1, keepdims=True)
    d = v - mu
    var = jnp.mean(d * d, axis=-1, keepdims=True)
    return d * lax.rsqrt(var + LN_EPS) * g + b


def _inproj_kernel(x_ref, w_ref, c_ref, sa_ref, sb_ref,
                   dq_ref, dk_ref, dv_ref, sq_ref, sk_ref, sv_ref, iq_ref, ik_ref, iw_ref):
    xb = x_ref[...].astype(BF16)
    cos_t = c_ref[...]
    sin_a = sa_ref[...]
    sin_b = sb_ref[...]

    def rope(h):
        cols = []
        for c in range(h.shape[1] // LANES):
            hc = h[:, c * LANES:(c + 1) * LANES]
            cols.append(hc * cos_t
                        + pltpu.roll(hc, LANES - ROT_HALF, 1) * sin_a
                        + pltpu.roll(hc, ROT_HALF, 1) * sin_b)
        return cols[0] if len(cols) == 1 else jnp.concatenate(cols, axis=1)

    outs = ((dq_ref, True), (dk_ref, True), (dv_ref, False), (sq_ref, True),
            (sk_ref, True), (sv_ref, False), (iq_ref, True))
    for i, (ref, rotary) in enumerate(outs):
        h = _dot(xb, w_ref[:, i * BRANCH_WIDTH:(i + 1) * BRANCH_WIDTH])
        if rotary:
            h = rope(h)
        ref[...] = h.astype(BF16)
    base = len(outs) * BRANCH_WIDTH
    h = _dot(xb, w_ref[:, base:base + 2 * LANES])
    ik_ref[...] = rope(h[:, :LANES]).astype(BF16)
    iw_ref[...] = h[:, LANES:] * (IDX_HEADS ** -0.5)


def _inproj(x2, w_a, cos_t, sin_a, sin_b, tm):
    n, d = x2.shape
    wc = w_a.shape[1]
    tok = lambda width: pl.BlockSpec((tm, width), lambda i: (i, 0))
    out_shape = [jax.ShapeDtypeStruct((n, BRANCH_WIDTH), BF16)] * 7 + [
        jax.ShapeDtypeStruct((n, LANES), BF16), jax.ShapeDtypeStruct((n, LANES), F32)]
    return pl.pallas_call(
        _inproj_kernel,
        grid=(n // tm,),
        in_specs=[tok(d), pl.BlockSpec((d, wc), lambda i: (0, 0)), tok(LANES), tok(LANES), tok(LANES)],
        out_specs=[tok(BRANCH_WIDTH)] * 7 + [tok(LANES), tok(LANES)],
        out_shape=out_shape,
        compiler_params=pltpu.CompilerParams(dimension_semantics=("arbitrary",),
                                             vmem_limit_bytes=VMEM_LIMIT),
        name="inproj",
    )(x2, w_a, cos_t, sin_a, sin_b)


def _diff_kernel(q_ref, k_ref, v_ref, lam_ref, g_ref, o_ref, m_ref, l_ref, acc_ref, *, tq, lam_init):
    qi = pl.program_id(2)
    q = q_ref[...]
    lane = lax.broadcasted_iota(jnp.int32, q.shape, 1)
    zero = jnp.zeros_like(q)
    qs = (jnp.where(lane < HEAD_DIM, q, zero), jnp.where(lane >= HEAD_DIM, q, zero))
    m_ref[...] = jnp.full(m_ref.shape, NEG_INF, F32)
    l_ref[...] = jnp.zeros(l_ref.shape, F32)
    acc_ref[...] = jnp.zeros(acc_ref.shape, F32)

    def step(j, causal):
        start = pl.multiple_of(j * tq, tq)
        kc = k_ref[pl.ds(start, tq), :]
        vc = v_ref[pl.ds(start, tq), :]
        for c in range(2):
            s = _dot_nt(qs[c], kc)
            if causal:
                row = lax.broadcasted_iota(jnp.int32, s.shape, 0)
                col = lax.broadcasted_iota(jnp.int32, s.shape, 1)
                s = jnp.where(col <= row, s, NEG_INF)
            m_prev = m_ref[c]
            m_new = jnp.maximum(m_prev, jnp.max(s, axis=-1, keepdims=True))
            alpha = jnp.exp(m_prev - m_new)
            p = jnp.exp(s - m_new)
            l_ref[c] = alpha * l_ref[c] + jnp.sum(p, axis=-1, keepdims=True)
            acc_ref[c] = alpha * acc_ref[c] + _dot(p.astype(BF16), vc)
            m_ref[c] = m_new

    def body(j, carry):
        step(j, False)
        return carry

    lax.fori_loop(0, qi, body, 0)
    step(qi, True)

    lp = lam_ref[...]
    lam = (jnp.exp(jnp.sum(lp[0:1] * lp[1:2], axis=-1, keepdims=True))
           - jnp.exp(jnp.sum(lp[2:3] * lp[3:4], axis=-1, keepdims=True)) + lam_init)
    o = acc_ref[0] / l_ref[0] - lam * (acc_ref[1] / l_ref[1])
    o = o * lax.rsqrt(jnp.mean(o * o, axis=-1, keepdims=True) + LN_EPS)
    o_ref[...] = (o * g_ref[...] * (1.0 - lam_init)).astype(BF16)


def _diff_attn(dq, dk, dv, lam_params, subln_g, batch, seq, tq, lam_init):
    nq = seq // tq
    hw = 2 * HEAD_DIM
    kv_spec = pl.BlockSpec((seq, hw), lambda b, h, i: (b, h))
    q_spec = pl.BlockSpec((tq, hw), lambda b, h, i: (b * nq + i, h))
    return pl.pallas_call(
        functools.partial(_diff_kernel, tq=tq, lam_init=lam_init),
        grid=(batch, DIFF_HEADS, nq),
        in_specs=[q_spec, kv_spec, kv_spec,
                  pl.BlockSpec(lam_params.shape, lambda b, h, i: (0, 0)),
                  pl.BlockSpec((1, hw), lambda b, h, i: (0, 0))],
        out_specs=q_spec,
        out_shape=jax.ShapeDtypeStruct(dq.shape, BF16),
        scratch_shapes=[pltpu.VMEM((2, tq, 1), F32), pltpu.VMEM((2, tq, 1), F32),
                        pltpu.VMEM((2, tq, hw), F32)],
        compiler_params=pltpu.CompilerParams(
            dimension_semantics=("arbitrary", "arbitrary", "arbitrary"),
            vmem_limit_bytes=VMEM_LIMIT),
        name="diff_attn",
    )(dq, dk, dv, lam_params, subln_g.reshape(1, hw))


def _key_to_float(key):
    bits = jnp.where(key >= 0, key, key ^ 0x7FFFFFFF)
    return lax.bitcast_convert_type(bits, F32)


def _dsa_kernel(sq_ref, iq_ref, iw_ref, ik_ref, sk_ref, sv_ref, o_ref,
                sc_ref, qm_ref, iqm_ref, wb_ref, m_ref, l_ref, acc_ref, *, tq, topk):
    qi = pl.program_id(1)
    nch = qi + 1
    lane = lax.broadcasted_iota(jnp.int32, (tq, LANES), 1)
    low_half = lane < HEAD_DIM
    row = lax.broadcasted_iota(jnp.int32, (tq, tq), 0)
    col = lax.broadcasted_iota(jnp.int32, (tq, tq), 1)
    causal = col <= row

    iw = iw_ref[...]
    for h in range(DSA_HEADS):
        pair = h // 2
        keep = low_half if h % 2 == 0 else jnp.logical_not(low_half)
        sq_pair = sq_ref[:, pair * LANES:(pair + 1) * LANES]
        iq_pair = iq_ref[:, pair * LANES:(pair + 1) * LANES]
        qm_ref[h] = jnp.where(keep, sq_pair, jnp.zeros_like(sq_pair))
        iqm_ref[h] = jnp.where(keep, iq_pair, jnp.zeros_like(iq_pair))
        wb_ref[h] = jnp.broadcast_to(iw[:, h:h + 1], (tq, LANES))

    def score_chunk(j, masked):
        start = pl.multiple_of(j * tq, tq)
        kc = ik_ref[pl.ds(start, tq), :]
        parts = [jnp.zeros((tq, LANES), F32) for _ in range(tq // LANES)]
        for h in range(IDX_HEADS):
            d = _dot_nt(iqm_ref[h], kc)
            w = wb_ref[h]
            for c in range(tq // LANES):
                parts[c] = parts[c] + w * jnp.maximum(d[:, c * LANES:(c + 1) * LANES], 0.0)
        sc = jnp.concatenate(parts, axis=1) + 0.0
        if masked:
            sc = jnp.where(causal, sc, NEG_INF)
        sc_ref[j] = sc

    def score_body(j, carry):
        score_chunk(j, False)
        return carry

    lax.fori_loop(0, qi, score_body, 0)
    score_chunk(qi, True)

    def count_where(pred):
        def body(j, cnt):
            hit = pred(sc_ref[j], j)
            for c in range(tq // LANES):
                cnt = cnt + jnp.where(hit[:, c * LANES:(c + 1) * LANES], 1.0, 0.0)
            return cnt
        cnt = lax.fori_loop(0, nch, body, jnp.zeros((tq, LANES), F32))
        return jnp.sum(cnt, axis=-1, keepdims=True)

    row_pos = qi * tq + lax.broadcasted_iota(jnp.int32, (tq, 1), 0)
    kk = jnp.minimum(row_pos + 1, topk).astype(F32)

    def unresolved(lo, hi, flo):
        return jnp.logical_and(flo != kk, hi - 1 > lo)

    def bis_cond(st):
        return st[0] > 0.0

    def bis_body(st):
        _, lo, hi, flo, fhi = st
        mid = (lo >> 1) + (hi >> 1) + (lo & hi & 1)
        thr = _key_to_float(mid)
        cnt = count_where(lambda blk, j: blk > thr)
        less = cnt < kk
        hi = jnp.where(less, mid, hi)
        fhi = jnp.where(less, cnt, fhi)
        lo = jnp.where(less, lo, mid)
        flo = jnp.where(less, flo, cnt)
        todo = jnp.max(jnp.where(unresolved(lo, hi, flo), 1.0, 0.0))
        return todo, lo, hi, flo, fhi

    lo0 = jnp.full((tq, 1), KEY_LO_INIT, jnp.int32)
    hi0 = jnp.full((tq, 1), KEY_HI_INIT, jnp.int32)
    flo0 = count_where(lambda blk, j: blk > NEG_INF)
    fhi0 = jnp.zeros((tq, 1), F32)
    todo0 = jnp.max(jnp.where(unresolved(lo0, hi0, flo0), 1.0, 0.0))
    _, lo, hi, flo, fhi = lax.while_loop(bis_cond, bis_body, (todo0, lo0, hi0, flo0, fhi0))

    tie_row = flo > kk
    thr_gt = _key_to_float(jnp.where(tie_row, hi, lo))
    tie_val = _key_to_float(hi)
    need = kk - fhi
    any_tie = jnp.max(jnp.where(tie_row, 1.0, 0.0))

    def key_index(j):
        return j * tq + lax.broadcasted_iota(jnp.int32, (tq, tq), 1)

    def tie_search(_):
        def body(_, st):
            ilo, ihi = st
            mid = (ilo + ihi) >> 1
            cnt = count_where(lambda blk, j: jnp.logical_and(blk == tie_val, key_index(j) <= mid))
            enough = cnt >= need
            return jnp.where(enough, ilo, mid), jnp.where(enough, mid, ihi)
        ilo0 = jnp.full((tq, 1), -1, jnp.int32)
        ihi0 = jnp.broadcast_to(qi * tq + tq - 1, (tq, 1)).astype(jnp.int32)
        n_iter = max(1, int(math.ceil(math.log2(sc_ref.shape[0] * tq))) + 1)
        _, ihi = lax.fori_loop(0, n_iter, body, (ilo0, ihi0))
        return jnp.where(tie_row, ihi, -1)

    tie_idx = lax.cond(any_tie > 0.0, tie_search, lambda _: jnp.full((tq, 1), -1, jnp.int32), 0)

    m_ref[...] = jnp.full(m_ref.shape, NEG_INF, F32)
    l_ref[...] = jnp.zeros(l_ref.shape, F32)
    acc_ref[...] = jnp.zeros(acc_ref.shape, F32)

    def attend(j, with_ties):
        start = pl.multiple_of(j * tq, tq)
        blk = sc_ref[j]
        sel = blk > thr_gt
        if with_ties:
            sel = jnp.logical_or(sel, jnp.logical_and(blk == tie_val, key_index(j) <= tie_idx))
        for pair in range(DSA_HEADS // 2):
            kc = sk_ref[pl.ds(start, tq), pl.ds(pair * LANES, LANES)]
            vc = sv_ref[pl.ds(start, tq), pl.ds(pair * LANES, LANES)]
            alphas, pvs = [], []
            for h in (2 * pair, 2 * pair + 1):
                s = jnp.where(sel, _dot_nt(qm_ref[h], kc), NEG_INF)
                m_prev = m_ref[h]
                m_new = jnp.maximum(m_prev, jnp.max(s, axis=-1, keepdims=True))
                alpha = jnp.exp(m_prev - m_new)
                p = jnp.exp(s - m_new)
                l_ref[h] = alpha * l_ref[h] + jnp.sum(p, axis=-1, keepdims=True)
                m_ref[h] = m_new
                alphas.append(alpha)
                pvs.append(_dot(p.astype(BF16), vc))
            alpha_pair = jnp.where(low_half, alphas[0], alphas[1])
            pv_pair = jnp.where(low_half, pvs[0], pvs[1])
            acc_ref[pair] = alpha_pair * acc_ref[pair] + pv_pair

    def attend_loop(with_ties):
        def run(_):
            def body(j, carry):
                attend(j, with_ties)
                return carry
            lax.fori_loop(0, nch, body, 0)
            return 0
        return run

    lax.cond(any_tie > 0.0, attend_loop(True), attend_loop(False), 0)

    for pair in range(DSA_HEADS // 2):
        l_pair = jnp.where(low_half, l_ref[2 * pair], l_ref[2 * pair + 1])
        o_ref[:, pair * LANES:(pair + 1) * LANES] = (acc_ref[pair] / l_pair).astype(BF16)


def _dsa_attn(sq, sk, sv, iq, ik2, iw, batch, seq, tq, topk):
    nq = seq // tq
    q_spec = lambda width: pl.BlockSpec((tq, width), lambda b, i: (b * nq + i, 0))
    resident = lambda width: pl.BlockSpec((seq, width), lambda b, i: (b, 0),
                                          pipeline_mode=pl.Buffered(1))
    return pl.pallas_call(
        functools.partial(_dsa_kernel, tq=tq, topk=topk),
        grid=(batch, nq),
        in_specs=[q_spec(BRANCH_WIDTH), q_spec(BRANCH_WIDTH), q_spec(LANES),
                  resident(LANES), resident(BRANCH_WIDTH), resident(BRANCH_WIDTH)],
        out_specs=q_spec(BRANCH_WIDTH),
        out_shape=jax.ShapeDtypeStruct(sq.shape, BF16),
        scratch_shapes=[pltpu.VMEM((nq, tq, tq), F32),
                        pltpu.VMEM((DSA_HEADS, tq, LANES), BF16),
                        pltpu.VMEM((IDX_HEADS, tq, LANES), BF16),
                        pltpu.VMEM((IDX_HEADS, tq, LANES), F32),
                        pltpu.VMEM((DSA_HEADS, tq, 1), F32),
                        pltpu.VMEM((DSA_HEADS, tq, 1), F32),
                        pltpu.VMEM((DSA_HEADS // 2, tq, LANES), F32)],
        compiler_params=pltpu.CompilerParams(
            dimension_semantics=("arbitrary", "arbitrary"),
            vmem_limit_bytes=VMEM_LIMIT),
        name="dsa_attn",
    )(sq, iq, iw, ik2, sk, sv)


def _merge_kernel(x_ref, ya_ref, yb_ref, p_ref, wg_ref, wbd_ref, wbs_ref, wo_ref, g1_ref, b1_ref,
                  wr_ref, br_ref, wpg_ref, wp_ref, x1_ref, res_ref, comb_ref, *, alpha, d_model):
    x = x_ref[...]
    gates = _dot(x.astype(BF16), wg_ref[...])
    merged = (_sigmoid(gates[:, :d_model]) * _dot(ya_ref[...], wbd_ref[...])
              + _sigmoid(gates[:, d_model:]) * _dot(yb_ref[...], wbs_ref[...]))
    mix = _dot(merged.astype(BF16), wo_ref[...])
    x1 = _layer_norm(alpha * x + mix, g1_ref[...], b1_ref[...])
    x1b = x1.astype(BF16)
    x1_ref[...] = x1b
    ple = _sigmoid(_dot(x1b, wpg_ref[...])) * _dot(p_ref[...].astype(BF16), wp_ref[...])
    res_ref[...] = alpha * x1 + ple

    logits = _dot(x1b, wr_ref[...]) + br_ref[...]
    lane = lax.broadcasted_iota(jnp.int32, logits.shape, 1)
    is_group = jnp.logical_and(lane >= N_EXPERTS, lane < N_EXPERTS + N_GROUPS)
    gl = jnp.where(is_group, logits, NEG_INF)
    gmax = jnp.max(gl, axis=-1, keepdims=True)
    gsum = jnp.sum(jnp.where(is_group, jnp.exp(gl - gmax), 0.0), axis=-1, keepdims=True)
    g_val = 1.0 / gsum
    g_idx = jnp.min(jnp.where(jnp.logical_and(is_group, gl == gmax), lane, 4 * LANES),
                    axis=-1, keepdims=True) - N_EXPERTS
    first = g_idx * EXPERTS_PER_GROUP
    in_group = jnp.logical_and(lane >= first, lane < first + EXPERTS_PER_GROUP)
    el = jnp.where(in_group, logits, NEG_INF)
    e1 = jnp.max(el, axis=-1, keepdims=True)
    i1 = jnp.min(jnp.where(jnp.logical_and(in_group, el == e1), lane, 4 * LANES), axis=-1, keepdims=True)
    el2 = jnp.where(lane == i1, NEG_INF, el)
    e2 = jnp.max(el2, axis=-1, keepdims=True)
    i2 = jnp.min(jnp.where(jnp.logical_and(in_group, el2 == e2), lane, 4 * LANES), axis=-1, keepdims=True)
    p2 = jnp.exp(e2 - e1)
    w1 = 1.0 / (1.0 + p2)
    w2 = p2 / (1.0 + p2)
    comb_ref[...] = g_val * (jnp.where(lane == i1, w1, 0.0) + jnp.where(lane == i2, w2, 0.0))


def _merge(x2, ya, yb, p2, w_g, w_bd, w_bs, w_o, g1, b1, w_r, b_r, w_pg, w_p, tm, alpha):
    n, d = x2.shape
    tok = lambda width: pl.BlockSpec((tm, width), lambda i: (i, 0))
    full = lambda a: pl.BlockSpec(a.shape, lambda i: (0, 0))
    weights = (w_g, w_bd, w_bs, w_o, g1, b1, w_r, b_r, w_pg, w_p)
    return pl.pallas_call(
        functools.partial(_merge_kernel, alpha=alpha, d_model=d),
        grid=(n // tm,),
        in_specs=[tok(d), tok(BRANCH_WIDTH), tok(BRANCH_WIDTH), tok(p2.shape[1])] + [full(w) for w in weights],
        out_specs=[tok(d), tok(d), tok(LANES)],
        out_shape=[jax.ShapeDtypeStruct((n, d), BF16), jax.ShapeDtypeStruct((n, d), F32),
                   jax.ShapeDtypeStruct((n, LANES), F32)],
        compiler_params=pltpu.CompilerParams(dimension_semantics=("arbitrary",),
                                             vmem_limit_bytes=VMEM_LIMIT),
        name="merge",
    )(x2, ya, yb, p2, *weights)


def _moe_kernel(x1_ref, res_ref, comb_ref, wgu_ref, wd_ref, g2_ref, b2_ref, o_ref, acc_ref, *, d_expert):
    e = pl.program_id(1)

    @pl.when(e == 0)
    def _():
        acc_ref[...] = jnp.zeros(acc_ref.shape, F32)

    gu = _dot(x1_ref[...], wgu_ref[0])
    comb = comb_ref[...]
    lane = lax.broadcasted_iota(jnp.int32, comb.shape, 1)
    ce = jnp.sum(jnp.where(lane == e, comb, 0.0), axis=-1, keepdims=True)
    g = gu[:, :d_expert]
    h = g * _sigmoid(g) * gu[:, d_expert:] * ce
    acc_ref[...] += _dot(h.astype(BF16), wd_ref[0])

    @pl.when(e == pl.num_programs(1) - 1)
    def _():
        o_ref[...] = _layer_norm(res_ref[...] + acc_ref[...], g2_ref[...], b2_ref[...])


def _moe(x1b, res, comb, w_gu, w_d, g2, b2, tm):
    n, d = x1b.shape
    n_exp, _, two_f = w_gu.shape
    tok = lambda width: pl.BlockSpec((tm, width), lambda i, e: (i, 0))
    return pl.pallas_call(
        functools.partial(_moe_kernel, d_expert=two_f // 2),
        grid=(n // tm, n_exp),
        in_specs=[tok(d), tok(d), tok(LANES),
                  pl.BlockSpec((1, d, two_f), lambda i, e: (e, 0, 0)),
                  pl.BlockSpec((1, two_f // 2, d), lambda i, e: (e, 0, 0)),
                  pl.BlockSpec((1, d), lambda i, e: (0, 0)),
                  pl.BlockSpec((1, d), lambda i, e: (0, 0))],
        out_specs=tok(d),
        out_shape=jax.ShapeDtypeStruct((n, d), F32),
        scratch_shapes=[pltpu.VMEM((tm, d), F32)],
        compiler_params=pltpu.CompilerParams(dimension_semantics=("arbitrary", "arbitrary"),
                                             vmem_limit_bytes=VMEM_LIMIT),
        name="moe",
    )(x1b, res, comb, w_gu, w_d, g2, b2)


def _rope_tables(positions):
    inv_freq = 1.0 / (ROPE_THETA ** (jnp.arange(0, ROT_DIM, 2, dtype=F32) / ROT_DIM))
    ang = positions.astype(F32)[..., None] * inv_freq
    cos, sin = jnp.cos(ang), jnp.sin(ang)
    n = positions.size
    cos, sin = cos.reshape(n, ROT_HALF), sin.reshape(n, ROT_HALF)
    rest = HEAD_DIM - ROT_DIM
    cos_t = jnp.concatenate([cos, cos, jnp.ones((n, rest), F32)], axis=1)
    sin_a = jnp.concatenate([-sin, jnp.zeros((n, rest + ROT_HALF), F32)], axis=1)
    sin_b = jnp.concatenate([jnp.zeros((n, ROT_HALF), F32), sin, jnp.zeros((n, rest), F32)], axis=1)
    rep = LANES // HEAD_DIM
    return tuple(jnp.tile(t, (1, rep)) for t in (cos_t, sin_a, sin_b))


def _tile(n, want):
    t = min(n, want)
    assert n % t == 0, (n, t)
    return t


def kernel(x, p, positions, w_in, diff_lambda, diff_subln_g, w_branch_diff, w_branch_dsa, w_out, ln1_g, ln1_b, w_route_group, b_route_group, w_route_expert, b_route_expert, w_exp_gate, w_exp_up, w_exp_down, w_ple, w_ple_gate, ln2_g, ln2_b):
    batch, seq, d = x.shape
    depth = w_in.shape[0]
    n = batch * seq
    alpha = (2 * depth) ** 0.25
    topk = min(TOPK_MAX, seq // 4)
    d_expert = w_exp_gate.shape[-1]
    assert BRANCH_WIDTH == DIFF_HEADS * 2 * HEAD_DIM == DSA_HEADS * HEAD_DIM == IDX_HEADS * HEAD_DIM
    n_qkv = 7 * BRANCH_WIDTH
    o_ik, o_iw = n_qkv, n_qkv + HEAD_DIM
    o_ga = o_iw + IDX_HEADS
    assert w_in.shape[2] == o_ga + 2 * d

    tm_proj = _tile(n, 512)
    tq_diff = _tile(seq, 512)
    tq_dsa = _tile(seq, 256)
    tm_merge = _tile(n, 256)
    tm_moe = _tile(n, 1024)

    cos_t, sin_a, sin_b = _rope_tables(positions)
    scale = HEAD_DIM ** -0.5
    col_scale = jnp.ones((n_qkv,), F32)
    for seg in (0, 3, 6):
        col_scale = col_scale.at[seg * BRANCH_WIDTH:(seg + 1) * BRANCH_WIDTH].set(scale)

    x2 = x.reshape(n, d)
    for i in range(depth):
        lam_init = 0.8 - 0.6 * math.exp(-0.3 * i)
        wi = w_in[i]
        w_ik = wi[:, o_ik:o_ik + HEAD_DIM]
        w_iw = jnp.pad(wi[:, o_iw:o_iw + IDX_HEADS], ((0, 0), (0, LANES - IDX_HEADS)))
        w_a = jnp.concatenate([wi[:, :n_qkv] * col_scale, w_ik, w_ik, w_iw], axis=1).astype(BF16)
        w_g = wi[:, o_ga:].astype(BF16)
        w_r = jnp.pad(jnp.concatenate([w_route_expert[i], w_route_group[i]], axis=1),
                      ((0, 0), (0, LANES - N_EXPERTS - N_GROUPS))).astype(BF16)
        b_r = jnp.pad(jnp.concatenate([b_route_expert[i], b_route_group[i]]),
                      (0, LANES - N_EXPERTS - N_GROUPS)).reshape(1, LANES)
        w_gu = jnp.concatenate([w_exp_gate[i], w_exp_up[i]], axis=-1).reshape(
            N_EXPERTS, d, 2 * d_expert).astype(BF16)
        w_d = w_exp_down[i].reshape(N_EXPERTS, d_expert, d).astype(BF16)

        dq, dk, dv, sq, sk, sv, iq, ik2, iw = _inproj(x2, w_a, cos_t, sin_a, sin_b, tm_proj)
        ya = _diff_attn(dq, dk, dv, diff_lambda[i], diff_subln_g[i], batch, seq, tq_diff, lam_init)
        yb = _dsa_attn(sq, sk, sv, iq, ik2, iw, batch, seq, tq_dsa, topk)
        x1b, res, comb = _merge(
            x2, ya, yb, p[i].reshape(n, -1), w_g,
            w_branch_diff[i].astype(BF16), w_branch_dsa[i].astype(BF16), w_out[i].astype(BF16),
            ln1_g[i].reshape(1, d), ln1_b[i].reshape(1, d), w_r, b_r,
            w_ple_gate[i].astype(BF16), w_ple[i].astype(BF16), tm_merge, alpha)
        x2 = _moe(x1b, res, comb, w_gu, w_d, ln2_g[i].reshape(1, d), ln2_b[i].reshape(1, d), tm_moe)
    return x2.reshape(batch, seq, d)
```

```python
import functools
import math

import jax
import jax.numpy as jnp
import numpy as np
from jax import lax
from jax.experimental import pallas as pl
from jax.experimental.pallas import tpu as pltpu

F32 = jnp.float32
BF16 = jnp.bfloat16

HEAD_DIM = 64
DIFF_HEADS = 4
DSA_HEADS = 8
IDX_HEADS = 8
TOPK_MAX = 256
N_GROUPS = 4
EXPERTS_PER_GROUP = 8
N_EXPERTS = N_GROUPS * EXPERTS_PER_GROUP
ROPE_THETA = 500000.0
ROT_DIM = HEAD_DIM // 4
ROT_HALF = ROT_DIM // 2
LN_EPS = 1e-5
NEG_INF = -1e30
LOG2E = math.log2(math.e)
LANES = 128
BRANCH_WIDTH = 512
VMEM_LIMIT = 62 * 1024 * 1024


def _float_key(v):
    bits = int(np.float32(v).view(np.int32))
    return bits if bits >= 0 else bits ^ 0x7FFFFFFF


KEY_LO_INIT = _float_key(NEG_INF)
KEY_HI_INIT = _float_key(np.inf)


def _dot_nt(a, b):
    return lax.dot_general(a, b, (((1,), (1,)), ((), ())), preferred_element_type=F32)


def _dot(a, b):
    return jnp.dot(a, b, preferred_element_type=F32)


def _sigmoid(v):
    return 1.0 / (1.0 + jnp.exp(-v))


def _layer_norm(v, g, b):
    mu = jnp.mean(v, axis=-1, keepdims=True)
    d = v - mu
    var = jnp.mean(d * d, axis=-1, keepdims=True)
    return d * lax.rsqrt(var + LN_EPS) * g + b


def _inproj_kernel(x_ref, w_ref, c_ref, sa_ref, sb_ref,
                   dq_ref, dk_ref, dv_ref, sq_ref, sk_ref, sv_ref, iq_ref, ik_ref, iw_ref):
    xb = x_ref[...].astype(BF16)
    cos_t = c_ref[...]
    sin_a = sa_ref[...]
    sin_b = sb_ref[...]

    def rope(h):
        cols = []
        for c in range(h.shape[1] // LANES):
            hc = h[:, c * LANES:(c + 1) * LANES]
            cols.append(hc * cos_t
                        + pltpu.roll(hc, LANES - ROT_HALF, 1) * sin_a
                        + pltpu.roll(hc, ROT_HALF, 1) * sin_b)
        return cols[0] if len(cols) == 1 else jnp.concatenate(cols, axis=1)

    outs = ((dq_ref, True, LOG2E), (dk_ref, True, None), (dv_ref, False, None), (sq_ref, True, LOG2E),
            (sk_ref, True, None), (sv_ref, False, None), (iq_ref, True, None))
    for i, (ref, rotary, mult) in enumerate(outs):
        h = _dot(xb, w_ref[:, i * BRANCH_WIDTH:(i + 1) * BRANCH_WIDTH])
        if mult is not None:
            h = h * mult
        if rotary:
            h = rope(h)
        ref[...] = h.astype(BF16)
    base = len(outs) * BRANCH_WIDTH
    h = _dot(xb, w_ref[:, base:base + 2 * LANES])
    ik_ref[...] = rope(h[:, :LANES]).astype(BF16)
    iw_ref[...] = h[:, LANES:] * (IDX_HEADS ** -0.5)


def _inproj(x2, w_a, cos_t, sin_a, sin_b, tm):
    n, d = x2.shape
    wc = w_a.shape[1]
    tok = lambda width: pl.BlockSpec((tm, width), lambda i: (i, 0))
    out_shape = [jax.ShapeDtypeStruct((n, BRANCH_WIDTH), BF16)] * 7 + [
        jax.ShapeDtypeStruct((n, LANES), BF16), jax.ShapeDtypeStruct((n, LANES), F32)]
    return pl.pallas_call(
        _inproj_kernel,
        grid=(n // tm,),
        in_specs=[tok(d), pl.BlockSpec((d, wc), lambda i: (0, 0)), tok(LANES), tok(LANES), tok(LANES)],
        out_specs=[tok(BRANCH_WIDTH)] * 7 + [tok(LANES), tok(LANES)],
        out_shape=out_shape,
        compiler_params=pltpu.CompilerParams(dimension_semantics=("arbitrary",),
                                             vmem_limit_bytes=VMEM_LIMIT),
        name="inproj",
    )(x2, w_a, cos_t, sin_a, sin_b)


def _lane_cols(s):
    return [s[:, c * LANES:(c + 1) * LANES] for c in range(s.shape[1] // LANES)]


def _softmax_step(cols, vc, m_ref, l_ref, idx):
    mx = functools.reduce(jnp.maximum, cols)
    m_prev = m_ref[idx]
    m_new = jnp.maximum(m_prev, jnp.max(mx, axis=-1, keepdims=True))
    alpha = jnp.exp2(m_prev - m_new)
    ps = [jnp.exp2(c - m_new) for c in cols]
    l_ref[idx] = alpha * l_ref[idx] + functools.reduce(jnp.add, ps)
    m_ref[idx] = m_new
    p = ps[0] if len(ps) == 1 else jnp.concatenate(ps, axis=1)
    return alpha, _dot(p.astype(BF16), vc)


def _diff_kernel(q_ref, k_ref, v_ref, lam_ref, g_ref, o_ref, qm_ref, m_ref, l_ref, acc_ref, *,
                 tq, tk, lam_init):
    qi = pl.program_id(2)
    q = q_ref[...]
    lane = lax.broadcasted_iota(jnp.int32, q.shape, 1)
    zero = jnp.zeros_like(q)
    qm_ref[0] = jnp.where(lane < HEAD_DIM, q, zero)
    qm_ref[1] = jnp.where(lane >= HEAD_DIM, q, zero)
    m_ref[...] = jnp.full(m_ref.shape, NEG_INF, F32)
    l_ref[...] = jnp.zeros(l_ref.shape, F32)
    acc_ref[...] = jnp.zeros(acc_ref.shape, F32)

    def step(j, causal):
        start = pl.multiple_of(j * tk, tk)
        kc = k_ref[pl.ds(start, tk), :]
        vc = v_ref[pl.ds(start, tk), :]
        for c in range(2):
            s = _dot_nt(qm_ref[c], kc)
            if causal:
                row = qi * tq + lax.broadcasted_iota(jnp.int32, s.shape, 0)
                col = j * tk + lax.broadcasted_iota(jnp.int32, s.shape, 1)
                s = jnp.where(col <= row, s, NEG_INF)
            alpha, pv = _softmax_step(_lane_cols(s), vc, m_ref, l_ref, c)
            acc_ref[c] = alpha * acc_ref[c] + pv

    def body(j, carry):
        step(j, False)
        return carry

    n_full = (qi * tq) // tk
    lax.fori_loop(0, n_full, body, 0)
    step(n_full, True)

    lp = lam_ref[...]
    lam = (jnp.exp(jnp.sum(lp[0:1] * lp[1:2], axis=-1, keepdims=True))
           - jnp.exp(jnp.sum(lp[2:3] * lp[3:4], axis=-1, keepdims=True)) + lam_init)
    l0 = jnp.sum(l_ref[0], axis=-1, keepdims=True)
    l1 = jnp.sum(l_ref[1], axis=-1, keepdims=True)
    o = acc_ref[0] / l0 - lam * (acc_ref[1] / l1)
    o = o * lax.rsqrt(jnp.mean(o * o, axis=-1, keepdims=True) + LN_EPS)
    o_ref[...] = (o * g_ref[...] * (1.0 - lam_init)).astype(BF16)


def _diff_attn(dq, dk, dv, lam_params, subln_g, batch, seq, tq, tk, lam_init):
    assert tk % tq == 0 and seq % tk == 0
    nq = seq // tq
    hw = 2 * HEAD_DIM
    kv_spec = pl.BlockSpec((seq, hw), lambda b, h, i: (b, h))
    q_spec = pl.BlockSpec((tq, hw), lambda b, h, i: (b * nq + i, h))
    return pl.pallas_call(
        functools.partial(_diff_kernel, tq=tq, tk=tk, lam_init=lam_init),
        grid=(batch, DIFF_HEADS, nq),
        in_specs=[q_spec, kv_spec, kv_spec,
                  pl.BlockSpec(lam_params.shape, lambda b, h, i: (0, 0)),
                  pl.BlockSpec((1, hw), lambda b, h, i: (0, 0))],
        out_specs=q_spec,
        out_shape=jax.ShapeDtypeStruct(dq.shape, BF16),
        scratch_shapes=[pltpu.VMEM((2, tq, hw), BF16), pltpu.VMEM((2, tq, LANES), F32),
                        pltpu.VMEM((2, tq, LANES), F32), pltpu.VMEM((2, tq, hw), F32)],
        compiler_params=pltpu.CompilerParams(
            dimension_semantics=("arbitrary", "arbitrary", "arbitrary"),
            vmem_limit_bytes=VMEM_LIMIT),
        name="diff_attn",
    )(dq, dk, dv, lam_params, subln_g.reshape(1, hw))


def _key_to_float(key):
    bits = jnp.where(key >= 0, key, key ^ 0x7FFFFFFF)
    return lax.bitcast_convert_type(bits, F32)


def _float_to_key(v):
    bits = lax.bitcast_convert_type(v, jnp.int32)
    return jnp.where(bits >= 0, bits, bits ^ 0x7FFFFFFF)


def _dsa_kernel(sq_ref, iq_ref, iw_ref, ik_ref, sk_ref, sv_ref, o_ref,
                sc_ref, qm_ref, iqm_ref, wb_ref, m_ref, l_ref, acc_ref, *, tq, cps, topk):
    qi = pl.program_id(1)
    nch = qi + 1
    lane = lax.broadcasted_iota(jnp.int32, (tq, LANES), 1)
    low_half = lane < HEAD_DIM
    row = lax.broadcasted_iota(jnp.int32, (tq, tq), 0)
    col = lax.broadcasted_iota(jnp.int32, (tq, tq), 1)
    causal = col <= row

    iw = iw_ref[...]
    for h in range(DSA_HEADS):
        pair = h // 2
        keep = low_half if h % 2 == 0 else jnp.logical_not(low_half)
        sq_pair = sq_ref[:, pair * LANES:(pair + 1) * LANES]
        iq_pair = iq_ref[:, pair * LANES:(pair + 1) * LANES]
        qm_ref[h] = jnp.where(keep, sq_pair, jnp.zeros_like(sq_pair))
        iqm_ref[h] = jnp.where(keep, iq_pair, jnp.zeros_like(iq_pair))
        wb_ref[h] = jnp.broadcast_to(iw[:, h:h + 1], (tq, LANES))

    def score_chunk(j, masked, top2):
        start = pl.multiple_of(j * tq, tq)
        kc = ik_ref[pl.ds(start, tq), :]
        parts = [jnp.zeros((tq, LANES), F32) for _ in range(tq // LANES)]
        for h in range(IDX_HEADS):
            d = _dot_nt(iqm_ref[h], kc)
            w = wb_ref[h]
            for c in range(tq // LANES):
                parts[c] = parts[c] + w * jnp.maximum(d[:, c * LANES:(c + 1) * LANES], 0.0)
        sc = jnp.concatenate(parts, axis=1) + 0.0
        if masked:
            sc = jnp.where(causal, sc, NEG_INF)
        sc_ref[j] = sc
        max1, max2 = top2
        for part in _lane_cols(sc):
            max2 = jnp.maximum(max2, jnp.minimum(max1, part))
            max1 = jnp.maximum(max1, part)
        return max1, max2

    neg = jnp.full((tq, LANES), NEG_INF, F32)
    top2 = lax.fori_loop(0, qi, lambda j, t: score_chunk(j, False, t), (neg, neg))
    max1, max2 = score_chunk(qi, True, top2)

    n_steps = (qi + cps) // cps
    for t in range(1, cps):
        @pl.when(qi + t < n_steps * cps)
        def _():
            sc_ref[qi + t] = jnp.full((tq, tq), NEG_INF, F32)

    def count_where(make_pred):
        band = min(tq, LANES)
        counts = []
        for r0 in range(0, tq, band):
            rows = slice(r0, r0 + band)
            pred = make_pred(rows)

            def body(j, cnt):
                for hit in pred(sc_ref[j, rows, :], j):
                    cnt = cnt + jnp.where(hit, 1.0, 0.0)
                return cnt
            cnt = lax.fori_loop(0, nch, body, jnp.zeros((band, LANES), F32))
            counts.append(jnp.sum(cnt, axis=-1, keepdims=True))
        return counts[0] if len(counts) == 1 else jnp.concatenate(counts, axis=0)

    def count_above(thr):
        def make_pred(rows):
            thr_b = jnp.broadcast_to(thr[rows], (rows.stop - rows.start, LANES))
            return lambda blk, j: [b > thr_b for b in _lane_cols(blk)]
        return count_where(make_pred)

    row_pos = qi * tq + lax.broadcasted_iota(jnp.int32, (tq, 1), 0)
    kk = jnp.minimum(row_pos + 1, topk).astype(F32)

    def unresolved(lo, hi, flo):
        return jnp.logical_and(flo != kk, hi - 1 > lo)

    def bis_cond(st):
        return st[0] > 0.0

    def bis_body(st):
        _, lo, hi, flo, fhi = st
        mid = (lo >> 1) + (hi >> 1) + (lo & hi & 1)
        thr = _key_to_float(mid)
        cnt = count_above(thr)
        less = cnt < kk
        hi = jnp.where(less, mid, hi)
        fhi = jnp.where(less, cnt, fhi)
        lo = jnp.where(less, lo, mid)
        flo = jnp.where(less, flo, cnt)
        todo = jnp.max(jnp.where(unresolved(lo, hi, flo), 1.0, 0.0))
        return todo, lo, hi, flo, fhi

    hi0 = _float_to_key(jnp.max(max1, axis=-1, keepdims=True))
    lo0 = jnp.maximum(_float_to_key(jnp.min(max2, axis=-1, keepdims=True)) - 1, KEY_LO_INIT)
    thr0 = _key_to_float(lo0)
    flo0 = count_above(thr0)
    fhi0 = jnp.zeros((tq, 1), F32)
    todo0 = jnp.max(jnp.where(unresolved(lo0, hi0, flo0), 1.0, 0.0))
    _, lo, hi, flo, fhi = lax.while_loop(bis_cond, bis_body, (todo0, lo0, hi0, flo0, fhi0))

    tie_row = flo > kk
    thr_gt = _key_to_float(jnp.where(tie_row, hi, lo))
    tie_val = _key_to_float(hi)
    need = kk - fhi
    any_tie = jnp.max(jnp.where(tie_row, 1.0, 0.0))

    def count_ties_upto(limit):
        def make_pred(rows):
            shape = (rows.stop - rows.start, LANES)
            val_b = jnp.broadcast_to(tie_val[rows], shape)
            lim_b = jnp.broadcast_to(limit[rows], shape)
            lane_b = lax.broadcasted_iota(jnp.int32, shape, 1)
            return lambda blk, j: [
                jnp.logical_and(b == val_b, j * tq + c * LANES + lane_b <= lim_b)
                for c, b in enumerate(_lane_cols(blk))]
        return count_where(make_pred)

    def tie_search(_):
        def body(_, st):
            ilo, ihi = st
            mid = (ilo + ihi) >> 1
            cnt = count_ties_upto(mid)
            enough = cnt >= need
            return jnp.where(enough, ilo, mid), jnp.where(enough, mid, ihi)
        ilo0 = jnp.full((tq, 1), -1, jnp.int32)
        ihi0 = jnp.broadcast_to(qi * tq + tq - 1, (tq, 1)).astype(jnp.int32)
        n_iter = max(1, int(math.ceil(math.log2(sc_ref.shape[0] * tq))) + 1)
        _, ihi = lax.fori_loop(0, n_iter, body, (ilo0, ihi0))
        return jnp.where(tie_row, ihi, -1)

    tie_idx = lax.cond(any_tie > 0.0, tie_search, lambda _: jnp.full((tq, 1), -1, jnp.int32), 0)

    m_ref[...] = jnp.full(m_ref.shape, NEG_INF, F32)
    l_ref[...] = jnp.zeros(l_ref.shape, F32)
    acc_ref[...] = jnp.zeros(acc_ref.shape, F32)

    thr_gt_b = jnp.broadcast_to(thr_gt, (tq, LANES))
    tie_val_b = jnp.broadcast_to(tie_val, (tq, LANES))
    tk = cps * tq

    def attend(g, with_ties):
        start = pl.multiple_of(g * tk, tk)
        for u in range(cps):
            blk = sc_ref[g * cps + u]
            bias = []
            for c, b in enumerate(_lane_cols(blk)):
                hit = b > thr_gt_b
                if with_ties:
                    idx = start + u * tq + c * LANES + lane
                    hit = jnp.logical_or(hit, jnp.logical_and(b == tie_val_b, idx <= tie_idx))
                bias.append(jnp.where(hit, 0.0, NEG_INF))
            sc_ref[g * cps + u] = jnp.concatenate(bias, axis=1)
        for pair in range(DSA_HEADS // 2):
            kc = sk_ref[pl.ds(start, tk), pl.ds(pair * LANES, LANES)]
            vc = sv_ref[pl.ds(start, tk), pl.ds(pair * LANES, LANES)]
            alphas, pvs = [], []
            for h in (2 * pair, 2 * pair + 1):
                s = _dot_nt(qm_ref[h], kc)
                cols = []
                for u in range(cps):
                    bias = sc_ref[g * cps + u]
                    cols += [a + b for a, b in zip(_lane_cols(s[:, u * tq:(u + 1) * tq]), _lane_cols(bias))]
                alpha, pv = _softmax_step(cols, vc, m_ref, l_ref, h)
                alphas.append(alpha)
                pvs.append(pv)
            alpha_pair = jnp.where(low_half, alphas[0], alphas[1])
            pv_pair = jnp.where(low_half, pvs[0], pvs[1])
            acc_ref[pair] = alpha_pair * acc_ref[pair] + pv_pair

    def attend_loop(with_ties):
        def run(_):
            def body(g, carry):
                attend(g, with_ties)
                return carry
            lax.fori_loop(0, n_steps, body, 0)
            return 0
        return run

    lax.cond(any_tie > 0.0, attend_loop(True), attend_loop(False), 0)

    for pair in range(DSA_HEADS // 2):
        l_pair = jnp.where(low_half, jnp.sum(l_ref[2 * pair], axis=-1, keepdims=True),
                           jnp.sum(l_ref[2 * pair + 1], axis=-1, keepdims=True))
        o_ref[:, pair * LANES:(pair + 1) * LANES] = (acc_ref[pair] / l_pair).astype(BF16)


def _dsa_attn(sq, sk, sv, iq, ik2, iw, batch, seq, tq, cps, topk):
    nq = seq // tq
    assert nq % cps == 0 and topk <= 2 * LANES
    q_spec = lambda width: pl.BlockSpec((tq, width), lambda b, i: (b * nq + i, 0))
    resident = lambda width: pl.BlockSpec((seq, width), lambda b, i: (b, 0),
                                          pipeline_mode=pl.Buffered(1))
    return pl.pallas_call(
        functools.partial(_dsa_kernel, tq=tq, cps=cps, topk=topk),
        grid=(batch, nq),
        in_specs=[q_spec(BRANCH_WIDTH), q_spec(BRANCH_WIDTH), q_spec(LANES),
                  resident(LANES), resident(BRANCH_WIDTH), resident(BRANCH_WIDTH)],
        out_specs=q_spec(BRANCH_WIDTH),
        out_shape=jax.ShapeDtypeStruct(sq.shape, BF16),
        scratch_shapes=[pltpu.VMEM((nq, tq, tq), F32),
                        pltpu.VMEM((DSA_HEADS, tq, LANES), BF16),
                        pltpu.VMEM((IDX_HEADS, tq, LANES), BF16),
                        pltpu.VMEM((IDX_HEADS, tq, LANES), F32),
                        pltpu.VMEM((DSA_HEADS, tq, LANES), F32),
                        pltpu.VMEM((DSA_HEADS, tq, LANES), F32),
                        pltpu.VMEM((DSA_HEADS // 2, tq, LANES), F32)],
        compiler_params=pltpu.CompilerParams(
            dimension_semantics=("arbitrary", "arbitrary"),
            vmem_limit_bytes=VMEM_LIMIT),
        name="dsa_attn",
    )(sq, iq, iw, ik2, sk, sv)


def _merge_kernel(x_ref, ya_ref, yb_ref, p_ref, wg_ref, wbd_ref, wbs_ref, wo_ref, g1_ref, b1_ref,
                  wr_ref, br_ref, wpg_ref, wp_ref, x1_ref, res_ref, comb_ref, *, alpha, d_model):
    x = x_ref[...]
    gates = _dot(x.astype(BF16), wg_ref[...])
    merged = (_sigmoid(gates[:, :d_model]) * _dot(ya_ref[...], wbd_ref[...])
              + _sigmoid(gates[:, d_model:]) * _dot(yb_ref[...], wbs_ref[...]))
    mix = _dot(merged.astype(BF16), wo_ref[...])
    x1 = _layer_norm(alpha * x + mix, g1_ref[...], b1_ref[...])
    x1b = x1.astype(BF16)
    x1_ref[...] = x1b
    ple = _sigmoid(_dot(x1b, wpg_ref[...])) * _dot(p_ref[...].astype(BF16), wp_ref[...])
    res_ref[...] = alpha * x1 + ple

    logits = _dot(x1b, wr_ref[...]) + br_ref[...]
    lane = lax.broadcasted_iota(jnp.int32, logits.shape, 1)
    is_group = jnp.logical_and(lane >= N_EXPERTS, lane < N_EXPERTS + N_GROUPS)
    gl = jnp.where(is_group, logits, NEG_INF)
    gmax = jnp.max(gl, axis=-1, keepdims=True)
    gsum = jnp.sum(jnp.where(is_group, jnp.exp(gl - gmax), 0.0), axis=-1, keepdims=True)
    g_val = 1.0 / gsum
    g_idx = jnp.min(jnp.where(jnp.logical_and(is_group, gl == gmax), lane, 4 * LANES),
                    axis=-1, keepdims=True) - N_EXPERTS
    first = g_idx * EXPERTS_PER_GROUP
    in_group = jnp.logical_and(lane >= first, lane < first + EXPERTS_PER_GROUP)
    el = jnp.where(in_group, logits, NEG_INF)
    e1 = jnp.max(el, axis=-1, keepdims=True)
    i1 = jnp.min(jnp.where(jnp.logical_and(in_group, el == e1), lane, 4 * LANES), axis=-1, keepdims=True)
    el2 = jnp.where(lane == i1, NEG_INF, el)
    e2 = jnp.max(el2, axis=-1, keepdims=True)
    i2 = jnp.min(jnp.where(jnp.logical_and(in_group, el2 == e2), lane, 4 * LANES), axis=-1, keepdims=True)
    p2 = jnp.exp(e2 - e1)
    w1 = 1.0 / (1.0 + p2)
    w2 = p2 / (1.0 + p2)
    comb_ref[...] = g_val * (jnp.where(lane == i1, w1, 0.0) + jnp.where(lane == i2, w2, 0.0))


def _merge(x2, ya, yb, p2, w_g, w_bd, w_bs, w_o, g1, b1, w_r, b_r, w_pg, w_p, tm, alpha):
    n, d = x2.shape
    tok = lambda width: pl.BlockSpec((tm, width), lambda i: (i, 0))
    full = lambda a: pl.BlockSpec(a.shape, lambda i: (0, 0))
    weights = (w_g, w_bd, w_bs, w_o, g1, b1, w_r, b_r, w_pg, w_p)
    return pl.pallas_call(
        functools.partial(_merge_kernel, alpha=alpha, d_model=d),
        grid=(n // tm,),
        in_specs=[tok(d), tok(BRANCH_WIDTH), tok(BRANCH_WIDTH), tok(p2.shape[1])] + [full(w) for w in weights],
        out_specs=[tok(d), tok(d), tok(LANES)],
        out_shape=[jax.ShapeDtypeStruct((n, d), BF16), jax.ShapeDtypeStruct((n, d), F32),
                   jax.ShapeDtypeStruct((n, LANES), F32)],
        compiler_params=pltpu.CompilerParams(dimension_semantics=("arbitrary",),
                                             vmem_limit_bytes=VMEM_LIMIT),
        name="merge",
    )(x2, ya, yb, p2, *weights)


def _moe_kernel(x1_ref, res_ref, comb_ref, wgu_ref, wd_ref, g2_ref, b2_ref, o_ref, acc_ref, *, d_expert):
    e = pl.program_id(1)

    @pl.when(e == 0)
    def _():
        acc_ref[...] = jnp.zeros(acc_ref.shape, F32)

    gu = _dot(x1_ref[...], wgu_ref[0])
    comb = comb_ref[...]
    lane = lax.broadcasted_iota(jnp.int32, comb.shape, 1)
    ce = jnp.sum(jnp.where(lane == e, comb, 0.0), axis=-1, keepdims=True)
    g = gu[:, :d_expert]
    h = g * _sigmoid(g) * gu[:, d_expert:] * ce
    acc_ref[...] += _dot(h.astype(BF16), wd_ref[0])

    @pl.when(e == pl.num_programs(1) - 1)
    def _():
        o_ref[...] = _layer_norm(res_ref[...] + acc_ref[...], g2_ref[...], b2_ref[...])


def _moe(x1b, res, comb, w_gu, w_d, g2, b2, tm):
    n, d = x1b.shape
    n_exp, _, two_f = w_gu.shape
    tok = lambda width: pl.BlockSpec((tm, width), lambda i, e: (i, 0))
    return pl.pallas_call(
        functools.partial(_moe_kernel, d_expert=two_f // 2),
        grid=(n // tm, n_exp),
        in_specs=[tok(d), tok(d), tok(LANES),
                  pl.BlockSpec((1, d, two_f), lambda i, e: (e, 0, 0)),
                  pl.BlockSpec((1, two_f // 2, d), lambda i, e: (e, 0, 0)),
                  pl.BlockSpec((1, d), lambda i, e: (0, 0)),
                  pl.BlockSpec((1, d), lambda i, e: (0, 0))],
        out_specs=tok(d),
        out_shape=jax.ShapeDtypeStruct((n, d), F32),
        scratch_shapes=[pltpu.VMEM((tm, d), F32)],
        compiler_params=pltpu.CompilerParams(dimension_semantics=("arbitrary", "arbitrary"),
                                             vmem_limit_bytes=VMEM_LIMIT),
        name="moe",
    )(x1b, res, comb, w_gu, w_d, g2, b2)


def _rope_tables(positions):
    inv_freq = 1.0 / (ROPE_THETA ** (jnp.arange(0, ROT_DIM, 2, dtype=F32) / ROT_DIM))
    ang = positions.astype(F32)[..., None] * inv_freq
    cos, sin = jnp.cos(ang), jnp.sin(ang)
    n = positions.size
    cos, sin = cos.reshape(n, ROT_HALF), sin.reshape(n, ROT_HALF)
    rest = HEAD_DIM - ROT_DIM
    cos_t = jnp.concatenate([cos, cos, jnp.ones((n, rest), F32)], axis=1)
    sin_a = jnp.concatenate([-sin, jnp.zeros((n, rest + ROT_HALF), F32)], axis=1)
    sin_b = jnp.concatenate([jnp.zeros((n, ROT_HALF), F32), sin, jnp.zeros((n, rest), F32)], axis=1)
    rep = LANES // HEAD_DIM
    return tuple(jnp.tile(t, (1, rep)) for t in (cos_t, sin_a, sin_b))


def _tile(n, want):
    t = min(n, want)
    assert n % t == 0, (n, t)
    return t


def kernel(x, p, positions, w_in, diff_lambda, diff_subln_g, w_branch_diff, w_branch_dsa, w_out, ln1_g, ln1_b, w_route_group, b_route_group, w_route_expert, b_route_expert, w_exp_gate, w_exp_up, w_exp_down, w_ple, w_ple_gate, ln2_g, ln2_b):
    batch, seq, d = x.shape
    depth = w_in.shape[0]
    n = batch * seq
    alpha = (2 * depth) ** 0.25
    topk = min(TOPK_MAX, seq // 4)
    d_expert = w_exp_gate.shape[-1]
    assert BRANCH_WIDTH == DIFF_HEADS * 2 * HEAD_DIM == DSA_HEADS * HEAD_DIM == IDX_HEADS * HEAD_DIM
    n_qkv = 7 * BRANCH_WIDTH
    o_ik, o_iw = n_qkv, n_qkv + HEAD_DIM
    o_ga = o_iw + IDX_HEADS
    assert w_in.shape[2] == o_ga + 2 * d

    tm_proj = _tile(n, 512)
    tq_diff = _tile(seq, 512)
    tk_diff = _tile(seq, 1024)
    tq_dsa = _tile(seq, 256)
    cps_dsa = min(4, seq // tq_dsa)
    tm_merge = _tile(n, 256)
    tm_moe = _tile(n, 1024)

    cos_t, sin_a, sin_b = _rope_tables(positions)
    scale = HEAD_DIM ** -0.5
    col_scale = jnp.ones((n_qkv,), F32)
    for seg in (0, 3, 6):
        col_scale = col_scale.at[seg * BRANCH_WIDTH:(seg + 1) * BRANCH_WIDTH].set(scale)

    x2 = x.reshape(n, d)
    for i in range(depth):
        lam_init = 0.8 - 0.6 * math.exp(-0.3 * i)
        wi = w_in[i]
        w_ik = wi[:, o_ik:o_ik + HEAD_DIM]
        w_iw = jnp.pad(wi[:, o_iw:o_iw + IDX_HEADS], ((0, 0), (0, LANES - IDX_HEADS)))
        w_a = jnp.concatenate([wi[:, :n_qkv] * col_scale, w_ik, w_ik, w_iw], axis=1).astype(BF16)
        w_g = wi[:, o_ga:].astype(BF16)
        w_r = jnp.pad(jnp.concatenate([w_route_expert[i], w_route_group[i]], axis=1),
                      ((0, 0), (0, LANES - N_EXPERTS - N_GROUPS))).astype(BF16)
        b_r = jnp.pad(jnp.concatenate([b_route_expert[i], b_route_group[i]]),
                      (0, LANES - N_EXPERTS - N_GROUPS)).reshape(1, LANES)
        w_gu = jnp.concatenate([w_exp_gate[i], w_exp_up[i]], axis=-1).reshape(
            N_EXPERTS, d, 2 * d_expert).astype(BF16)
        w_d = w_exp_down[i].reshape(N_EXPERTS, d_expert, d).astype(BF16)

        dq, dk, dv, sq, sk, sv, iq, ik2, iw = _inproj(x2, w_a, cos_t, sin_a, sin_b, tm_proj)
        ya = _diff_attn(dq, dk, dv, diff_lambda[i], diff_subln_g[i], batch, seq, tq_diff, tk_diff, lam_init)
        yb = _dsa_attn(sq, sk, sv, iq, ik2, iw, batch, seq, tq_dsa, cps_dsa, topk)
        x1b, res, comb = _merge(
            x2, ya, yb, p[i].reshape(n, -1), w_g,
            w_branch_diff[i].astype(BF16), w_branch_dsa[i].astype(BF16), w_out[i].astype(BF16),
            ln1_g[i].reshape(1, d), ln1_b[i].reshape(1, d), w_r, b_r,
            w_ple_gate[i].astype(BF16), w_ple[i].astype(BF16), tm_merge, alpha)
        x2 = _moe(x1b, res, comb, w_gu, w_d, ln2_g[i].reshape(1, d), ln2_b[i].reshape(1, d), tm_moe)
    return x2.reshape(batch, seq, d)
```

```python
import functools
import math

import jax
import jax.numpy as jnp
import numpy as np
from jax import lax
from jax.experimental import pallas as pl
from jax.experimental.pallas import tpu as pltpu

F32 = jnp.float32
BF16 = jnp.bfloat16

HEAD_DIM = 64
DIFF_HEADS = 4
DSA_HEADS = 8
IDX_HEADS = 8
TOPK_MAX = 256
N_GROUPS = 4
EXPERTS_PER_GROUP = 8
N_EXPERTS = N_GROUPS * EXPERTS_PER_GROUP
ROPE_THETA = 500000.0
ROT_DIM = HEAD_DIM // 4
ROT_HALF = ROT_DIM // 2
LN_EPS = 1e-5
NEG_INF = -1e30
LOG2E = math.log2(math.e)
LANES = 128
BRANCH_WIDTH = 512
VMEM_LIMIT = 62 * 1024 * 1024


def _float_key(v):
    bits = int(np.float32(v).view(np.int32))
    return bits if bits >= 0 else bits ^ 0x7FFFFFFF


KEY_LO_INIT = _float_key(NEG_INF)
VALUE_STEPS = 24
TIGHTEN_START = 10
TIGHTEN_EVERY = 3


def _dot_nt(a, b):
    return lax.dot_general(a, b, (((1,), (1,)), ((), ())), preferred_element_type=F32)


def _dot(a, b):
    return jnp.dot(a, b, preferred_element_type=F32)


def _sigmoid(v):
    return 1.0 / (1.0 + jnp.exp(-v))


def _layer_norm(v, g, b):
    mu = jnp.mean(v, axis=-1, keepdims=True)
    d = v - mu
    var = jnp.mean(d * d, axis=-1, keepdims=True)
    return d * lax.rsqrt(var + LN_EPS) * g + b


def _inproj_kernel(x_ref, w_ref, c_ref, sa_ref, sb_ref,
                   dq_ref, dk_ref, dv_ref, sq_ref, sk_ref, sv_ref, iq_ref, ik_ref, iw_ref):
    xb = x_ref[...].astype(BF16)
    cos_t = c_ref[...]
    sin_a = sa_ref[...]
    sin_b = sb_ref[...]

    def rope(h):
        cols = []
        for c in range(h.shape[1] // LANES):
            hc = h[:, c * LANES:(c + 1) * LANES]
            cols.append(hc * cos_t
                        + pltpu.roll(hc, LANES - ROT_HALF, 1) * sin_a
                        + pltpu.roll(hc, ROT_HALF, 1) * sin_b)
        return cols[0] if len(cols) == 1 else jnp.concatenate(cols, axis=1)

    outs = ((dq_ref, True, LOG2E), (dk_ref, True, None), (dv_ref, False, None), (sq_ref, True, LOG2E),
            (sk_ref, True, None), (sv_ref, False, None), (iq_ref, True, None))
    for i, (ref, rotary, mult) in enumerate(outs):
        h = _dot(xb, w_ref[:, i * BRANCH_WIDTH:(i + 1) * BRANCH_WIDTH])
        if mult is not None:
            h = h * mult
        if rotary:
            h = rope(h)
        ref[...] = h.astype(BF16)
    base = len(outs) * BRANCH_WIDTH
    h = _dot(xb, w_ref[:, base:base + 2 * LANES])
    ik_ref[...] = rope(h[:, :LANES]).astype(BF16)
    iw_ref[...] = h[:, LANES:] * (IDX_HEADS ** -0.5)


def _inproj(x2, w_a, cos_t, sin_a, sin_b, tm):
    n, d = x2.shape
    wc = w_a.shape[1]
    tok = lambda width: pl.BlockSpec((tm, width), lambda i: (i, 0))
    out_shape = [jax.ShapeDtypeStruct((n, BRANCH_WIDTH), BF16)] * 7 + [
        jax.ShapeDtypeStruct((n, LANES), BF16), jax.ShapeDtypeStruct((n, LANES), F32)]
    return pl.pallas_call(
        _inproj_kernel,
        grid=(n // tm,),
        in_specs=[tok(d), pl.BlockSpec((d, wc), lambda i: (0, 0)), tok(LANES), tok(LANES), tok(LANES)],
        out_specs=[tok(BRANCH_WIDTH)] * 7 + [tok(LANES), tok(LANES)],
        out_shape=out_shape,
        compiler_params=pltpu.CompilerParams(dimension_semantics=("arbitrary",),
                                             vmem_limit_bytes=VMEM_LIMIT),
        name="inproj",
    )(x2, w_a, cos_t, sin_a, sin_b)


def _lane_cols(s):
    return [s[:, c * LANES:(c + 1) * LANES] for c in range(s.shape[1] // LANES)]


def _softmax_step(cols, vc, m_ref, l_ref, idx):
    mx = functools.reduce(jnp.maximum, cols)
    m_prev = m_ref[idx]
    m_new = jnp.maximum(m_prev, jnp.max(mx, axis=-1, keepdims=True))
    alpha = jnp.exp2(m_prev - m_new)
    ps = [jnp.exp2(c - m_new) for c in cols]
    l_ref[idx] = alpha * l_ref[idx] + functools.reduce(jnp.add, ps)
    m_ref[idx] = m_new
    p = ps[0] if len(ps) == 1 else jnp.concatenate(ps, axis=1)
    return alpha, _dot(p.astype(BF16), vc)


def _diff_kernel(q_ref, k_ref, v_ref, lam_ref, g_ref, o_ref, qm_ref, m_ref, l_ref, acc_ref, *,
                 tq, tk, lam_init):
    qi = pl.program_id(2)
    q = q_ref[...]
    lane = lax.broadcasted_iota(jnp.int32, q.shape, 1)
    zero = jnp.zeros_like(q)
    qm_ref[0] = jnp.where(lane < HEAD_DIM, q, zero)
    qm_ref[1] = jnp.where(lane >= HEAD_DIM, q, zero)
    m_ref[...] = jnp.full(m_ref.shape, NEG_INF, F32)
    l_ref[...] = jnp.zeros(l_ref.shape, F32)
    acc_ref[...] = jnp.zeros(acc_ref.shape, F32)

    def step(j, causal):
        start = pl.multiple_of(j * tk, tk)
        kc = k_ref[pl.ds(start, tk), :]
        vc = v_ref[pl.ds(start, tk), :]
        for c in range(2):
            s = _dot_nt(qm_ref[c], kc)
            if causal:
                row = qi * tq + lax.broadcasted_iota(jnp.int32, s.shape, 0)
                col = j * tk + lax.broadcasted_iota(jnp.int32, s.shape, 1)
                s = jnp.where(col <= row, s, NEG_INF)
            alpha, pv = _softmax_step(_lane_cols(s), vc, m_ref, l_ref, c)
            acc_ref[c] = alpha * acc_ref[c] + pv

    def body(j, carry):
        step(j, False)
        return carry

    n_full = (qi * tq) // tk
    lax.fori_loop(0, n_full, body, 0)
    step(n_full, True)

    lp = lam_ref[...]
    lam = (jnp.exp(jnp.sum(lp[0:1] * lp[1:2], axis=-1, keepdims=True))
           - jnp.exp(jnp.sum(lp[2:3] * lp[3:4], axis=-1, keepdims=True)) + lam_init)
    l0 = jnp.sum(l_ref[0], axis=-1, keepdims=True)
    l1 = jnp.sum(l_ref[1], axis=-1, keepdims=True)
    o = acc_ref[0] / l0 - lam * (acc_ref[1] / l1)
    o = o * lax.rsqrt(jnp.mean(o * o, axis=-1, keepdims=True) + LN_EPS)
    o_ref[...] = (o * g_ref[...] * (1.0 - lam_init)).astype(BF16)


def _diff_attn(dq, dk, dv, lam_params, subln_g, batch, seq, tq, tk, lam_init):
    assert tk % tq == 0 and seq % tk == 0
    nq = seq // tq
    hw = 2 * HEAD_DIM
    kv_spec = pl.BlockSpec((seq, hw), lambda b, h, i: (b, h))
    q_spec = pl.BlockSpec((tq, hw), lambda b, h, i: (b * nq + i, h))
    return pl.pallas_call(
        functools.partial(_diff_kernel, tq=tq, tk=tk, lam_init=lam_init),
        grid=(batch, DIFF_HEADS, nq),
        in_specs=[q_spec, kv_spec, kv_spec,
                  pl.BlockSpec(lam_params.shape, lambda b, h, i: (0, 0)),
                  pl.BlockSpec((1, hw), lambda b, h, i: (0, 0))],
        out_specs=q_spec,
        out_shape=jax.ShapeDtypeStruct(dq.shape, BF16),
        scratch_shapes=[pltpu.VMEM((2, tq, hw), BF16), pltpu.VMEM((2, tq, LANES), F32),
                        pltpu.VMEM((2, tq, LANES), F32), pltpu.VMEM((2, tq, hw), F32)],
        compiler_params=pltpu.CompilerParams(
            dimension_semantics=("arbitrary", "arbitrary", "arbitrary"),
            vmem_limit_bytes=VMEM_LIMIT),
        name="diff_attn",
    )(dq, dk, dv, lam_params, subln_g.reshape(1, hw))


def _key_to_float(key):
    bits = jnp.where(key >= 0, key, key ^ 0x7FFFFFFF)
    return lax.bitcast_convert_type(bits, F32)


def _float_to_key(v):
    bits = lax.bitcast_convert_type(v, jnp.int32)
    return jnp.where(bits >= 0, bits, bits ^ 0x7FFFFFFF)


def _dsa_kernel(sq_ref, iq_ref, iw_ref, ik_ref, sk_ref, sv_ref, o_ref,
                sc_ref, qm_ref, iqm_ref, wb_ref, m_ref, l_ref, acc_ref, *, tq, cps, topk):
    qi = pl.program_id(1)
    nch = qi + 1
    lane = lax.broadcasted_iota(jnp.int32, (tq, LANES), 1)
    low_half = lane < HEAD_DIM
    row = lax.broadcasted_iota(jnp.int32, (tq, tq), 0)
    col = lax.broadcasted_iota(jnp.int32, (tq, tq), 1)
    causal = col <= row

    iw = iw_ref[...]
    for h in range(DSA_HEADS):
        pair = h // 2
        keep = low_half if h % 2 == 0 else jnp.logical_not(low_half)
        sq_pair = sq_ref[:, pair * LANES:(pair + 1) * LANES]
        iq_pair = iq_ref[:, pair * LANES:(pair + 1) * LANES]
        qm_ref[h] = jnp.where(keep, sq_pair, jnp.zeros_like(sq_pair))
        iqm_ref[h] = jnp.where(keep, iq_pair, jnp.zeros_like(iq_pair))
        wb_ref[h] = jnp.broadcast_to(iw[:, h:h + 1], (tq, LANES))

    def score_chunk(j, masked, top2):
        start = pl.multiple_of(j * tq, tq)
        kc = ik_ref[pl.ds(start, tq), :]
        parts = [jnp.zeros((tq, LANES), F32) for _ in range(tq // LANES)]
        for h in range(IDX_HEADS):
            d = _dot_nt(iqm_ref[h], kc)
            w = wb_ref[h]
            for c in range(tq // LANES):
                parts[c] = parts[c] + w * jnp.maximum(d[:, c * LANES:(c + 1) * LANES], 0.0)
        sc = jnp.concatenate(parts, axis=1) + 0.0
        if masked:
            sc = jnp.where(causal, sc, NEG_INF)
        sc_ref[j] = sc
        max1, max2 = top2
        for part in _lane_cols(sc):
            max2 = jnp.maximum(max2, jnp.minimum(max1, part))
            max1 = jnp.maximum(max1, part)
        return max1, max2

    neg = jnp.full((tq, LANES), NEG_INF, F32)
    top2 = lax.fori_loop(0, qi, lambda j, t: score_chunk(j, False, t), (neg, neg))
    max1, max2 = score_chunk(qi, True, top2)

    n_steps = (qi + cps) // cps
    for t in range(1, cps):
        @pl.when(qi + t < n_steps * cps)
        def _():
            sc_ref[qi + t] = jnp.full((tq, tq), NEG_INF, F32)

    bands = [slice(b * LANES, (b + 1) * LANES) for b in range(tq // LANES)]
    lane_row = lax.broadcasted_iota(jnp.int32, (1, LANES), 1)
    tile = (LANES, LANES)

    def rows_of(vec):
        return jnp.broadcast_to(vec, tile).T

    def per_row(mat, reduce):
        return reduce(mat.T, axis=0, keepdims=True)

    def sweep(b, fn, init):
        def body(j, acc):
            for c, part in enumerate(_lane_cols(sc_ref[j, bands[b], :])):
                acc = fn(acc, part, j, c)
            return acc
        return lax.fori_loop(0, nch, body, init)

    def count_above(b, thr):
        thr_b = rows_of(thr)
        cnt = sweep(b, lambda acc, part, j, c: acc + jnp.where(part > thr_b, 1.0, 0.0), jnp.zeros(tile, F32))
        return per_row(cnt, jnp.sum)

    def band_extent(b, lo_f, hi_f):
        lo_b, hi_b = rows_of(lo_f), rows_of(hi_f)

        def fn(acc, part, j, c):
            top, bot = acc
            return (jnp.maximum(top, jnp.where(part <= hi_b, part, -jnp.inf)),
                    jnp.minimum(bot, jnp.where(part > lo_b, part, jnp.inf)))
        top, bot = sweep(b, fn, (jnp.full(tile, -jnp.inf, F32), jnp.full(tile, jnp.inf, F32)))
        return per_row(top, jnp.max), per_row(bot, jnp.min)

    kks = [jnp.minimum(qi * tq + b * LANES + lane_row + 1, topk).astype(F32) for b in range(len(bands))]

    def unresolved(b, lo, hi, flo):
        return jnp.logical_and(flo != kks[b], hi - 1 > lo)

    def any_set(flags):
        return functools.reduce(jnp.maximum, [jnp.max(jnp.where(f, 1.0, 0.0)) for f in flags])

    def bisect(it, b, lo, hi, flo, fhi):
        mid_key = (lo >> 1) + (hi >> 1) + (lo & hi & 1)
        mid_val = _float_to_key(0.5 * _key_to_float(lo) + 0.5 * _key_to_float(hi))
        by_value = jnp.logical_and(jnp.logical_and(mid_val > lo, mid_val < hi), it < VALUE_STEPS)
        mid = jnp.where(by_value, mid_val, mid_key)
        cnt = count_above(b, _key_to_float(mid))
        less = cnt < kks[b]
        return (jnp.where(less, lo, mid), jnp.where(less, mid, hi),
                jnp.where(less, flo, cnt), jnp.where(less, cnt, fhi))

    def tighten(it, b, lo, hi, flo, fhi):
        top, bot = band_extent(b, _key_to_float(lo), _key_to_float(hi))
        todo = unresolved(b, lo, hi, flo)
        new_lo = jnp.maximum(_float_to_key(bot) - 1, lo)
        new_hi = jnp.minimum(_float_to_key(top), hi)
        return jnp.where(todo, new_lo, lo), jnp.where(todo, new_hi, hi), flo, fhi

    def search_body(st):
        it, _, state = st
        do_tighten = jnp.logical_and(it >= TIGHTEN_START, (it - TIGHTEN_START) % TIGHTEN_EVERY == 0)
        state = lax.cond(
            do_tighten,
            lambda s: tuple(tighten(it, b, *s[b]) for b in range(len(bands))),
            lambda s: tuple(bisect(it, b, *s[b]) for b in range(len(bands))),
            state)
        todo = any_set([unresolved(b, s[0], s[1], s[2]) for b, s in enumerate(state)])
        return it + 1, todo, state

    state0 = []
    for b, rows in enumerate(bands):
        hi0 = _float_to_key(per_row(max1[rows], jnp.max))
        lo0 = jnp.maximum(_float_to_key(per_row(max2[rows], jnp.min)) - 1, KEY_LO_INIT)
        state0.append((lo0, hi0, count_above(b, _key_to_float(lo0)), jnp.zeros((1, LANES), F32)))
    state0 = tuple(state0)
    todo0 = any_set([unresolved(b, s[0], s[1], s[2]) for b, s in enumerate(state0)])
    _, _, state = lax.while_loop(lambda st: st[1] > 0.0, search_body, (jnp.int32(0), todo0, state0))

    tie_rows = [s[2] > kks[b] for b, s in enumerate(state)]
    thr_gts = [_key_to_float(jnp.where(tie_rows[b], s[1], s[0])) for b, s in enumerate(state)]
    tie_vals = [_key_to_float(s[1]) for s in state]
    needs = [kks[b] - s[3] for b, s in enumerate(state)]
    any_tie = any_set(tie_rows)
    lane_f = lax.broadcasted_iota(jnp.int32, tile, 1).astype(F32)

    def count_ties_upto(b, limit):
        val_b, lim_b = rows_of(tie_vals[b]), rows_of(limit)

        def fn(acc, part, j, c):
            idx = (j * tq + c * LANES).astype(F32) + lane_f
            return acc + jnp.where(jnp.logical_and(part == val_b, idx <= lim_b), 1.0, 0.0)
        return per_row(sweep(b, fn, jnp.zeros(tile, F32)), jnp.sum)

    def tie_search(_):
        def body(_, st):
            out = []
            for b, (ilo, ihi) in enumerate(st):
                mid = jnp.floor(0.5 * (ilo + ihi))
                enough = count_ties_upto(b, mid) >= needs[b]
                out.append((jnp.where(enough, ilo, mid), jnp.where(enough, mid, ihi)))
            return tuple(out)
        last = (qi * tq + tq - 1).astype(F32)
        st0 = tuple((jnp.full((1, LANES), -1.0, F32), jnp.full((1, LANES), last, F32)) for _ in bands)
        n_iter = int(math.ceil(math.log2(sc_ref.shape[0] * tq))) + 1
        st = lax.fori_loop(0, n_iter, body, st0)
        return tuple(jnp.where(tie_rows[b], ihi, -1.0) for b, (_, ihi) in enumerate(st))

    tie_idxs = lax.cond(any_tie > 0.0, tie_search,
                        lambda _: tuple(jnp.full((1, LANES), -1.0, F32) for _ in bands), 0)

    m_ref[...] = jnp.full(m_ref.shape, NEG_INF, F32)
    l_ref[...] = jnp.zeros(l_ref.shape, F32)
    acc_ref[...] = jnp.zeros(acc_ref.shape, F32)

    stack_rows = lambda vecs: jnp.concatenate([rows_of(v) for v in vecs], axis=0)
    thr_gt_b = stack_rows(thr_gts)
    tie_val_b = stack_rows(tie_vals)
    tie_idx_b = stack_rows(tie_idxs)
    lane_idx = lane.astype(F32)
    tk = cps * tq

    def attend(g, with_ties):
        start = pl.multiple_of(g * tk, tk)
        for u in range(cps):
            blk = sc_ref[g * cps + u]
            bias = []
            for c, b in enumerate(_lane_cols(blk)):
                hit = b > thr_gt_b
                if with_ties:
                    idx = (start + u * tq + c * LANES).astype(F32) + lane_idx
                    hit = jnp.logical_or(hit, jnp.logical_and(b == tie_val_b, idx <= tie_idx_b))
                bias.append(jnp.where(hit, 0.0, NEG_INF))
            sc_ref[g * cps + u] = jnp.concatenate(bias, axis=1)
        for pair in range(DSA_HEADS // 2):
            kc = sk_ref[pl.ds(start, tk), pl.ds(pair * LANES, LANES)]
            vc = sv_ref[pl.ds(start, tk), pl.ds(pair * LANES, LANES)]
            alphas, pvs = [], []
            for h in (2 * pair, 2 * pair + 1):
                s = _dot_nt(qm_ref[h], kc)
                cols = []
                for u in range(cps):
                    bias = sc_ref[g * cps + u]
                    cols += [a + b for a, b in zip(_lane_cols(s[:, u * tq:(u + 1) * tq]), _lane_cols(bias))]
                alpha, pv = _softmax_step(cols, vc, m_ref, l_ref, h)
                alphas.append(alpha)
                pvs.append(pv)
            alpha_pair = jnp.where(low_half, alphas[0], alphas[1])
            pv_pair = jnp.where(low_half, pvs[0], pvs[1])
            acc_ref[pair] = alpha_pair * acc_ref[pair] + pv_pair

    def attend_loop(with_ties):
        def run(_):
            def body(g, carry):
                attend(g, with_ties)
                return carry
            lax.fori_loop(0, n_steps, body, 0)
            return 0
        return run

    lax.cond(any_tie > 0.0, attend_loop(True), attend_loop(False), 0)

    for pair in range(DSA_HEADS // 2):
        l_pair = jnp.where(low_half, jnp.sum(l_ref[2 * pair], axis=-1, keepdims=True),
                           jnp.sum(l_ref[2 * pair + 1], axis=-1, keepdims=True))
        o_ref[:, pair * LANES:(pair + 1) * LANES] = (acc_ref[pair] / l_pair).astype(BF16)


def _dsa_attn(sq, sk, sv, iq, ik2, iw, batch, seq, tq, cps, topk):
    nq = seq // tq
    assert nq % cps == 0 and topk <= 2 * LANES
    q_spec = lambda width: pl.BlockSpec((tq, width), lambda b, i: (b * nq + i, 0))
    resident = lambda width: pl.BlockSpec((seq, width), lambda b, i: (b, 0),
                                          pipeline_mode=pl.Buffered(1))
    return pl.pallas_call(
        functools.partial(_dsa_kernel, tq=tq, cps=cps, topk=topk),
        grid=(batch, nq),
        in_specs=[q_spec(BRANCH_WIDTH), q_spec(BRANCH_WIDTH), q_spec(LANES),
                  resident(LANES), resident(BRANCH_WIDTH), resident(BRANCH_WIDTH)],
        out_specs=q_spec(BRANCH_WIDTH),
        out_shape=jax.ShapeDtypeStruct(sq.shape, BF16),
        scratch_shapes=[pltpu.VMEM((nq, tq, tq), F32),
                        pltpu.VMEM((DSA_HEADS, tq, LANES), BF16),
                        pltpu.VMEM((IDX_HEADS, tq, LANES), BF16),
                        pltpu.VMEM((IDX_HEADS, tq, LANES), F32),
                        pltpu.VMEM((DSA_HEADS, tq, LANES), F32),
                        pltpu.VMEM((DSA_HEADS, tq, LANES), F32),
                        pltpu.VMEM((DSA_HEADS // 2, tq, LANES), F32)],
        compiler_params=pltpu.CompilerParams(
            dimension_semantics=("arbitrary", "arbitrary"),
            vmem_limit_bytes=VMEM_LIMIT),
        name="dsa_attn",
    )(sq, iq, iw, ik2, sk, sv)


def _merge_kernel(x_ref, ya_ref, yb_ref, p_ref, wg_ref, wbd_ref, wbs_ref, wo_ref, g1_ref, b1_ref,
                  wr_ref, br_ref, wpg_ref, wp_ref, x1_ref, res_ref, comb_ref, *, alpha, d_model):
    x = x_ref[...]
    gates = _dot(x.astype(BF16), wg_ref[...])
    merged = (_sigmoid(gates[:, :d_model]) * _dot(ya_ref[...], wbd_ref[...])
              + _sigmoid(gates[:, d_model:]) * _dot(yb_ref[...], wbs_ref[...]))
    mix = _dot(merged.astype(BF16), wo_ref[...])
    x1 = _layer_norm(alpha * x + mix, g1_ref[...], b1_ref[...])
    x1b = x1.astype(BF16)
    x1_ref[...] = x1b
    ple = _sigmoid(_dot(x1b, wpg_ref[...])) * _dot(p_ref[...].astype(BF16), wp_ref[...])
    res_ref[...] = alpha * x1 + ple

    logits = _dot(x1b, wr_ref[...]) + br_ref[...]
    lane = lax.broadcasted_iota(jnp.int32, logits.shape, 1)
    is_group = jnp.logical_and(lane >= N_EXPERTS, lane < N_EXPERTS + N_GROUPS)
    gl = jnp.where(is_group, logits, NEG_INF)
    gmax = jnp.max(gl, axis=-1, keepdims=True)
    gsum = jnp.sum(jnp.where(is_group, jnp.exp(gl - gmax), 0.0), axis=-1, keepdims=True)
    g_val = 1.0 / gsum
    g_idx = jnp.min(jnp.where(jnp.logical_and(is_group, gl == gmax), lane, 4 * LANES),
                    axis=-1, keepdims=True) - N_EXPERTS
    first = g_idx * EXPERTS_PER_GROUP
    in_group = jnp.logical_and(lane >= first, lane < first + EXPERTS_PER_GROUP)
    el = jnp.where(in_group, logits, NEG_INF)
    e1 = jnp.max(el, axis=-1, keepdims=True)
    i1 = jnp.min(jnp.where(jnp.logical_and(in_group, el == e1), lane, 4 * LANES), axis=-1, keepdims=True)
    el2 = jnp.where(lane == i1, NEG_INF, el)
    e2 = jnp.max(el2, axis=-1, keepdims=True)
    i2 = jnp.min(jnp.where(jnp.logical_and(in_group, el2 == e2), lane, 4 * LANES), axis=-1, keepdims=True)
    p2 = jnp.exp(e2 - e1)
    w1 = 1.0 / (1.0 + p2)
    w2 = p2 / (1.0 + p2)
    comb_ref[...] = g_val * (jnp.where(lane == i1, w1, 0.0) + jnp.where(lane == i2, w2, 0.0))


def _merge(x2, ya, yb, p2, w_g, w_bd, w_bs, w_o, g1, b1, w_r, b_r, w_pg, w_p, tm, alpha):
    n, d = x2.shape
    tok = lambda width: pl.BlockSpec((tm, width), lambda i: (i, 0))
    full = lambda a: pl.BlockSpec(a.shape, lambda i: (0, 0))
    weights = (w_g, w_bd, w_bs, w_o, g1, b1, w_r, b_r, w_pg, w_p)
    return pl.pallas_call(
        functools.partial(_merge_kernel, alpha=alpha, d_model=d),
        grid=(n // tm,),
        in_specs=[tok(d), tok(BRANCH_WIDTH), tok(BRANCH_WIDTH), tok(p2.shape[1])] + [full(w) for w in weights],
        out_specs=[tok(d), tok(d), tok(LANES)],
        out_shape=[jax.ShapeDtypeStruct((n, d), BF16), jax.ShapeDtypeStruct((n, d), F32),
                   jax.ShapeDtypeStruct((n, LANES), F32)],
        compiler_params=pltpu.CompilerParams(dimension_semantics=("arbitrary",),
                                             vmem_limit_bytes=VMEM_LIMIT),
        name="merge",
    )(x2, ya, yb, p2, *weights)


def _moe_kernel(x1_ref, res_ref, comb_ref, wgu_ref, wd_ref, g2_ref, b2_ref, o_ref, acc_ref, *, d_expert):
    e = pl.program_id(1)

    @pl.when(e == 0)
    def _():
        acc_ref[...] = jnp.zeros(acc_ref.shape, F32)

    gu = _dot(x1_ref[...], wgu_ref[0])
    comb = comb_ref[...]
    lane = lax.broadcasted_iota(jnp.int32, comb.shape, 1)
    ce = jnp.sum(jnp.where(lane == e, comb, 0.0), axis=-1, keepdims=True)
    g = gu[:, :d_expert]
    h = g * _sigmoid(g) * gu[:, d_expert:] * ce
    acc_ref[...] += _dot(h.astype(BF16), wd_ref[0])

    @pl.when(e == pl.num_programs(1) - 1)
    def _():
        o_ref[...] = _layer_norm(res_ref[...] + acc_ref[...], g2_ref[...], b2_ref[...])


def _moe(x1b, res, comb, w_gu, w_d, g2, b2, tm):
    n, d = x1b.shape
    n_exp, _, two_f = w_gu.shape
    tok = lambda width: pl.BlockSpec((tm, width), lambda i, e: (i, 0))
    return pl.pallas_call(
        functools.partial(_moe_kernel, d_expert=two_f // 2),
        grid=(n // tm, n_exp),
        in_specs=[tok(d), tok(d), tok(LANES),
                  pl.BlockSpec((1, d, two_f), lambda i, e: (e, 0, 0)),
                  pl.BlockSpec((1, two_f // 2, d), lambda i, e: (e, 0, 0)),
                  pl.BlockSpec((1, d), lambda i, e: (0, 0)),
                  pl.BlockSpec((1, d), lambda i, e: (0, 0))],
        out_specs=tok(d),
        out_shape=jax.ShapeDtypeStruct((n, d), F32),
        scratch_shapes=[pltpu.VMEM((tm, d), F32)],
        compiler_params=pltpu.CompilerParams(dimension_semantics=("arbitrary", "arbitrary"),
                                             vmem_limit_bytes=VMEM_LIMIT),
        name="moe",
    )(x1b, res, comb, w_gu, w_d, g2, b2)


def _rope_tables(positions):
    inv_freq = 1.0 / (ROPE_THETA ** (jnp.arange(0, ROT_DIM, 2, dtype=F32) / ROT_DIM))
    ang = positions.astype(F32)[..., None] * inv_freq
    cos, sin = jnp.cos(ang), jnp.sin(ang)
    n = positions.size
    cos, sin = cos.reshape(n, ROT_HALF), sin.reshape(n, ROT_HALF)
    rest = HEAD_DIM - ROT_DIM
    cos_t = jnp.concatenate([cos, cos, jnp.ones((n, rest), F32)], axis=1)
    sin_a = jnp.concatenate([-sin, jnp.zeros((n, rest + ROT_HALF), F32)], axis=1)
    sin_b = jnp.concatenate([jnp.zeros((n, ROT_HALF), F32), sin, jnp.zeros((n, rest), F32)], axis=1)
    rep = LANES // HEAD_DIM
    return tuple(jnp.tile(t, (1, rep)) for t in (cos_t, sin_a, sin_b))


def _tile(n, want):
    t = min(n, want)
    assert n % t == 0, (n, t)
    return t


def kernel(x, p, positions, w_in, diff_lambda, diff_subln_g, w_branch_diff, w_branch_dsa, w_out, ln1_g, ln1_b, w_route_group, b_route_group, w_route_expert, b_route_expert, w_exp_gate, w_exp_up, w_exp_down, w_ple, w_ple_gate, ln2_g, ln2_b):
    batch, seq, d = x.shape
    depth = w_in.shape[0]
    n = batch * seq
    alpha = (2 * depth) ** 0.25
    topk = min(TOPK_MAX, seq // 4)
    d_expert = w_exp_gate.shape[-1]
    assert BRANCH_WIDTH == DIFF_HEADS * 2 * HEAD_DIM == DSA_HEADS * HEAD_DIM == IDX_HEADS * HEAD_DIM
    n_qkv = 7 * BRANCH_WIDTH
    o_ik, o_iw = n_qkv, n_qkv + HEAD_DIM
    o_ga = o_iw + IDX_HEADS
    assert w_in.shape[2] == o_ga + 2 * d

    tm_proj = _tile(n, 512)
    tq_diff = _tile(seq, 512)
    tk_diff = _tile(seq, 1024)
    tq_dsa = _tile(seq, 256)
    cps_dsa = min(4, seq // tq_dsa)
    tm_merge = _tile(n, 256)
    tm_moe = _tile(n, 1024)

    cos_t, sin_a, sin_b = _rope_tables(positions)
    scale = HEAD_DIM ** -0.5
    col_scale = jnp.ones((n_qkv,), F32)
    for seg in (0, 3, 6):
        col_scale = col_scale.at[seg * BRANCH_WIDTH:(seg + 1) * BRANCH_WIDTH].set(scale)

    x2 = x.reshape(n, d)
    for i in range(depth):
        lam_init = 0.8 - 0.6 * math.exp(-0.3 * i)
        wi = w_in[i]
        w_ik = wi[:, o_ik:o_ik + HEAD_DIM]
        w_iw = jnp.pad(wi[:, o_iw:o_iw + IDX_HEADS], ((0, 0), (0, LANES - IDX_HEADS)))
        w_a = jnp.concatenate([wi[:, :n_qkv] * col_scale, w_ik, w_ik, w_iw], axis=1).astype(BF16)
        w_g = wi[:, o_ga:].astype(BF16)
        w_r = jnp.pad(jnp.concatenate([w_route_expert[i], w_route_group[i]], axis=1),
                      ((0, 0), (0, LANES - N_EXPERTS - N_GROUPS))).astype(BF16)
        b_r = jnp.pad(jnp.concatenate([b_route_expert[i], b_route_group[i]]),
                      (0, LANES - N_EXPERTS - N_GROUPS)).reshape(1, LANES)
        w_gu = jnp.concatenate([w_exp_gate[i], w_exp_up[i]], axis=-1).reshape(
            N_EXPERTS, d, 2 * d_expert).astype(BF16)
        w_d = w_exp_down[i].reshape(N_EXPERTS, d_expert, d).astype(BF16)

        dq, dk, dv, sq, sk, sv, iq, ik2, iw = _inproj(x2, w_a, cos_t, sin_a, sin_b, tm_proj)
        ya = _diff_attn(dq, dk, dv, diff_lambda[i], diff_subln_g[i], batch, seq, tq_diff, tk_diff, lam_init)
        yb = _dsa_attn(sq, sk, sv, iq, ik2, iw, batch, seq, tq_dsa, cps_dsa, topk)
        x1b, res, comb = _merge(
            x2, ya, yb, p[i].reshape(n, -1), w_g,
            w_branch_diff[i].astype(BF16), w_branch_dsa[i].astype(BF16), w_out[i].astype(BF16),
            ln1_g[i].reshape(1, d), ln1_b[i].reshape(1, d), w_r, b_r,
            w_ple_gate[i].astype(BF16), w_ple[i].astype(BF16), tm_merge, alpha)
        x2 = _moe(x1b, res, comb, w_gu, w_d, ln2_g[i].reshape(1, d), ln2_b[i].reshape(1, d), tm_moe)
    return x2.reshape(batch, seq, d)
```

```python
import functools
import math

import jax
import jax.numpy as jnp
import numpy as np
from jax import lax
from jax.experimental import pallas as pl
from jax.experimental.pallas import tpu as pltpu

F32 = jnp.float32
BF16 = jnp.bfloat16

HEAD_DIM = 64
DIFF_HEADS = 4
DSA_HEADS = 8
IDX_HEADS = 8
TOPK_MAX = 256
N_GROUPS = 4
EXPERTS_PER_GROUP = 8
N_EXPERTS = N_GROUPS * EXPERTS_PER_GROUP
ROPE_THETA = 500000.0
ROT_DIM = HEAD_DIM // 4
ROT_HALF = ROT_DIM // 2
LN_EPS = 1e-5
NEG_INF = -1e30
LOG2E = math.log2(math.e)
LANES = 128
BRANCH_WIDTH = 512
VMEM_LIMIT = 62 * 1024 * 1024


SIGN_BIT = np.int32(-2 ** 31)
MAGNITUDE_BITS = np.int32(2 ** 31 - 1)
SUBNORMAL_BITS = np.int32(2 ** 23 - 1)


def _float_key(v):
    bits = int(np.float32(v).view(np.int32))
    mag = max((bits & int(MAGNITUDE_BITS)) - int(SUBNORMAL_BITS), 0)
    return -mag if bits < 0 else mag


KEY_LO_INIT = _float_key(NEG_INF)
VALUE_STEPS = 24
TIGHTEN_START = 13
TIGHTEN_EVERY = 4


def _dot_nt(a, b):
    return lax.dot_general(a, b, (((1,), (1,)), ((), ())), preferred_element_type=F32)


def _dot(a, b):
    return jnp.dot(a, b, preferred_element_type=F32)


def _sigmoid(v):
    return 1.0 / (1.0 + jnp.exp(-v))


def _layer_norm(v, g, b):
    mu = jnp.mean(v, axis=-1, keepdims=True)
    d = v - mu
    var = jnp.mean(d * d, axis=-1, keepdims=True)
    return d * lax.rsqrt(var + LN_EPS) * g + b


def _inproj_kernel(x_ref, w_ref, c_ref, sa_ref, sb_ref,
                   dq_ref, dk_ref, dv_ref, sq_ref, sk_ref, sv_ref, iq_ref, ik_ref, iw_ref):
    xb = x_ref[...].astype(BF16)
    cos_t = c_ref[...]
    sin_a = sa_ref[...]
    sin_b = sb_ref[...]

    def rope(h):
        cols = []
        for c in range(h.shape[1] // LANES):
            hc = h[:, c * LANES:(c + 1) * LANES]
            cols.append(hc * cos_t
                        + pltpu.roll(hc, LANES - ROT_HALF, 1) * sin_a
                        + pltpu.roll(hc, ROT_HALF, 1) * sin_b)
        return cols[0] if len(cols) == 1 else jnp.concatenate(cols, axis=1)

    outs = ((dq_ref, True, LOG2E), (dk_ref, True, None), (dv_ref, False, None), (sq_ref, True, LOG2E),
            (sk_ref, True, None), (sv_ref, False, None), (iq_ref, True, None))
    for i, (ref, rotary, mult) in enumerate(outs):
        h = _dot(xb, w_ref[:, i * BRANCH_WIDTH:(i + 1) * BRANCH_WIDTH])
        if mult is not None:
            h = h * mult
        if rotary:
            h = rope(h)
        ref[...] = h.astype(BF16)
    base = len(outs) * BRANCH_WIDTH
    h = _dot(xb, w_ref[:, base:base + 2 * LANES])
    ik_ref[...] = rope(h[:, :LANES]).astype(BF16)
    iw_ref[...] = h[:, LANES:] * (IDX_HEADS ** -0.5)


def _inproj(x2, w_a, cos_t, sin_a, sin_b, tm):
    n, d = x2.shape
    wc = w_a.shape[1]
    tok = lambda width: pl.BlockSpec((tm, width), lambda i: (i, 0))
    out_shape = [jax.ShapeDtypeStruct((n, BRANCH_WIDTH), BF16)] * 7 + [
        jax.ShapeDtypeStruct((n, LANES), BF16), jax.ShapeDtypeStruct((n, LANES), F32)]
    return pl.pallas_call(
        _inproj_kernel,
        grid=(n // tm,),
        in_specs=[tok(d), pl.BlockSpec((d, wc), lambda i: (0, 0)), tok(LANES), tok(LANES), tok(LANES)],
        out_specs=[tok(BRANCH_WIDTH)] * 7 + [tok(LANES), tok(LANES)],
        out_shape=out_shape,
        compiler_params=pltpu.CompilerParams(dimension_semantics=("arbitrary",),
                                             vmem_limit_bytes=VMEM_LIMIT),
        name="inproj",
    )(x2, w_a, cos_t, sin_a, sin_b)


def _lane_cols(s):
    return [s[:, c * LANES:(c + 1) * LANES] for c in range(s.shape[1] // LANES)]


def _softmax_step(cols, vc, m_ref, l_ref, idx):
    mx = functools.reduce(jnp.maximum, cols)
    m_prev = m_ref[idx]
    m_new = jnp.maximum(m_prev, jnp.max(mx, axis=-1, keepdims=True))
    alpha = jnp.exp2(m_prev - m_new)
    ps = [jnp.exp2(c - m_new) for c in cols]
    l_ref[idx] = alpha * l_ref[idx] + functools.reduce(jnp.add, ps)
    m_ref[idx] = m_new
    p = ps[0] if len(ps) == 1 else jnp.concatenate(ps, axis=1)
    return alpha, _dot(p.astype(BF16), vc)


def _diff_kernel(q_ref, k_ref, v_ref, lam_ref, g_ref, o_ref, qm_ref, m_ref, l_ref, acc_ref, *,
                 tq, tk, lam_init):
    qi = pl.program_id(2)
    q = q_ref[...]
    lane = lax.broadcasted_iota(jnp.int32, q.shape, 1)
    zero = jnp.zeros_like(q)
    qm_ref[0] = jnp.where(lane < HEAD_DIM, q, zero)
    qm_ref[1] = jnp.where(lane >= HEAD_DIM, q, zero)
    m_ref[...] = jnp.full(m_ref.shape, NEG_INF, F32)
    l_ref[...] = jnp.zeros(l_ref.shape, F32)
    acc_ref[...] = jnp.zeros(acc_ref.shape, F32)

    def step(j, causal):
        start = pl.multiple_of(j * tk, tk)
        kc = k_ref[pl.ds(start, tk), :]
        vc = v_ref[pl.ds(start, tk), :]
        for c in range(2):
            s = _dot_nt(qm_ref[c], kc)
            if causal:
                row = qi * tq + lax.broadcasted_iota(jnp.int32, s.shape, 0)
                col = j * tk + lax.broadcasted_iota(jnp.int32, s.shape, 1)
                s = jnp.where(col <= row, s, NEG_INF)
            alpha, pv = _softmax_step(_lane_cols(s), vc, m_ref, l_ref, c)
            acc_ref[c] = alpha * acc_ref[c] + pv

    def body(j, carry):
        step(j, False)
        return carry

    n_full = (qi * tq) // tk
    lax.fori_loop(0, n_full, body, 0)
    step(n_full, True)

    lp = lam_ref[...]
    lam = (jnp.exp(jnp.sum(lp[0:1] * lp[1:2], axis=-1, keepdims=True))
           - jnp.exp(jnp.sum(lp[2:3] * lp[3:4], axis=-1, keepdims=True)) + lam_init)
    l0 = jnp.sum(l_ref[0], axis=-1, keepdims=True)
    l1 = jnp.sum(l_ref[1], axis=-1, keepdims=True)
    o = acc_ref[0] / l0 - lam * (acc_ref[1] / l1)
    o = o * lax.rsqrt(jnp.mean(o * o, axis=-1, keepdims=True) + LN_EPS)
    o_ref[...] = (o * g_ref[...] * (1.0 - lam_init)).astype(BF16)


def _diff_attn(dq, dk, dv, lam_params, subln_g, batch, seq, tq, tk, lam_init):
    assert tk % tq == 0 and seq % tk == 0
    nq = seq // tq
    hw = 2 * HEAD_DIM
    kv_spec = pl.BlockSpec((seq, hw), lambda b, h, i: (b, h))
    q_spec = pl.BlockSpec((tq, hw), lambda b, h, i: (b * nq + i, h))
    return pl.pallas_call(
        functools.partial(_diff_kernel, tq=tq, tk=tk, lam_init=lam_init),
        grid=(batch, DIFF_HEADS, nq),
        in_specs=[q_spec, kv_spec, kv_spec,
                  pl.BlockSpec(lam_params.shape, lambda b, h, i: (0, 0)),
                  pl.BlockSpec((1, hw), lambda b, h, i: (0, 0))],
        out_specs=q_spec,
        out_shape=jax.ShapeDtypeStruct(dq.shape, BF16),
        scratch_shapes=[pltpu.VMEM((2, tq, hw), BF16), pltpu.VMEM((2, tq, LANES), F32),
                        pltpu.VMEM((2, tq, LANES), F32), pltpu.VMEM((2, tq, hw), F32)],
        compiler_params=pltpu.CompilerParams(
            dimension_semantics=("arbitrary", "arbitrary", "arbitrary"),
            vmem_limit_bytes=VMEM_LIMIT),
        name="diff_attn",
    )(dq, dk, dv, lam_params, subln_g.reshape(1, hw))


def _key_to_float(key):
    mag = jnp.abs(key)
    bits = jnp.where(mag > 0, mag + SUBNORMAL_BITS, 0)
    return lax.bitcast_convert_type(jnp.where(key < 0, bits | SIGN_BIT, bits), F32)


def _float_to_key(v):
    bits = lax.bitcast_convert_type(v, jnp.int32)
    mag = jnp.maximum((bits & MAGNITUDE_BITS) - SUBNORMAL_BITS, 0)
    return jnp.where(bits < 0, -mag, mag)


def _dsa_kernel(sq_ref, iq_ref, iw_ref, ik_ref, sk_ref, sv_ref, tri_ref, o_ref,
                sc_ref, qm_ref, iqm_ref, wb_ref, m_ref, l_ref, acc_ref, *, tq, cps, topk):
    qi = pl.program_id(1)
    nch = qi + 1
    lane = lax.broadcasted_iota(jnp.int32, (tq, LANES), 1)
    low_half = lane < HEAD_DIM
    row = lax.broadcasted_iota(jnp.int32, (tq, tq), 0)
    col = lax.broadcasted_iota(jnp.int32, (tq, tq), 1)
    causal = col <= row

    iw = iw_ref[...]
    for h in range(DSA_HEADS):
        pair = h // 2
        keep = low_half if h % 2 == 0 else jnp.logical_not(low_half)
        sq_pair = sq_ref[:, pair * LANES:(pair + 1) * LANES]
        iq_pair = iq_ref[:, pair * LANES:(pair + 1) * LANES]
        qm_ref[h] = jnp.where(keep, sq_pair, jnp.zeros_like(sq_pair))
        iqm_ref[h] = jnp.where(keep, iq_pair, jnp.zeros_like(iq_pair))
        wb_ref[h] = jnp.broadcast_to(iw[:, h:h + 1], (tq, LANES))

    def score_chunk(j, masked, top2):
        start = pl.multiple_of(j * tq, tq)
        kc = ik_ref[pl.ds(start, tq), :]
        parts = [jnp.zeros((tq, LANES), F32) for _ in range(tq // LANES)]
        for h in range(IDX_HEADS):
            d = _dot_nt(iqm_ref[h], kc)
            w = wb_ref[h]
            for c in range(tq // LANES):
                parts[c] = parts[c] + w * jnp.maximum(d[:, c * LANES:(c + 1) * LANES], 0.0)
        sc = jnp.concatenate(parts, axis=1) + 0.0
        if masked:
            sc = jnp.where(causal, sc, NEG_INF)
        sc_ref[j] = sc
        max1, max2 = top2
        for part in _lane_cols(sc):
            max2 = jnp.maximum(max2, jnp.minimum(max1, part))
            max1 = jnp.maximum(max1, part)
        return max1, max2

    neg = jnp.full((tq, LANES), NEG_INF, F32)
    top2 = lax.fori_loop(0, qi, lambda j, t: score_chunk(j, False, t), (neg, neg))
    max1, max2 = score_chunk(qi, True, top2)

    n_steps = (qi + cps) // cps
    for t in range(1, cps):
        @pl.when(qi + t < n_steps * cps)
        def _():
            sc_ref[qi + t] = jnp.full((tq, tq), NEG_INF, F32)

    bands = [slice(b * LANES, (b + 1) * LANES) for b in range(tq // LANES)]
    lane_row = lax.broadcasted_iota(jnp.int32, (1, LANES), 1)
    tile = (LANES, LANES)

    def rows_of(vec):
        return jnp.broadcast_to(vec, tile).T

    def per_row(mat, reduce):
        return reduce(mat.T, axis=0, keepdims=True)

    def sweep(b, fn, init):
        def body(j, acc):
            for c, part in enumerate(_lane_cols(sc_ref[j, bands[b], :])):
                acc = fn(acc, part, j, c)
            return acc
        return lax.fori_loop(0, nch, body, init)

    def count_above(thrs):
        thr_bs = [rows_of(t) for t in thrs]
        cnts = [sweep(b, lambda acc, part, j, c, t=t: acc + jnp.where(part > t, 1.0, 0.0),
                      jnp.zeros(tile, F32)) for b, t in enumerate(thr_bs)]
        return [per_row(c, jnp.sum) for c in cnts]

    def band_extent(lo_fs, hi_fs):
        edges = [(rows_of(lo_f), rows_of(hi_f)) for lo_f, hi_f in zip(lo_fs, hi_fs)]

        def fn(acc, part, j, c, lo_b, hi_b):
            top, bot = acc
            return (jnp.maximum(top, jnp.where(part <= hi_b, part, -jnp.inf)),
                    jnp.minimum(bot, jnp.where(part > lo_b, part, jnp.inf)))
        init = (jnp.full(tile, -jnp.inf, F32), jnp.full(tile, jnp.inf, F32))
        ext = [sweep(b, functools.partial(fn, lo_b=lo_b, hi_b=hi_b), init) for b, (lo_b, hi_b) in enumerate(edges)]
        return [(per_row(top, jnp.max), per_row(bot, jnp.min)) for top, bot in ext]

    kks = [jnp.minimum(qi * tq + b * LANES + lane_row + 1, topk).astype(F32) for b in range(len(bands))]

    def unresolved(b, lo, hi, flo):
        return jnp.logical_and(flo != kks[b], hi - 1 > lo)

    def any_set(flags):
        return functools.reduce(jnp.maximum, [jnp.max(jnp.where(f, 1.0, 0.0)) for f in flags])

    def midpoint(it, lo, hi):
        mid_key = (lo >> 1) + (hi >> 1) + (lo & hi & 1)
        mid_val = _float_to_key(0.5 * _key_to_float(lo) + 0.5 * _key_to_float(hi))
        by_value = jnp.logical_and(jnp.logical_and(mid_val > lo, mid_val < hi), it < VALUE_STEPS)
        return jnp.where(by_value, mid_val, mid_key)

    def bisect(it, state):
        mids = [midpoint(it, lo, hi) for lo, hi, _, _ in state]
        cnts = count_above([_key_to_float(mid) for mid in mids])
        out = []
        for b, (lo, hi, flo, fhi) in enumerate(state):
            mid, cnt = mids[b], cnts[b]
            less = cnt < kks[b]
            out.append((jnp.where(less, lo, mid), jnp.where(less, mid, hi),
                        jnp.where(less, flo, cnt), jnp.where(less, cnt, fhi)))
        return tuple(out)

    def tighten(state):
        ext = band_extent([_key_to_float(s[0]) for s in state], [_key_to_float(s[1]) for s in state])
        out = []
        for b, (lo, hi, flo, fhi) in enumerate(state):
            top, bot = ext[b]
            todo = unresolved(b, lo, hi, flo)
            new_lo = jnp.maximum(_float_to_key(bot) - 1, lo)
            new_hi = jnp.minimum(_float_to_key(top), hi)
            out.append((jnp.where(todo, new_lo, lo), jnp.where(todo, new_hi, hi), flo, fhi))
        return tuple(out)

    def search_body(st):
        it, _, state = st
        do_tighten = jnp.logical_and(it >= TIGHTEN_START, (it - TIGHTEN_START) % TIGHTEN_EVERY == 0)
        state = lax.cond(do_tighten, tighten, functools.partial(bisect, it), state)
        todo = any_set([unresolved(b, s[0], s[1], s[2]) for b, s in enumerate(state)])
        return it + 1, todo, state

    hi0s = [_float_to_key(per_row(max1[rows], jnp.max)) for rows in bands]
    lo0s = [jnp.maximum(_float_to_key(per_row(max2[rows], jnp.min)) - 1, KEY_LO_INIT) for rows in bands]
    flo0s = count_above([_key_to_float(lo0) for lo0 in lo0s])
    state0 = tuple((lo0s[b], hi0s[b], flo0s[b], jnp.zeros((1, LANES), F32)) for b in range(len(bands)))
    todo0 = any_set([unresolved(b, s[0], s[1], s[2]) for b, s in enumerate(state0)])
    _, _, state = lax.while_loop(lambda st: st[1] > 0.0, search_body, (jnp.int32(0), todo0, state0))

    tie_rows = [s[2] > kks[b] for b, s in enumerate(state)]
    thr_gts = [_key_to_float(jnp.where(tie_rows[b], s[1], s[0])) for b, s in enumerate(state)]
    tie_vals = [_key_to_float(s[1]) for s in state]
    needs = [jnp.where(tie_rows[b], kks[b] - s[3], 0.0) for b, s in enumerate(state)]
    any_tie = any_set(tie_rows)

    m_ref[...] = jnp.full(m_ref.shape, NEG_INF, F32)
    l_ref[...] = jnp.zeros(l_ref.shape, F32)
    acc_ref[...] = jnp.zeros(acc_ref.shape, F32)

    stack_rows = lambda vecs: jnp.concatenate([rows_of(v) for v in vecs], axis=0)
    thr_gt_b = stack_rows(thr_gts)
    tie_val_b = stack_rows(tie_vals)
    need_b = stack_rows(needs)
    tk = cps * tq

    def attend(g, seen):
        start = pl.multiple_of(g * tk, tk)
        for u in range(cps):
            blk = sc_ref[g * cps + u]
            bias = []
            for b in _lane_cols(blk):
                hit = b > thr_gt_b
                if seen is not None:
                    tied = b == tie_val_b
                    counts = _dot(jnp.where(tied, 1.0, 0.0).astype(BF16), tri_ref[...])
                    rank = seen + counts[:, :LANES]
                    hit = jnp.logical_or(hit, jnp.logical_and(tied, rank <= need_b))
                    seen = seen + counts[:, LANES:]
                bias.append(jnp.where(hit, 0.0, NEG_INF))
            sc_ref[g * cps + u] = jnp.concatenate(bias, axis=1)
        for pair in range(DSA_HEADS // 2):
            kc = sk_ref[pl.ds(start, tk), pl.ds(pair * LANES, LANES)]
            vc = sv_ref[pl.ds(start, tk), pl.ds(pair * LANES, LANES)]
            alphas, pvs = [], []
            for h in (2 * pair, 2 * pair + 1):
                s = _dot_nt(qm_ref[h], kc)
                cols = []
                for u in range(cps):
                    bias = sc_ref[g * cps + u]
                    cols += [a + b for a, b in zip(_lane_cols(s[:, u * tq:(u + 1) * tq]), _lane_cols(bias))]
                alpha, pv = _softmax_step(cols, vc, m_ref, l_ref, h)
                alphas.append(alpha)
                pvs.append(pv)
            alpha_pair = jnp.where(low_half, alphas[0], alphas[1])
            pv_pair = jnp.where(low_half, pvs[0], pvs[1])
            acc_ref[pair] = alpha_pair * acc_ref[pair] + pv_pair
        return seen

    def attend_ties(_):
        lax.fori_loop(0, n_steps, attend, jnp.zeros((tq, LANES), F32))
        return 0

    def attend_plain(_):
        lax.fori_loop(0, n_steps, lambda g, carry: (attend(g, None), carry)[1], 0)
        return 0

    lax.cond(any_tie > 0.0, attend_ties, attend_plain, 0)

    for pair in range(DSA_HEADS // 2):
        l_pair = jnp.where(low_half, jnp.sum(l_ref[2 * pair], axis=-1, keepdims=True),
                           jnp.sum(l_ref[2 * pair + 1], axis=-1, keepdims=True))
        o_ref[:, pair * LANES:(pair + 1) * LANES] = (acc_ref[pair] / l_pair).astype(BF16)


def _dsa_attn(sq, sk, sv, iq, ik2, iw, batch, seq, tq, cps, topk):
    nq = seq // tq
    assert nq % cps == 0 and topk <= 2 * LANES
    k_idx = jnp.arange(LANES)
    tri = jnp.concatenate([(k_idx[:, None] <= k_idx[None, :]).astype(BF16),
                           jnp.ones((LANES, LANES), BF16)], axis=1)
    q_spec = lambda width: pl.BlockSpec((tq, width), lambda b, i: (b * nq + i, 0))
    resident = lambda width: pl.BlockSpec((seq, width), lambda b, i: (b, 0),
                                          pipeline_mode=pl.Buffered(1))
    return pl.pallas_call(
        functools.partial(_dsa_kernel, tq=tq, cps=cps, topk=topk),
        grid=(batch, nq),
        in_specs=[q_spec(BRANCH_WIDTH), q_spec(BRANCH_WIDTH), q_spec(LANES),
                  resident(LANES), resident(BRANCH_WIDTH), resident(BRANCH_WIDTH),
                  pl.BlockSpec(tri.shape, lambda b, i: (0, 0))],
        out_specs=q_spec(BRANCH_WIDTH),
        out_shape=jax.ShapeDtypeStruct(sq.shape, BF16),
        scratch_shapes=[pltpu.VMEM((nq, tq, tq), F32),
                        pltpu.VMEM((DSA_HEADS, tq, LANES), BF16),
                        pltpu.VMEM((IDX_HEADS, tq, LANES), BF16),
                        pltpu.VMEM((IDX_HEADS, tq, LANES), F32),
                        pltpu.VMEM((DSA_HEADS, tq, LANES), F32),
                        pltpu.VMEM((DSA_HEADS, tq, LANES), F32),
                        pltpu.VMEM((DSA_HEADS // 2, tq, LANES), F32)],
        compiler_params=pltpu.CompilerParams(
            dimension_semantics=("arbitrary", "arbitrary"),
            vmem_limit_bytes=VMEM_LIMIT),
        name="dsa_attn",
    )(sq, iq, iw, ik2, sk, sv, tri)


def _merge_kernel(x_ref, ya_ref, yb_ref, p_ref, wg_ref, wbd_ref, wbs_ref, wo_ref, g1_ref, b1_ref,
                  wr_ref, br_ref, wpg_ref, wp_ref, x1_ref, res_ref, comb_ref, *, alpha, d_model):
    x = x_ref[...]
    gates = _dot(x.astype(BF16), wg_ref[...])
    merged = (_sigmoid(gates[:, :d_model]) * _dot(ya_ref[...], wbd_ref[...])
              + _sigmoid(gates[:, d_model:]) * _dot(yb_ref[...], wbs_ref[...]))
    mix = _dot(merged.astype(BF16), wo_ref[...])
    x1 = _layer_norm(alpha * x + mix, g1_ref[...], b1_ref[...])
    x1b = x1.astype(BF16)
    x1_ref[...] = x1b
    ple = _sigmoid(_dot(x1b, wpg_ref[...])) * _dot(p_ref[...].astype(BF16), wp_ref[...])
    res_ref[...] = alpha * x1 + ple

    logits = _dot(x1b, wr_ref[...]) + br_ref[...]
    lane = lax.broadcasted_iota(jnp.int32, logits.shape, 1)
    is_group = jnp.logical_and(lane >= N_EXPERTS, lane < N_EXPERTS + N_GROUPS)
    gl = jnp.where(is_group, logits, NEG_INF)
    gmax = jnp.max(gl, axis=-1, keepdims=True)
    gsum = jnp.sum(jnp.where(is_group, jnp.exp(gl - gmax), 0.0), axis=-1, keepdims=True)
    g_val = 1.0 / gsum
    g_idx = jnp.min(jnp.where(jnp.logical_and(is_group, gl == gmax), lane, 4 * LANES),
                    axis=-1, keepdims=True) - N_EXPERTS
    first = g_idx * EXPERTS_PER_GROUP
    in_group = jnp.logical_and(lane >= first, lane < first + EXPERTS_PER_GROUP)
    el = jnp.where(in_group, logits, NEG_INF)
    e1 = jnp.max(el, axis=-1, keepdims=True)
    i1 = jnp.min(jnp.where(jnp.logical_and(in_group, el == e1), lane, 4 * LANES), axis=-1, keepdims=True)
    el2 = jnp.where(lane == i1, NEG_INF, el)
    e2 = jnp.max(el2, axis=-1, keepdims=True)
    i2 = jnp.min(jnp.where(jnp.logical_and(in_group, el2 == e2), lane, 4 * LANES), axis=-1, keepdims=True)
    p2 = jnp.exp(e2 - e1)
    w1 = 1.0 / (1.0 + p2)
    w2 = p2 / (1.0 + p2)
    comb_ref[...] = g_val * (jnp.where(lane == i1, w1, 0.0) + jnp.where(lane == i2, w2, 0.0))


def _merge(x2, ya, yb, p2, w_g, w_bd, w_bs, w_o, g1, b1, w_r, b_r, w_pg, w_p, tm, alpha):
    n, d = x2.shape
    tok = lambda width: pl.BlockSpec((tm, width), lambda i: (i, 0))
    full = lambda a: pl.BlockSpec(a.shape, lambda i: (0, 0))
    weights = (w_g, w_bd, w_bs, w_o, g1, b1, w_r, b_r, w_pg, w_p)
    return pl.pallas_call(
        functools.partial(_merge_kernel, alpha=alpha, d_model=d),
        grid=(n // tm,),
        in_specs=[tok(d), tok(BRANCH_WIDTH), tok(BRANCH_WIDTH), tok(p2.shape[1])] + [full(w) for w in weights],
        out_specs=[tok(d), tok(d), tok(LANES)],
        out_shape=[jax.ShapeDtypeStruct((n, d), BF16), jax.ShapeDtypeStruct((n, d), F32),
                   jax.ShapeDtypeStruct((n, LANES), F32)],
        compiler_params=pltpu.CompilerParams(dimension_semantics=("arbitrary",),
                                             vmem_limit_bytes=VMEM_LIMIT),
        name="merge",
    )(x2, ya, yb, p2, *weights)


def _moe_kernel(x1_ref, res_ref, comb_ref, wgu_ref, wd_ref, g2_ref, b2_ref, o_ref, acc_ref, *, d_expert):
    e = pl.program_id(1)

    @pl.when(e == 0)
    def _():
        acc_ref[...] = jnp.zeros(acc_ref.shape, F32)

    gu = _dot(x1_ref[...], wgu_ref[0])
    comb = comb_ref[...]
    lane = lax.broadcasted_iota(jnp.int32, comb.shape, 1)
    ce = jnp.sum(jnp.where(lane == e, comb, 0.0), axis=-1, keepdims=True)
    g = gu[:, :d_expert]
    h = g * _sigmoid(g) * gu[:, d_expert:] * ce
    acc_ref[...] += _dot(h.astype(BF16), wd_ref[0])

    @pl.when(e == pl.num_programs(1) - 1)
    def _():
        o_ref[...] = _layer_norm(res_ref[...] + acc_ref[...], g2_ref[...], b2_ref[...])


def _moe(x1b, res, comb, w_gu, w_d, g2, b2, tm):
    n, d = x1b.shape
    n_exp, _, two_f = w_gu.shape
    tok = lambda width: pl.BlockSpec((tm, width), lambda i, e: (i, 0))
    return pl.pallas_call(
        functools.partial(_moe_kernel, d_expert=two_f // 2),
        grid=(n // tm, n_exp),
        in_specs=[tok(d), tok(d), tok(LANES),
                  pl.BlockSpec((1, d, two_f), lambda i, e: (e, 0, 0)),
                  pl.BlockSpec((1, two_f // 2, d), lambda i, e: (e, 0, 0)),
                  pl.BlockSpec((1, d), lambda i, e: (0, 0)),
                  pl.BlockSpec((1, d), lambda i, e: (0, 0))],
        out_specs=tok(d),
        out_shape=jax.ShapeDtypeStruct((n, d), F32),
        scratch_shapes=[pltpu.VMEM((tm, d), F32)],
        compiler_params=pltpu.CompilerParams(dimension_semantics=("arbitrary", "arbitrary"),
                                             vmem_limit_bytes=VMEM_LIMIT),
        name="moe",
    )(x1b, res, comb, w_gu, w_d, g2, b2)


def _rope_tables(positions):
    inv_freq = 1.0 / (ROPE_THETA ** (jnp.arange(0, ROT_DIM, 2, dtype=F32) / ROT_DIM))
    ang = positions.astype(F32)[..., None] * inv_freq
    cos, sin = jnp.cos(ang), jnp.sin(ang)
    n = positions.size
    cos, sin = cos.reshape(n, ROT_HALF), sin.reshape(n, ROT_HALF)
    rest = HEAD_DIM - ROT_DIM
    cos_t = jnp.concatenate([cos, cos, jnp.ones((n, rest), F32)], axis=1)
    sin_a = jnp.concatenate([-sin, jnp.zeros((n, rest + ROT_HALF), F32)], axis=1)
    sin_b = jnp.concatenate([jnp.zeros((n, ROT_HALF), F32), sin, jnp.zeros((n, rest), F32)], axis=1)
    rep = LANES // HEAD_DIM
    return tuple(jnp.tile(t, (1, rep)) for t in (cos_t, sin_a, sin_b))


def _tile(n, want):
    t = min(n, want)
    assert n % t == 0, (n, t)
    return t


def kernel(x, p, positions, w_in, diff_lambda, diff_subln_g, w_branch_diff, w_branch_dsa, w_out, ln1_g, ln1_b, w_route_group, b_route_group, w_route_expert, b_route_expert, w_exp_gate, w_exp_up, w_exp_down, w_ple, w_ple_gate, ln2_g, ln2_b):
    batch, seq, d = x.shape
    depth = w_in.shape[0]
    n = batch * seq
    alpha = (2 * depth) ** 0.25
    topk = min(TOPK_MAX, seq // 4)
    d_expert = w_exp_gate.shape[-1]
    assert BRANCH_WIDTH == DIFF_HEADS * 2 * HEAD_DIM == DSA_HEADS * HEAD_DIM == IDX_HEADS * HEAD_DIM
    n_qkv = 7 * BRANCH_WIDTH
    o_ik, o_iw = n_qkv, n_qkv + HEAD_DIM
    o_ga = o_iw + IDX_HEADS
    assert w_in.shape[2] == o_ga + 2 * d

    tm_proj = _tile(n, 512)
    tq_diff = _tile(seq, 512)
    tk_diff = _tile(seq, 1024)
    tq_dsa = _tile(seq, 256)
    cps_dsa = min(4, seq // tq_dsa)
    tm_merge = _tile(n, 256)
    tm_moe = _tile(n, 1024)

    cos_t, sin_a, sin_b = _rope_tables(positions)
    scale = HEAD_DIM ** -0.5
    col_scale = jnp.ones((n_qkv,), F32)
    for seg in (0, 3, 6):
        col_scale = col_scale.at[seg * BRANCH_WIDTH:(seg + 1) * BRANCH_WIDTH].set(scale)

    x2 = x.reshape(n, d)
    for i in range(depth):
        lam_init = 0.8 - 0.6 * math.exp(-0.3 * i)
        wi = w_in[i]
        w_ik = wi[:, o_ik:o_ik + HEAD_DIM]
        w_iw = jnp.pad(wi[:, o_iw:o_iw + IDX_HEADS], ((0, 0), (0, LANES - IDX_HEADS)))
        w_a = jnp.concatenate([wi[:, :n_qkv] * col_scale, w_ik, w_ik, w_iw], axis=1).astype(BF16)
        w_g = wi[:, o_ga:].astype(BF16)
        w_r = jnp.pad(jnp.concatenate([w_route_expert[i], w_route_group[i]], axis=1),
                      ((0, 0), (0, LANES - N_EXPERTS - N_GROUPS))).astype(BF16)
        b_r = jnp.pad(jnp.concatenate([b_route_expert[i], b_route_group[i]]),
                      (0, LANES - N_EXPERTS - N_GROUPS)).reshape(1, LANES)
        w_gu = jnp.concatenate([w_exp_gate[i], w_exp_up[i]], axis=-1).reshape(
            N_EXPERTS, d, 2 * d_expert).astype(BF16)
        w_d = w_exp_down[i].reshape(N_EXPERTS, d_expert, d).astype(BF16)

        dq, dk, dv, sq, sk, sv, iq, ik2, iw = _inproj(x2, w_a, cos_t, sin_a, sin_b, tm_proj)
        ya = _diff_attn(dq, dk, dv, diff_lambda[i], diff_subln_g[i], batch, seq, tq_diff, tk_diff, lam_init)
        yb = _dsa_attn(sq, sk, sv, iq, ik2, iw, batch, seq, tq_dsa, cps_dsa, topk)
        x1b, res, comb = _merge(
            x2, ya, yb, p[i].reshape(n, -1), w_g,
            w_branch_diff[i].astype(BF16), w_branch_dsa[i].astype(BF16), w_out[i].astype(BF16),
            ln1_g[i].reshape(1, d), ln1_b[i].reshape(1, d), w_r, b_r,
            w_ple_gate[i].astype(BF16), w_ple[i].astype(BF16), tm_merge, alpha)
        x2 = _moe(x1b, res, comb, w_gu, w_d, ln2_g[i].reshape(1, d), ln2_b[i].reshape(1, d), tm_moe)
    return x2.reshape(batch, seq, d)
```

```python
import functools
import math

import jax
import jax.numpy as jnp
import numpy as np
from jax import lax
from jax.experimental import pallas as pl
from jax.experimental.pallas import tpu as pltpu

F32 = jnp.float32
BF16 = jnp.bfloat16

HEAD_DIM = 64
DIFF_HEADS = 4
DSA_HEADS = 8
IDX_HEADS = 8
TOPK_MAX = 256
N_GROUPS = 4
EXPERTS_PER_GROUP = 8
N_EXPERTS = N_GROUPS * EXPERTS_PER_GROUP
MOE_EXPERTS_PER_STEP = 4
ROPE_THETA = 500000.0
ROT_DIM = HEAD_DIM // 4
ROT_HALF = ROT_DIM // 2
LN_EPS = 1e-5
NEG_INF = -1e30
LOG2E = math.log2(math.e)
LANES = 128
BRANCH_WIDTH = 512
VMEM_LIMIT = 62 * 1024 * 1024


SIGN_BIT = np.int32(-2 ** 31)
MAGNITUDE_BITS = np.int32(2 ** 31 - 1)
SUBNORMAL_BITS = np.int32(2 ** 23 - 1)


def _float_key(v):
    bits = int(np.float32(v).view(np.int32))
    mag = max((bits & int(MAGNITUDE_BITS)) - int(SUBNORMAL_BITS), 0)
    return -mag if bits < 0 else mag


KEY_LO_INIT = _float_key(NEG_INF)
VALUE_STEPS = 24
TIGHTEN_START = 13
TIGHTEN_EVERY = 4


def _dot_nt(a, b):
    return lax.dot_general(a, b, (((1,), (1,)), ((), ())), preferred_element_type=F32)


def _dot(a, b):
    return jnp.dot(a, b, preferred_element_type=F32)


def _sigmoid(v):
    return 1.0 / (1.0 + jnp.exp(-v))


def _layer_norm(v, g, b):
    mu = jnp.mean(v, axis=-1, keepdims=True)
    d = v - mu
    var = jnp.mean(d * d, axis=-1, keepdims=True)
    return d * lax.rsqrt(var + LN_EPS) * g + b


def _inproj_kernel(x_ref, w_ref, c_ref, sa_ref, sb_ref,
                   dq_ref, dk_ref, dv_ref, sq_ref, sk_ref, sv_ref, iq_ref, ik_ref, iw_ref):
    xb = x_ref[...].astype(BF16)
    cos_t = c_ref[...]
    sin_a = sa_ref[...]
    sin_b = sb_ref[...]

    def rope(h):
        cols = []
        for c in range(h.shape[1] // LANES):
            hc = h[:, c * LANES:(c + 1) * LANES]
            cols.append(hc * cos_t
                        + pltpu.roll(hc, LANES - ROT_HALF, 1) * sin_a
                        + pltpu.roll(hc, ROT_HALF, 1) * sin_b)
        return cols[0] if len(cols) == 1 else jnp.concatenate(cols, axis=1)

    outs = ((dq_ref, True, LOG2E), (dk_ref, True, None), (dv_ref, False, None), (sq_ref, True, LOG2E),
            (sk_ref, True, None), (sv_ref, False, None), (iq_ref, True, None))
    for i, (ref, rotary, mult) in enumerate(outs):
        h = _dot(xb, w_ref[:, i * BRANCH_WIDTH:(i + 1) * BRANCH_WIDTH])
        if mult is not None:
            h = h * mult
        if rotary:
            h = rope(h)
        ref[...] = h.astype(BF16)
    base = len(outs) * BRANCH_WIDTH
    h = _dot(xb, w_ref[:, base:base + 2 * LANES])
    ik_ref[...] = rope(h[:, :LANES]).astype(BF16)
    iw_ref[...] = h[:, LANES:] * (IDX_HEADS ** -0.5)


def _inproj(x2, w_a, cos_t, sin_a, sin_b, tm):
    n, d = x2.shape
    wc = w_a.shape[1]
    tok = lambda width: pl.BlockSpec((tm, width), lambda i: (i, 0))
    out_shape = [jax.ShapeDtypeStruct((n, BRANCH_WIDTH), BF16)] * 7 + [
        jax.ShapeDtypeStruct((n, LANES), BF16), jax.ShapeDtypeStruct((n, LANES), F32)]
    return pl.pallas_call(
        _inproj_kernel,
        grid=(n // tm,),
        in_specs=[tok(d), pl.BlockSpec((d, wc), lambda i: (0, 0)), tok(LANES), tok(LANES), tok(LANES)],
        out_specs=[tok(BRANCH_WIDTH)] * 7 + [tok(LANES), tok(LANES)],
        out_shape=out_shape,
        compiler_params=pltpu.CompilerParams(dimension_semantics=("arbitrary",),
                                             vmem_limit_bytes=VMEM_LIMIT),
        name="inproj",
    )(x2, w_a, cos_t, sin_a, sin_b)


def _lane_cols(s):
    return [s[:, c * LANES:(c + 1) * LANES] for c in range(s.shape[1] // LANES)]


def _softmax_step(cols, vc, m_ref, l_ref, idx):
    mx = functools.reduce(jnp.maximum, cols)
    m_prev = m_ref[idx]
    m_new = jnp.maximum(m_prev, jnp.max(mx, axis=-1, keepdims=True))
    alpha = jnp.exp2(m_prev - m_new)
    ps = [jnp.exp2(c - m_new) for c in cols]
    l_ref[idx] = alpha * l_ref[idx] + functools.reduce(jnp.add, ps)
    m_ref[idx] = m_new
    p = ps[0] if len(ps) == 1 else jnp.concatenate(ps, axis=1)
    return alpha, _dot(p.astype(BF16), vc)


def _diff_kernel(q_ref, k_ref, v_ref, lam_ref, g_ref, o_ref, qm_ref, m_ref, l_ref, acc_ref, *,
                 tq, tk, lam_init):
    qi = pl.program_id(2)
    q = q_ref[...]
    lane = lax.broadcasted_iota(jnp.int32, q.shape, 1)
    zero = jnp.zeros_like(q)
    qm_ref[0] = jnp.where(lane < HEAD_DIM, q, zero)
    qm_ref[1] = jnp.where(lane >= HEAD_DIM, q, zero)
    m_ref[...] = jnp.full(m_ref.shape, NEG_INF, F32)
    l_ref[...] = jnp.zeros(l_ref.shape, F32)
    acc_ref[...] = jnp.zeros(acc_ref.shape, F32)

    def step(j, causal):
        start = pl.multiple_of(j * tk, tk)
        kc = k_ref[pl.ds(start, tk), :]
        vc = v_ref[pl.ds(start, tk), :]
        for c in range(2):
            s = _dot_nt(qm_ref[c], kc)
            if causal:
                row = qi * tq + lax.broadcasted_iota(jnp.int32, s.shape, 0)
                col = j * tk + lax.broadcasted_iota(jnp.int32, s.shape, 1)
                s = jnp.where(col <= row, s, NEG_INF)
            alpha, pv = _softmax_step(_lane_cols(s), vc, m_ref, l_ref, c)
            acc_ref[c] = alpha * acc_ref[c] + pv

    def body(j, carry):
        step(j, False)
        return carry

    n_full = (qi * tq) // tk
    lax.fori_loop(0, n_full, body, 0)
    step(n_full, True)

    lp = lam_ref[...]
    lam = (jnp.exp(jnp.sum(lp[0:1] * lp[1:2], axis=-1, keepdims=True))
           - jnp.exp(jnp.sum(lp[2:3] * lp[3:4], axis=-1, keepdims=True)) + lam_init)
    l0 = jnp.sum(l_ref[0], axis=-1, keepdims=True)
    l1 = jnp.sum(l_ref[1], axis=-1, keepdims=True)
    o = acc_ref[0] / l0 - lam * (acc_ref[1] / l1)
    o = o * lax.rsqrt(jnp.mean(o * o, axis=-1, keepdims=True) + LN_EPS)
    o_ref[...] = (o * g_ref[...] * (1.0 - lam_init)).astype(BF16)


def _diff_attn(dq, dk, dv, lam_params, subln_g, batch, seq, tq, tk, lam_init):
    assert tk % tq == 0 and seq % tk == 0
    nq = seq // tq
    hw = 2 * HEAD_DIM
    kv_spec = pl.BlockSpec((seq, hw), lambda b, h, i: (b, h))
    q_spec = pl.BlockSpec((tq, hw), lambda b, h, i: (b * nq + i, h))
    return pl.pallas_call(
        functools.partial(_diff_kernel, tq=tq, tk=tk, lam_init=lam_init),
        grid=(batch, DIFF_HEADS, nq),
        in_specs=[q_spec, kv_spec, kv_spec,
                  pl.BlockSpec(lam_params.shape, lambda b, h, i: (0, 0)),
                  pl.BlockSpec((1, hw), lambda b, h, i: (0, 0))],
        out_specs=q_spec,
        out_shape=jax.ShapeDtypeStruct(dq.shape, BF16),
        scratch_shapes=[pltpu.VMEM((2, tq, hw), BF16), pltpu.VMEM((2, tq, LANES), F32),
                        pltpu.VMEM((2, tq, LANES), F32), pltpu.VMEM((2, tq, hw), F32)],
        compiler_params=pltpu.CompilerParams(
            dimension_semantics=("arbitrary", "arbitrary", "arbitrary"),
            vmem_limit_bytes=VMEM_LIMIT),
        name="diff_attn",
    )(dq, dk, dv, lam_params, subln_g.reshape(1, hw))


def _key_to_float(key):
    mag = jnp.abs(key)
    bits = jnp.where(mag > 0, mag + SUBNORMAL_BITS, 0)
    return lax.bitcast_convert_type(jnp.where(key < 0, bits | SIGN_BIT, bits), F32)


def _float_to_key(v):
    bits = lax.bitcast_convert_type(v, jnp.int32)
    mag = jnp.maximum((bits & MAGNITUDE_BITS) - SUBNORMAL_BITS, 0)
    return jnp.where(bits < 0, -mag, mag)


def _dsa_kernel(sq_ref, iq_ref, iw_ref, ik_ref, sk_ref, sv_ref, tri_ref, o_ref,
                sc_ref, qm_ref, iqm_ref, wb_ref, m_ref, l_ref, acc_ref, *, tq, cps, topk):
    qi = pl.program_id(1)
    nch = qi + 1
    lane = lax.broadcasted_iota(jnp.int32, (tq, LANES), 1)
    low_half = lane < HEAD_DIM
    row = lax.broadcasted_iota(jnp.int32, (tq, tq), 0)
    col = lax.broadcasted_iota(jnp.int32, (tq, tq), 1)
    causal = col <= row

    iw = iw_ref[...]
    for h in range(DSA_HEADS):
        pair = h // 2
        keep = low_half if h % 2 == 0 else jnp.logical_not(low_half)
        sq_pair = sq_ref[:, pair * LANES:(pair + 1) * LANES]
        iq_pair = iq_ref[:, pair * LANES:(pair + 1) * LANES]
        qm_ref[h] = jnp.where(keep, sq_pair, jnp.zeros_like(sq_pair))
        iqm_ref[h] = jnp.where(keep, iq_pair, jnp.zeros_like(iq_pair))
        wb_ref[h] = jnp.broadcast_to(iw[:, h:h + 1], (tq, LANES))

    def score_chunk(j, masked, top2):
        start = pl.multiple_of(j * tq, tq)
        kc = ik_ref[pl.ds(start, tq), :]
        parts = [jnp.zeros((tq, LANES), F32) for _ in range(tq // LANES)]
        for h in range(IDX_HEADS):
            d = _dot_nt(iqm_ref[h], kc)
            w = wb_ref[h]
            for c in range(tq // LANES):
                parts[c] = parts[c] + w * jnp.maximum(d[:, c * LANES:(c + 1) * LANES], 0.0)
        sc = jnp.concatenate(parts, axis=1) + 0.0
        if masked:
            sc = jnp.where(causal, sc, NEG_INF)
        sc_ref[j] = sc
        max1, max2 = top2
        for part in _lane_cols(sc):
            max2 = jnp.maximum(max2, jnp.minimum(max1, part))
            max1 = jnp.maximum(max1, part)
        return max1, max2

    neg = jnp.full((tq, LANES), NEG_INF, F32)
    top2 = lax.fori_loop(0, qi, lambda j, t: score_chunk(j, False, t), (neg, neg))
    max1, max2 = score_chunk(qi, True, top2)

    n_steps = (qi + cps) // cps
    for t in range(1, cps):
        @pl.when(qi + t < n_steps * cps)
        def _():
            sc_ref[qi + t] = jnp.full((tq, tq), NEG_INF, F32)

    bands = [slice(b * LANES, (b + 1) * LANES) for b in range(tq // LANES)]
    lane_row = lax.broadcasted_iota(jnp.int32, (1, LANES), 1)
    tile = (LANES, LANES)

    def rows_of(vec):
        return jnp.broadcast_to(vec, tile).T

    def per_row(mat, reduce):
        return reduce(mat.T, axis=0, keepdims=True)

    def sweep(b, fn, init):
        def body(j, acc):
            for c, part in enumerate(_lane_cols(sc_ref[j, bands[b], :])):
                acc = fn(acc, part, j, c)
            return acc
        return lax.fori_loop(0, nch, body, init)

    def count_above(thrs):
        thr_bs = [rows_of(t) for t in thrs]
        cnts = [sweep(b, lambda acc, part, j, c, t=t: acc + jnp.where(part > t, 1.0, 0.0),
                      jnp.zeros(tile, F32)) for b, t in enumerate(thr_bs)]
        return [per_row(c, jnp.sum) for c in cnts]

    def band_extent(lo_fs, hi_fs):
        edges = [(rows_of(lo_f), rows_of(hi_f)) for lo_f, hi_f in zip(lo_fs, hi_fs)]

        def fn(acc, part, j, c, lo_b, hi_b):
            top, bot = acc
            return (jnp.maximum(top, jnp.where(part <= hi_b, part, -jnp.inf)),
                    jnp.minimum(bot, jnp.where(part > lo_b, part, jnp.inf)))
        init = (jnp.full(tile, -jnp.inf, F32), jnp.full(tile, jnp.inf, F32))
        ext = [sweep(b, functools.partial(fn, lo_b=lo_b, hi_b=hi_b), init) for b, (lo_b, hi_b) in enumerate(edges)]
        return [(per_row(top, jnp.max), per_row(bot, jnp.min)) for top, bot in ext]

    kks = [jnp.minimum(qi * tq + b * LANES + lane_row + 1, topk).astype(F32) for b in range(len(bands))]

    def unresolved(b, lo, hi, flo):
        return jnp.logical_and(flo != kks[b], hi - 1 > lo)

    def any_set(flags):
        return functools.reduce(jnp.maximum, [jnp.max(jnp.where(f, 1.0, 0.0)) for f in flags])

    def midpoint(it, lo, hi):
        mid_key = (lo >> 1) + (hi >> 1) + (lo & hi & 1)
        mid_val = _float_to_key(0.5 * _key_to_float(lo) + 0.5 * _key_to_float(hi))
        by_value = jnp.logical_and(jnp.logical_and(mid_val > lo, mid_val < hi), it < VALUE_STEPS)
        return jnp.where(by_value, mid_val, mid_key)

    def bisect(it, state):
        mids = [midpoint(it, lo, hi) for lo, hi, _, _ in state]
        cnts = count_above([_key_to_float(mid) for mid in mids])
        out = []
        for b, (lo, hi, flo, fhi) in enumerate(state):
            mid, cnt = mids[b], cnts[b]
            less = cnt < kks[b]
            out.append((jnp.where(less, lo, mid), jnp.where(less, mid, hi),
                        jnp.where(less, flo, cnt), jnp.where(less, cnt, fhi)))
        return tuple(out)

    def tighten(state):
        ext = band_extent([_key_to_float(s[0]) for s in state], [_key_to_float(s[1]) for s in state])
        out = []
        for b, (lo, hi, flo, fhi) in enumerate(state):
            top, bot = ext[b]
            todo = unresolved(b, lo, hi, flo)
            new_lo = jnp.maximum(_float_to_key(bot) - 1, lo)
            new_hi = jnp.minimum(_float_to_key(top), hi)
            out.append((jnp.where(todo, new_lo, lo), jnp.where(todo, new_hi, hi), flo, fhi))
        return tuple(out)

    def search_body(st):
        it, _, state = st
        do_tighten = jnp.logical_and(it >= TIGHTEN_START, (it - TIGHTEN_START) % TIGHTEN_EVERY == 0)
        state = lax.cond(do_tighten, tighten, functools.partial(bisect, it), state)
        todo = any_set([unresolved(b, s[0], s[1], s[2]) for b, s in enumerate(state)])
        return it + 1, todo, state

    hi0s = [_float_to_key(per_row(max1[rows], jnp.max)) for rows in bands]
    lo0s = [jnp.maximum(_float_to_key(per_row(max2[rows], jnp.min)) - 1, KEY_LO_INIT) for rows in bands]
    flo0s = count_above([_key_to_float(lo0) for lo0 in lo0s])
    state0 = tuple((lo0s[b], hi0s[b], flo0s[b], jnp.zeros((1, LANES), F32)) for b in range(len(bands)))
    todo0 = any_set([unresolved(b, s[0], s[1], s[2]) for b, s in enumerate(state0)])
    _, _, state = lax.while_loop(lambda st: st[1] > 0.0, search_body, (jnp.int32(0), todo0, state0))

    tie_rows = [s[2] > kks[b] for b, s in enumerate(state)]
    thr_gts = [_key_to_float(jnp.where(tie_rows[b], s[1], s[0])) for b, s in enumerate(state)]
    tie_vals = [_key_to_float(s[1]) for s in state]
    needs = [jnp.where(tie_rows[b], kks[b] - s[3], 0.0) for b, s in enumerate(state)]
    any_tie = any_set(tie_rows)

    m_ref[...] = jnp.full(m_ref.shape, NEG_INF, F32)
    l_ref[...] = jnp.zeros(l_ref.shape, F32)
    acc_ref[...] = jnp.zeros(acc_ref.shape, F32)

    stack_rows = lambda vecs: jnp.concatenate([rows_of(v) for v in vecs], axis=0)
    thr_gt_b = stack_rows(thr_gts)
    tie_val_b = stack_rows(tie_vals)
    need_b = stack_rows(needs)
    tk = cps * tq

    def attend(g, seen):
        start = pl.multiple_of(g * tk, tk)
        for u in range(cps):
            blk = sc_ref[g * cps + u]
            bias = []
            for b in _lane_cols(blk):
                hit = b > thr_gt_b
                if seen is not None:
                    tied = b == tie_val_b
                    counts = _dot(jnp.where(tied, 1.0, 0.0).astype(BF16), tri_ref[...])
                    rank = seen + counts[:, :LANES]
                    hit = jnp.logical_or(hit, jnp.logical_and(tied, rank <= need_b))
                    seen = seen + counts[:, LANES:]
                bias.append(jnp.where(hit, 0.0, NEG_INF))
            sc_ref[g * cps + u] = jnp.concatenate(bias, axis=1)
        for pair in range(DSA_HEADS // 2):
            kc = sk_ref[pl.ds(start, tk), pl.ds(pair * LANES, LANES)]
            vc = sv_ref[pl.ds(start, tk), pl.ds(pair * LANES, LANES)]
            alphas, pvs = [], []
            for h in (2 * pair, 2 * pair + 1):
                s = _dot_nt(qm_ref[h], kc)
                cols = []
                for u in range(cps):
                    bias = sc_ref[g * cps + u]
                    cols += [a + b for a, b in zip(_lane_cols(s[:, u * tq:(u + 1) * tq]), _lane_cols(bias))]
                alpha, pv = _softmax_step(cols, vc, m_ref, l_ref, h)
                alphas.append(alpha)
                pvs.append(pv)
            alpha_pair = jnp.where(low_half, alphas[0], alphas[1])
            pv_pair = jnp.where(low_half, pvs[0], pvs[1])
            acc_ref[pair] = alpha_pair * acc_ref[pair] + pv_pair
        return seen

    def attend_ties(_):
        lax.fori_loop(0, n_steps, attend, jnp.zeros((tq, LANES), F32))
        return 0

    def attend_plain(_):
        lax.fori_loop(0, n_steps, lambda g, carry: (attend(g, None), carry)[1], 0)
        return 0

    lax.cond(any_tie > 0.0, attend_ties, attend_plain, 0)

    for pair in range(DSA_HEADS // 2):
        l_pair = jnp.where(low_half, jnp.sum(l_ref[2 * pair], axis=-1, keepdims=True),
                           jnp.sum(l_ref[2 * pair + 1], axis=-1, keepdims=True))
        o_ref[:, pair * LANES:(pair + 1) * LANES] = (acc_ref[pair] / l_pair).astype(BF16)


def _dsa_attn(sq, sk, sv, iq, ik2, iw, batch, seq, tq, cps, topk):
    nq = seq // tq
    assert nq % cps == 0 and topk <= 2 * LANES
    k_idx = jnp.arange(LANES)
    tri = jnp.concatenate([(k_idx[:, None] <= k_idx[None, :]).astype(BF16),
                           jnp.ones((LANES, LANES), BF16)], axis=1)
    q_spec = lambda width: pl.BlockSpec((tq, width), lambda b, i: (b * nq + i, 0))
    resident = lambda width: pl.BlockSpec((seq, width), lambda b, i: (b, 0),
                                          pipeline_mode=pl.Buffered(1))
    return pl.pallas_call(
        functools.partial(_dsa_kernel, tq=tq, cps=cps, topk=topk),
        grid=(batch, nq),
        in_specs=[q_spec(BRANCH_WIDTH), q_spec(BRANCH_WIDTH), q_spec(LANES),
                  resident(LANES), resident(BRANCH_WIDTH), resident(BRANCH_WIDTH),
                  pl.BlockSpec(tri.shape, lambda b, i: (0, 0))],
        out_specs=q_spec(BRANCH_WIDTH),
        out_shape=jax.ShapeDtypeStruct(sq.shape, BF16),
        scratch_shapes=[pltpu.VMEM((nq, tq, tq), F32),
                        pltpu.VMEM((DSA_HEADS, tq, LANES), BF16),
                        pltpu.VMEM((IDX_HEADS, tq, LANES), BF16),
                        pltpu.VMEM((IDX_HEADS, tq, LANES), F32),
                        pltpu.VMEM((DSA_HEADS, tq, LANES), F32),
                        pltpu.VMEM((DSA_HEADS, tq, LANES), F32),
                        pltpu.VMEM((DSA_HEADS // 2, tq, LANES), F32)],
        compiler_params=pltpu.CompilerParams(
            dimension_semantics=("arbitrary", "arbitrary"),
            vmem_limit_bytes=VMEM_LIMIT),
        name="dsa_attn",
    )(sq, iq, iw, ik2, sk, sv, tri)


def _merge_kernel(x_ref, ya_ref, yb_ref, p_ref, wg_ref, wbd_ref, wbs_ref, wo_ref, g1_ref, b1_ref,
                  wr_ref, br_ref, wpg_ref, wp_ref, x1_ref, res_ref, comb_ref, *, alpha, d_model):
    x = x_ref[...]
    gates = _dot(x.astype(BF16), wg_ref[...])
    merged = (_sigmoid(gates[:, :d_model]) * _dot(ya_ref[...], wbd_ref[...])
              + _sigmoid(gates[:, d_model:]) * _dot(yb_ref[...], wbs_ref[...]))
    mix = _dot(merged.astype(BF16), wo_ref[...])
    x1 = _layer_norm(alpha * x + mix, g1_ref[...], b1_ref[...])
    x1b = x1.astype(BF16)
    x1_ref[...] = x1b
    ple = _sigmoid(_dot(x1b, wpg_ref[...])) * _dot(p_ref[...].astype(BF16), wp_ref[...])
    res_ref[...] = alpha * x1 + ple

    logits = _dot(x1b, wr_ref[...]) + br_ref[...]
    lane = lax.broadcasted_iota(jnp.int32, logits.shape, 1)
    is_group = jnp.logical_and(lane >= N_EXPERTS, lane < N_EXPERTS + N_GROUPS)
    gl = jnp.where(is_group, logits, NEG_INF)
    gmax = jnp.max(gl, axis=-1, keepdims=True)
    gsum = jnp.sum(jnp.where(is_group, jnp.exp(gl - gmax), 0.0), axis=-1, keepdims=True)
    g_val = 1.0 / gsum
    g_idx = jnp.min(jnp.where(jnp.logical_and(is_group, gl == gmax), lane, 4 * LANES),
                    axis=-1, keepdims=True) - N_EXPERTS
    first = g_idx * EXPERTS_PER_GROUP
    in_group = jnp.logical_and(lane >= first, lane < first + EXPERTS_PER_GROUP)
    el = jnp.where(in_group, logits, NEG_INF)
    e1 = jnp.max(el, axis=-1, keepdims=True)
    i1 = jnp.min(jnp.where(jnp.logical_and(in_group, el == e1), lane, 4 * LANES), axis=-1, keepdims=True)
    el2 = jnp.where(lane == i1, NEG_INF, el)
    e2 = jnp.max(el2, axis=-1, keepdims=True)
    i2 = jnp.min(jnp.where(jnp.logical_and(in_group, el2 == e2), lane, 4 * LANES), axis=-1, keepdims=True)
    p2 = jnp.exp(e2 - e1)
    w1 = 1.0 / (1.0 + p2)
    w2 = p2 / (1.0 + p2)
    comb_ref[...] = g_val * (jnp.where(lane == i1, w1, 0.0) + jnp.where(lane == i2, w2, 0.0))


def _merge(x2, ya, yb, p2, w_g, w_bd, w_bs, w_o, g1, b1, w_r, b_r, w_pg, w_p, tm, alpha):
    n, d = x2.shape
    tok = lambda width: pl.BlockSpec((tm, width), lambda i: (i, 0))
    full = lambda a: pl.BlockSpec(a.shape, lambda i: (0, 0))
    weights = (w_g, w_bd, w_bs, w_o, g1, b1, w_r, b_r, w_pg, w_p)
    return pl.pallas_call(
        functools.partial(_merge_kernel, alpha=alpha, d_model=d),
        grid=(n // tm,),
        in_specs=[tok(d), tok(BRANCH_WIDTH), tok(BRANCH_WIDTH), tok(p2.shape[1])] + [full(w) for w in weights],
        out_specs=[tok(d), tok(d), tok(LANES)],
        out_shape=[jax.ShapeDtypeStruct((n, d), BF16), jax.ShapeDtypeStruct((n, d), F32),
                   jax.ShapeDtypeStruct((n, LANES), F32)],
        compiler_params=pltpu.CompilerParams(dimension_semantics=("arbitrary",),
                                             vmem_limit_bytes=VMEM_LIMIT),
        name="merge",
    )(x2, ya, yb, p2, *weights)


def _moe_kernel(x1_ref, res_ref, comb_ref, wgu_ref, wd_ref, g2_ref, b2_ref, o_ref, h_ref, acc_ref, *,
                d_expert, eps):
    s = pl.program_id(1)
    x = x1_ref[...]
    comb = comb_ref[...]
    lane = lax.broadcasted_iota(jnp.int32, comb.shape, 1)
    for e in range(eps):
        gu = _dot(x, wgu_ref[e])
        ce = jnp.sum(jnp.where(lane == s * eps + e, comb, 0.0), axis=-1, keepdims=True)
        g = gu[:, :d_expert]
        h = g * _sigmoid(g) * gu[:, d_expert:] * ce
        h_ref[:, e * d_expert:(e + 1) * d_expert] = h.astype(BF16)
    y = _dot(h_ref[...], wd_ref[...])

    @pl.when(s == 0)
    def _():
        acc_ref[...] = y

    @pl.when(s > 0)
    def _():
        acc_ref[...] += y

    @pl.when(s == pl.num_programs(1) - 1)
    def _():
        o_ref[...] = _layer_norm(res_ref[...] + acc_ref[...], g2_ref[...], b2_ref[...])


def _moe(x1b, res, comb, w_gu, w_d, g2, b2, tm, eps):
    n, d = x1b.shape
    n_exp, _, two_f = w_gu.shape
    d_expert = two_f // 2
    assert n_exp % eps == 0 and w_d.shape == (n_exp * d_expert, d)
    tok = lambda width: pl.BlockSpec((tm, width), lambda i, s: (i, 0))
    return pl.pallas_call(
        functools.partial(_moe_kernel, d_expert=d_expert, eps=eps),
        grid=(n // tm, n_exp // eps),
        in_specs=[tok(d), tok(d), tok(LANES),
                  pl.BlockSpec((eps, d, two_f), lambda i, s: (s, 0, 0)),
                  pl.BlockSpec((eps * d_expert, d), lambda i, s: (s, 0)),
                  pl.BlockSpec((1, d), lambda i, s: (0, 0)),
                  pl.BlockSpec((1, d), lambda i, s: (0, 0))],
        out_specs=tok(d),
        out_shape=jax.ShapeDtypeStruct((n, d), F32),
        scratch_shapes=[pltpu.VMEM((tm, eps * d_expert), BF16), pltpu.VMEM((tm, d), F32)],
        compiler_params=pltpu.CompilerParams(dimension_semantics=("arbitrary", "arbitrary"),
                                             vmem_limit_bytes=VMEM_LIMIT),
        name="moe",
    )(x1b, res, comb, w_gu, w_d, g2, b2)


def _rope_tables(positions):
    inv_freq = 1.0 / (ROPE_THETA ** (jnp.arange(0, ROT_DIM, 2, dtype=F32) / ROT_DIM))
    ang = positions.astype(F32)[..., None] * inv_freq
    cos, sin = jnp.cos(ang), jnp.sin(ang)
    n = positions.size
    cos, sin = cos.reshape(n, ROT_HALF), sin.reshape(n, ROT_HALF)
    rest = HEAD_DIM - ROT_DIM
    cos_t = jnp.concatenate([cos, cos, jnp.ones((n, rest), F32)], axis=1)
    sin_a = jnp.concatenate([-sin, jnp.zeros((n, rest + ROT_HALF), F32)], axis=1)
    sin_b = jnp.concatenate([jnp.zeros((n, ROT_HALF), F32), sin, jnp.zeros((n, rest), F32)], axis=1)
    rep = LANES // HEAD_DIM
    return tuple(jnp.tile(t, (1, rep)) for t in (cos_t, sin_a, sin_b))


def _tile(n, want):
    t = min(n, want)
    assert n % t == 0, (n, t)
    return t


def kernel(x, p, positions, w_in, diff_lambda, diff_subln_g, w_branch_diff, w_branch_dsa, w_out, ln1_g, ln1_b, w_route_group, b_route_group, w_route_expert, b_route_expert, w_exp_gate, w_exp_up, w_exp_down, w_ple, w_ple_gate, ln2_g, ln2_b):
    batch, seq, d = x.shape
    depth = w_in.shape[0]
    n = batch * seq
    alpha = (2 * depth) ** 0.25
    topk = min(TOPK_MAX, seq // 4)
    d_expert = w_exp_gate.shape[-1]
    assert BRANCH_WIDTH == DIFF_HEADS * 2 * HEAD_DIM == DSA_HEADS * HEAD_DIM == IDX_HEADS * HEAD_DIM
    n_qkv = 7 * BRANCH_WIDTH
    o_ik, o_iw = n_qkv, n_qkv + HEAD_DIM
    o_ga = o_iw + IDX_HEADS
    assert w_in.shape[2] == o_ga + 2 * d

    tm_proj = _tile(n, 512)
    tq_diff = _tile(seq, 1024)
    tk_diff = _tile(seq, 1024)
    tq_dsa = _tile(seq, 256)
    cps_dsa = min(4, seq // tq_dsa)
    tm_merge = _tile(n, 256)
    tm_moe = _tile(n, 1024)

    cos_t, sin_a, sin_b = _rope_tables(positions)
    scale = HEAD_DIM ** -0.5
    col_scale = jnp.ones((n_qkv,), F32)
    for seg in (0, 3, 6):
        col_scale = col_scale.at[seg * BRANCH_WIDTH:(seg + 1) * BRANCH_WIDTH].set(scale)

    x2 = x.reshape(n, d)
    for i in range(depth):
        lam_init = 0.8 - 0.6 * math.exp(-0.3 * i)
        wi = w_in[i]
        w_ik = wi[:, o_ik:o_ik + HEAD_DIM]
        w_iw = jnp.pad(wi[:, o_iw:o_iw + IDX_HEADS], ((0, 0), (0, LANES - IDX_HEADS)))
        w_a = jnp.concatenate([wi[:, :n_qkv] * col_scale, w_ik, w_ik, w_iw], axis=1).astype(BF16)
        w_g = wi[:, o_ga:].astype(BF16)
        w_r = jnp.pad(jnp.concatenate([w_route_expert[i], w_route_group[i]], axis=1),
                      ((0, 0), (0, LANES - N_EXPERTS - N_GROUPS))).astype(BF16)
        b_r = jnp.pad(jnp.concatenate([b_route_expert[i], b_route_group[i]]),
                      (0, LANES - N_EXPERTS - N_GROUPS)).reshape(1, LANES)
        w_gu = jnp.concatenate([w_exp_gate[i], w_exp_up[i]], axis=-1).reshape(
            N_EXPERTS, d, 2 * d_expert).astype(BF16)
        w_d = w_exp_down[i].reshape(N_EXPERTS * d_expert, d).astype(BF16)

        dq, dk, dv, sq, sk, sv, iq, ik2, iw = _inproj(x2, w_a, cos_t, sin_a, sin_b, tm_proj)
        ya = _diff_attn(dq, dk, dv, diff_lambda[i], diff_subln_g[i], batch, seq, tq_diff, tk_diff, lam_init)
        yb = _dsa_attn(sq, sk, sv, iq, ik2, iw, batch, seq, tq_dsa, cps_dsa, topk)
        x1b, res, comb = _merge(
            x2, ya, yb, p[i].reshape(n, -1), w_g,
            w_branch_diff[i].astype(BF16), w_branch_dsa[i].astype(BF16), w_out[i].astype(BF16),
            ln1_g[i].reshape(1, d), ln1_b[i].reshape(1, d), w_r, b_r,
            w_ple_gate[i].astype(BF16), w_ple[i].astype(BF16), tm_merge, alpha)
        x2 = _moe(x1b, res, comb, w_gu, w_d, ln2_g[i].reshape(1, d), ln2_b[i].reshape(1, d), tm_moe,
                   MOE_EXPERTS_PER_STEP)
    return x2.reshape(batch, seq, d)
```

```python
import functools
import math

import jax
import jax.numpy as jnp
import numpy as np
from jax import lax
from jax.experimental import pallas as pl
from jax.experimental.pallas import tpu as pltpu

F32 = jnp.float32
BF16 = jnp.bfloat16

HEAD_DIM = 64
DIFF_HEADS = 4
DSA_HEADS = 8
IDX_HEADS = 8
TOPK_MAX = 256
N_GROUPS = 4
EXPERTS_PER_GROUP = 8
N_EXPERTS = N_GROUPS * EXPERTS_PER_GROUP
MOE_EXPERTS_PER_STEP = 4
ROPE_THETA = 500000.0
ROT_DIM = HEAD_DIM // 4
ROT_HALF = ROT_DIM // 2
LN_EPS = 1e-5
NEG_INF = -1e30
LOG2E = math.log2(math.e)
LANES = 128
BRANCH_WIDTH = 512
VMEM_LIMIT = 62 * 1024 * 1024


SIGN_BIT = np.int32(-2 ** 31)
MAGNITUDE_BITS = np.int32(2 ** 31 - 1)
SUBNORMAL_BITS = np.int32(2 ** 23 - 1)


def _float_key(v):
    bits = int(np.float32(v).view(np.int32))
    mag = max((bits & int(MAGNITUDE_BITS)) - int(SUBNORMAL_BITS), 0)
    return -mag if bits < 0 else mag


KEY_LO_INIT = _float_key(NEG_INF)
VALUE_STEPS = 24
TIGHTEN_START = 13
TIGHTEN_EVERY = 4
SCORE_CHUNK = 256


def _dot_nt(a, b):
    return lax.dot_general(a, b, (((1,), (1,)), ((), ())), preferred_element_type=F32)


def _dot(a, b):
    return jnp.dot(a, b, preferred_element_type=F32)


def _sigmoid(v):
    return 1.0 / (1.0 + jnp.exp(-v))


def _layer_norm(v, g, b):
    mu = jnp.mean(v, axis=-1, keepdims=True)
    d = v - mu
    var = jnp.mean(d * d, axis=-1, keepdims=True)
    return d * lax.rsqrt(var + LN_EPS) * g + b


def _inproj_kernel(x_ref, w_ref, c_ref, sa_ref, sb_ref,
                   dq_ref, dk_ref, dv_ref, sq_ref, sk_ref, sv_ref, iq_ref, ik_ref, iw_ref):
    xb = x_ref[...].astype(BF16)
    cos_t = c_ref[...]
    sin_a = sa_ref[...]
    sin_b = sb_ref[...]

    def rope(h):
        cols = []
        for c in range(h.shape[1] // LANES):
            hc = h[:, c * LANES:(c + 1) * LANES]
            cols.append(hc * cos_t
                        + pltpu.roll(hc, LANES - ROT_HALF, 1) * sin_a
                        + pltpu.roll(hc, ROT_HALF, 1) * sin_b)
        return cols[0] if len(cols) == 1 else jnp.concatenate(cols, axis=1)

    outs = ((dq_ref, True, LOG2E), (dk_ref, True, None), (dv_ref, False, None), (sq_ref, True, LOG2E),
            (sk_ref, True, None), (sv_ref, False, None), (iq_ref, True, None))
    for i, (ref, rotary, mult) in enumerate(outs):
        h = _dot(xb, w_ref[:, i * BRANCH_WIDTH:(i + 1) * BRANCH_WIDTH])
        if mult is not None:
            h = h * mult
        if rotary:
            h = rope(h)
        ref[...] = h.astype(BF16)
    base = len(outs) * BRANCH_WIDTH
    h = _dot(xb, w_ref[:, base:base + 2 * LANES])
    ik_ref[...] = rope(h[:, :LANES]).astype(BF16)
    iw_ref[...] = h[:, LANES:] * (IDX_HEADS ** -0.5)


def _inproj(x2, w_a, cos_t, sin_a, sin_b, tm):
    n, d = x2.shape
    wc = w_a.shape[1]
    tok = lambda width: pl.BlockSpec((tm, width), lambda i: (i, 0))
    out_shape = [jax.ShapeDtypeStruct((n, BRANCH_WIDTH), BF16)] * 7 + [
        jax.ShapeDtypeStruct((n, LANES), BF16), jax.ShapeDtypeStruct((n, LANES), F32)]
    return pl.pallas_call(
        _inproj_kernel,
        grid=(n // tm,),
        in_specs=[tok(d), pl.BlockSpec((d, wc), lambda i: (0, 0)), tok(LANES), tok(LANES), tok(LANES)],
        out_specs=[tok(BRANCH_WIDTH)] * 7 + [tok(LANES), tok(LANES)],
        out_shape=out_shape,
        compiler_params=pltpu.CompilerParams(dimension_semantics=("arbitrary",),
                                             vmem_limit_bytes=VMEM_LIMIT),
        name="inproj",
    )(x2, w_a, cos_t, sin_a, sin_b)


def _lane_cols(s):
    return [s[:, c * LANES:(c + 1) * LANES] for c in range(s.shape[1] // LANES)]


def _softmax_step(cols, vc, m_ref, l_ref, idx):
    mx = functools.reduce(jnp.maximum, cols)
    m_prev = m_ref[idx]
    m_new = jnp.maximum(m_prev, jnp.max(mx, axis=-1, keepdims=True))
    alpha = jnp.exp2(m_prev - m_new)
    ps = [jnp.exp2(c - m_new) for c in cols]
    l_ref[idx] = alpha * l_ref[idx] + functools.reduce(jnp.add, ps)
    m_ref[idx] = m_new
    p = ps[0] if len(ps) == 1 else jnp.concatenate(ps, axis=1)
    return alpha, _dot(p.astype(BF16), vc)


def _diff_kernel(q_ref, k_ref, v_ref, lam_ref, g_ref, o_ref, qm_ref, m_ref, l_ref, acc_ref, *,
                 tq, tk, lam_init):
    qi = pl.program_id(2)
    q = q_ref[...]
    lane = lax.broadcasted_iota(jnp.int32, q.shape, 1)
    zero = jnp.zeros_like(q)
    qm_ref[0] = jnp.where(lane < HEAD_DIM, q, zero)
    qm_ref[1] = jnp.where(lane >= HEAD_DIM, q, zero)
    m_ref[...] = jnp.full(m_ref.shape, NEG_INF, F32)
    l_ref[...] = jnp.zeros(l_ref.shape, F32)
    acc_ref[...] = jnp.zeros(acc_ref.shape, F32)

    def step(j, causal):
        start = pl.multiple_of(j * tk, tk)
        kc = k_ref[pl.ds(start, tk), :]
        vc = v_ref[pl.ds(start, tk), :]
        for c in range(2):
            s = _dot_nt(qm_ref[c], kc)
            if causal:
                row = qi * tq + lax.broadcasted_iota(jnp.int32, s.shape, 0)
                col = j * tk + lax.broadcasted_iota(jnp.int32, s.shape, 1)
                s = jnp.where(col <= row, s, NEG_INF)
            alpha, pv = _softmax_step(_lane_cols(s), vc, m_ref, l_ref, c)
            acc_ref[c] = alpha * acc_ref[c] + pv

    def body(j, carry):
        step(j, False)
        return carry

    n_full = (qi * tq) // tk
    lax.fori_loop(0, n_full, body, 0)
    step(n_full, True)

    lp = lam_ref[...]
    lam = (jnp.exp(jnp.sum(lp[0:1] * lp[1:2], axis=-1, keepdims=True))
           - jnp.exp(jnp.sum(lp[2:3] * lp[3:4], axis=-1, keepdims=True)) + lam_init)
    l0 = jnp.sum(l_ref[0], axis=-1, keepdims=True)
    l1 = jnp.sum(l_ref[1], axis=-1, keepdims=True)
    o = acc_ref[0] / l0 - lam * (acc_ref[1] / l1)
    o = o * lax.rsqrt(jnp.mean(o * o, axis=-1, keepdims=True) + LN_EPS)
    o_ref[...] = (o * g_ref[...] * (1.0 - lam_init)).astype(BF16)


def _diff_attn(dq, dk, dv, lam_params, subln_g, batch, seq, tq, tk, lam_init):
    assert tk % tq == 0 and seq % tk == 0
    nq = seq // tq
    hw = 2 * HEAD_DIM
    kv_spec = pl.BlockSpec((seq, hw), lambda b, h, i: (b, h))
    q_spec = pl.BlockSpec((tq, hw), lambda b, h, i: (b * nq + i, h))
    return pl.pallas_call(
        functools.partial(_diff_kernel, tq=tq, tk=tk, lam_init=lam_init),
        grid=(batch, DIFF_HEADS, nq),
        in_specs=[q_spec, kv_spec, kv_spec,
                  pl.BlockSpec(lam_params.shape, lambda b, h, i: (0, 0)),
                  pl.BlockSpec((1, hw), lambda b, h, i: (0, 0))],
        out_specs=q_spec,
        out_shape=jax.ShapeDtypeStruct(dq.shape, BF16),
        scratch_shapes=[pltpu.VMEM((2, tq, hw), BF16), pltpu.VMEM((2, tq, LANES), F32),
                        pltpu.VMEM((2, tq, LANES), F32), pltpu.VMEM((2, tq, hw), F32)],
        compiler_params=pltpu.CompilerParams(
            dimension_semantics=("arbitrary", "arbitrary", "arbitrary"),
            vmem_limit_bytes=VMEM_LIMIT),
        name="diff_attn",
    )(dq, dk, dv, lam_params, subln_g.reshape(1, hw))


def _key_to_float(key):
    mag = jnp.abs(key)
    bits = jnp.where(mag > 0, mag + SUBNORMAL_BITS, 0)
    return lax.bitcast_convert_type(jnp.where(key < 0, bits | SIGN_BIT, bits), F32)


def _float_to_key(v):
    bits = lax.bitcast_convert_type(v, jnp.int32)
    mag = jnp.maximum((bits & MAGNITUDE_BITS) - SUBNORMAL_BITS, 0)
    return jnp.where(bits < 0, -mag, mag)


def _causal_steps(seq, tq, tk):
    pairs = [(i, g) for i in range(seq // tq) for g in range(-(-(i + 1) * tq // tk))]
    return (jnp.asarray([p[0] for p in pairs], jnp.int32), jnp.asarray([p[1] for p in pairs], jnp.int32))


def _select_kernel(it_ref, gt_ref, iq_ref, iw_ref, ik_ref, tri_ref, o_ref,
                   sc_ref, iqm_ref, wb_ref, thr_ref, val_ref, need_ref, seen_ref, flag_ref, *,
                   tq, tk, topk):
    t = pl.program_id(1)
    qi = it_ref[t]
    g = gt_ref[t]
    ck = SCORE_CHUNK
    cps = tk // ck
    bands = [slice(b * LANES, (b + 1) * LANES) for b in range(tq // LANES)]
    tile = (LANES, LANES)

    def rows_of(vec):
        return jnp.broadcast_to(vec, tile).T

    def per_row(mat, reduce):
        return reduce(mat.T, axis=0, keepdims=True)

    @pl.when(g == 0)
    def _search():
        lane = lax.broadcasted_iota(jnp.int32, (tq, LANES), 1)
        low_half = lane < HEAD_DIM
        nch = (qi + 1) * tq // ck
        n_plain = qi * tq // ck

        iw = iw_ref[...]
        for h in range(IDX_HEADS):
            pair = h // 2
            keep = low_half if h % 2 == 0 else jnp.logical_not(low_half)
            iq_pair = iq_ref[:, pair * LANES:(pair + 1) * LANES]
            iqm_ref[h] = jnp.where(keep, iq_pair, jnp.zeros_like(iq_pair))
            wb_ref[h] = jnp.broadcast_to(iw[:, h:h + 1], (tq, LANES))

        def score_chunk(j, masked, top2):
            start = pl.multiple_of(j * ck, ck)
            kc = ik_ref[pl.ds(start, ck), :]
            parts = [jnp.zeros((tq, LANES), F32) for _ in range(ck // LANES)]
            for h in range(IDX_HEADS):
                d = _dot_nt(iqm_ref[h], kc)
                w = wb_ref[h]
                for c in range(ck // LANES):
                    parts[c] = parts[c] + w * jnp.maximum(d[:, c * LANES:(c + 1) * LANES], 0.0)
            sc = jnp.concatenate(parts, axis=1) + 0.0
            if masked:
                row = qi * tq + lax.broadcasted_iota(jnp.int32, (tq, ck), 0)
                col = start + lax.broadcasted_iota(jnp.int32, (tq, ck), 1)
                sc = jnp.where(col <= row, sc, NEG_INF)
            sc_ref[j] = sc
            max1, max2 = top2
            for part in _lane_cols(sc):
                max2 = jnp.maximum(max2, jnp.minimum(max1, part))
                max1 = jnp.maximum(max1, part)
            return max1, max2

        neg = jnp.full((tq, LANES), NEG_INF, F32)
        top2 = lax.fori_loop(0, n_plain, lambda j, s: score_chunk(j, False, s), (neg, neg))
        max1, max2 = lax.fori_loop(n_plain, nch, lambda j, s: score_chunk(j, True, s), top2)

        n_steps = (nch + cps - 1) // cps

        def fill(j, carry):
            sc_ref[j] = jnp.full((tq, ck), NEG_INF, F32)
            return carry
        lax.fori_loop(nch, n_steps * cps, fill, 0)

        lane_row = lax.broadcasted_iota(jnp.int32, (1, LANES), 1)

        def sweep(b, fn, init):
            def body(j, acc):
                for c, part in enumerate(_lane_cols(sc_ref[j, bands[b], :])):
                    acc = fn(acc, part, j, c)
                return acc
            return lax.fori_loop(0, nch, body, init)

        def count_above(thrs):
            thr_bs = [rows_of(v) for v in thrs]
            cnts = [sweep(b, lambda acc, part, j, c, v=v: acc + jnp.where(part > v, 1.0, 0.0),
                          jnp.zeros(tile, F32)) for b, v in enumerate(thr_bs)]
            return [per_row(c, jnp.sum) for c in cnts]

        def band_extent(lo_fs, hi_fs):
            edges = [(rows_of(lo_f), rows_of(hi_f)) for lo_f, hi_f in zip(lo_fs, hi_fs)]

            def fn(acc, part, j, c, lo_b, hi_b):
                top, bot = acc
                return (jnp.maximum(top, jnp.where(part <= hi_b, part, -jnp.inf)),
                        jnp.minimum(bot, jnp.where(part > lo_b, part, jnp.inf)))
            init = (jnp.full(tile, -jnp.inf, F32), jnp.full(tile, jnp.inf, F32))
            ext = [sweep(b, functools.partial(fn, lo_b=lo_b, hi_b=hi_b), init)
                   for b, (lo_b, hi_b) in enumerate(edges)]
            return [(per_row(top, jnp.max), per_row(bot, jnp.min)) for top, bot in ext]

        kks = [jnp.minimum(qi * tq + b * LANES + lane_row + 1, topk).astype(F32) for b in range(len(bands))]

        def unresolved(b, lo, hi, flo):
            return jnp.logical_and(flo != kks[b], hi - 1 > lo)

        def any_set(flags):
            return functools.reduce(jnp.maximum, [jnp.max(jnp.where(f, 1.0, 0.0)) for f in flags])

        def midpoint(it, lo, hi):
            mid_key = (lo >> 1) + (hi >> 1) + (lo & hi & 1)
            mid_val = _float_to_key(0.5 * _key_to_float(lo) + 0.5 * _key_to_float(hi))
            by_value = jnp.logical_and(jnp.logical_and(mid_val > lo, mid_val < hi), it < VALUE_STEPS)
            return jnp.where(by_value, mid_val, mid_key)

        def bisect(it, state):
            mids = [midpoint(it, lo, hi) for lo, hi, _, _ in state]
            cnts = count_above([_key_to_float(mid) for mid in mids])
            out = []
            for b, (lo, hi, flo, fhi) in enumerate(state):
                mid, cnt = mids[b], cnts[b]
                less = cnt < kks[b]
                out.append((jnp.where(less, lo, mid), jnp.where(less, mid, hi),
                            jnp.where(less, flo, cnt), jnp.where(less, cnt, fhi)))
            return tuple(out)

        def tighten(state):
            ext = band_extent([_key_to_float(s[0]) for s in state], [_key_to_float(s[1]) for s in state])
            out = []
            for b, (lo, hi, flo, fhi) in enumerate(state):
                top, bot = ext[b]
                todo = unresolved(b, lo, hi, flo)
                new_lo = jnp.maximum(_float_to_key(bot) - 1, lo)
                new_hi = jnp.minimum(_float_to_key(top), hi)
                out.append((jnp.where(todo, new_lo, lo), jnp.where(todo, new_hi, hi), flo, fhi))
            return tuple(out)

        def search_body(st):
            it, _, state = st
            do_tighten = jnp.logical_and(it >= TIGHTEN_START, (it - TIGHTEN_START) % TIGHTEN_EVERY == 0)
            state = lax.cond(do_tighten, tighten, functools.partial(bisect, it), state)
            todo = any_set([unresolved(b, s[0], s[1], s[2]) for b, s in enumerate(state)])
            return it + 1, todo, state

        hi0s = [_float_to_key(per_row(max1[rows], jnp.max)) for rows in bands]
        lo0s = [jnp.maximum(_float_to_key(per_row(max2[rows], jnp.min)) - 1, KEY_LO_INIT) for rows in bands]
        flo0s = count_above([_key_to_float(lo0) for lo0 in lo0s])
        state0 = tuple((lo0s[b], hi0s[b], flo0s[b], jnp.zeros((1, LANES), F32)) for b in range(len(bands)))
        todo0 = any_set([unresolved(b, s[0], s[1], s[2]) for b, s in enumerate(state0)])
        _, _, state = lax.while_loop(lambda st: st[1] > 0.0, search_body, (jnp.int32(0), todo0, state0))

        tie_rows = [s[2] > kks[b] for b, s in enumerate(state)]
        stack_rows = lambda vecs: jnp.concatenate([rows_of(v) for v in vecs], axis=0)
        thr_ref[...] = stack_rows([_key_to_float(jnp.where(tie_rows[b], s[1], s[0])) for b, s in enumerate(state)])
        val_ref[...] = stack_rows([_key_to_float(s[1]) for s in state])
        need_ref[...] = stack_rows([jnp.where(tie_rows[b], kks[b] - s[3], 0.0) for b, s in enumerate(state)])
        seen_ref[...] = jnp.zeros(seen_ref.shape, F32)
        flag_ref[0] = (any_set(tie_rows) > 0.0).astype(jnp.int32)

    thr_b = thr_ref[...]

    def write_tile(with_ties):
        def run(_):
            seen = seen_ref[...] if with_ties else None
            for u in range(cps):
                blk = sc_ref[g * cps + u]
                for c, part in enumerate(_lane_cols(blk)):
                    hit = part > thr_b
                    if with_ties:
                        tied = part == val_ref[...]
                        counts = _dot(jnp.where(tied, 1.0, 0.0).astype(BF16), tri_ref[...])
                        rank = seen + counts[:, :LANES]
                        hit = jnp.logical_or(hit, jnp.logical_and(tied, rank <= need_ref[...]))
                        seen = seen + counts[:, LANES:]
                    o_ref[:, u * ck + c * LANES:u * ck + (c + 1) * LANES] = jnp.where(hit, 0.0, NEG_INF)
            if with_ties:
                seen_ref[...] = seen
            return 0
        return run

    lax.cond(flag_ref[0] > 0, write_tile(True), write_tile(False), 0)


def _select(iq, ik2, iw, batch, seq, tq, tk, topk):
    ck = SCORE_CHUNK
    assert tq % LANES == 0 and tq % ck == 0 and tk % ck == 0 and seq % tk == 0 and topk <= 2 * LANES
    nq = seq // tq
    i_tab, g_tab = _causal_steps(seq, tq, tk)
    k_idx = jnp.arange(LANES)
    tri = jnp.concatenate([(k_idx[:, None] <= k_idx[None, :]).astype(BF16),
                           jnp.ones((LANES, LANES), BF16)], axis=1)
    q_spec = lambda width: pl.BlockSpec((tq, width), lambda b, t, it, gt: (b * nq + it[t], 0))
    stat = pltpu.VMEM((tq, LANES), F32)
    return pl.pallas_call(
        functools.partial(_select_kernel, tq=tq, tk=tk, topk=topk),
        grid_spec=pltpu.PrefetchScalarGridSpec(
            num_scalar_prefetch=2,
            grid=(batch, i_tab.shape[0]),
            in_specs=[q_spec(BRANCH_WIDTH), q_spec(LANES),
                      pl.BlockSpec((seq, LANES), lambda b, t, it, gt: (b, 0), pipeline_mode=pl.Buffered(1)),
                      pl.BlockSpec(tri.shape, lambda b, t, it, gt: (0, 0))],
            out_specs=pl.BlockSpec((tq, tk), lambda b, t, it, gt: (b * nq + it[t], gt[t])),
            scratch_shapes=[pltpu.VMEM((seq // ck, tq, ck), F32),
                            pltpu.VMEM((IDX_HEADS, tq, LANES), BF16),
                            pltpu.VMEM((IDX_HEADS, tq, LANES), F32),
                            stat, stat, stat, stat,
                            pltpu.SMEM((1,), jnp.int32)]),
        out_shape=jax.ShapeDtypeStruct((batch * seq, seq), F32),
        compiler_params=pltpu.CompilerParams(
            dimension_semantics=("arbitrary", "arbitrary"),
            vmem_limit_bytes=VMEM_LIMIT),
        name="select",
    )(i_tab, g_tab, iq, iw, ik2, tri)


def _sparse_attn_kernel(it_ref, gt_ref, q_ref, k_ref, v_ref, bias_ref, o_ref,
                        qm_ref, m_ref, l_ref, acc_ref, *, tq, tk):
    t = pl.program_id(1)
    qi = it_ref[t]
    g = gt_ref[t]
    lane = lax.broadcasted_iota(jnp.int32, (tq, LANES), 1)
    low_half = lane < HEAD_DIM

    @pl.when(g == 0)
    def _():
        for h in range(DSA_HEADS):
            pair = h // 2
            keep = low_half if h % 2 == 0 else jnp.logical_not(low_half)
            q_pair = q_ref[:, pair * LANES:(pair + 1) * LANES]
            qm_ref[h] = jnp.where(keep, q_pair, jnp.zeros_like(q_pair))
        m_ref[...] = jnp.full(m_ref.shape, NEG_INF, F32)
        l_ref[...] = jnp.zeros(l_ref.shape, F32)
        acc_ref[...] = jnp.zeros(acc_ref.shape, F32)

    for pair in range(DSA_HEADS // 2):
        kc = k_ref[:, pair * LANES:(pair + 1) * LANES]
        vc = v_ref[:, pair * LANES:(pair + 1) * LANES]
        alphas, pvs = [], []
        for h in (2 * pair, 2 * pair + 1):
            s = _dot_nt(qm_ref[h], kc)
            cols = [a + b for a, b in zip(_lane_cols(s), _lane_cols(bias_ref[...]))]
            alpha, pv = _softmax_step(cols, vc, m_ref, l_ref, h)
            alphas.append(alpha)
            pvs.append(pv)
        alpha_pair = jnp.where(low_half, alphas[0], alphas[1])
        pv_pair = jnp.where(low_half, pvs[0], pvs[1])
        acc_ref[pair] = alpha_pair * acc_ref[pair] + pv_pair

    @pl.when((g + 1) * tk >= (qi + 1) * tq)
    def _():
        for pair in range(DSA_HEADS // 2):
            l_pair = jnp.where(low_half, jnp.sum(l_ref[2 * pair], axis=-1, keepdims=True),
                               jnp.sum(l_ref[2 * pair + 1], axis=-1, keepdims=True))
            o_ref[:, pair * LANES:(pair + 1) * LANES] = (acc_ref[pair] / l_pair).astype(BF16)


def _sparse_attn(sq, sk, sv, bias, batch, seq, tq, tk):
    assert tq == tk and seq % tk == 0
    nq, nk = seq // tq, seq // tk
    i_tab, g_tab = _causal_steps(seq, tq, tk)
    q_spec = pl.BlockSpec((tq, BRANCH_WIDTH), lambda b, t, it, gt: (b * nq + it[t], 0))
    kv_spec = pl.BlockSpec((tk, BRANCH_WIDTH), lambda b, t, it, gt: (b * nk + gt[t], 0))
    return pl.pallas_call(
        functools.partial(_sparse_attn_kernel, tq=tq, tk=tk),
        grid_spec=pltpu.PrefetchScalarGridSpec(
            num_scalar_prefetch=2,
            grid=(batch, i_tab.shape[0]),
            in_specs=[q_spec, kv_spec, kv_spec,
                      pl.BlockSpec((tq, tk), lambda b, t, it, gt: (b * nq + it[t], gt[t]))],
            out_specs=q_spec,
            scratch_shapes=[pltpu.VMEM((DSA_HEADS, tq, LANES), BF16),
                            pltpu.VMEM((DSA_HEADS, tq, LANES), F32),
                            pltpu.VMEM((DSA_HEADS, tq, LANES), F32),
                            pltpu.VMEM((DSA_HEADS // 2, tq, LANES), F32)]),
        out_shape=jax.ShapeDtypeStruct(sq.shape, BF16),
        compiler_params=pltpu.CompilerParams(
            dimension_semantics=("arbitrary", "arbitrary"),
            vmem_limit_bytes=VMEM_LIMIT),
        name="sparse_attn",
    )(i_tab, g_tab, sq, sk, sv, bias)


def _merge_kernel(x_ref, ya_ref, yb_ref, p_ref, wg_ref, wbd_ref, wbs_ref, wo_ref, g1_ref, b1_ref,
                  wr_ref, br_ref, wpg_ref, wp_ref, x1_ref, res_ref, comb_ref, *, alpha, d_model):
    x = x_ref[...]
    gates = _dot(x.astype(BF16), wg_ref[...])
    merged = (_sigmoid(gates[:, :d_model]) * _dot(ya_ref[...], wbd_ref[...])
              + _sigmoid(gates[:, d_model:]) * _dot(yb_ref[...], wbs_ref[...]))
    mix = _dot(merged.astype(BF16), wo_ref[...])
    x1 = _layer_norm(alpha * x + mix, g1_ref[...], b1_ref[...])
    x1b = x1.astype(BF16)
    x1_ref[...] = x1b
    ple = _sigmoid(_dot(x1b, wpg_ref[...])) * _dot(p_ref[...].astype(BF16), wp_ref[...])
    res_ref[...] = alpha * x1 + ple

    logits = _dot(x1b, wr_ref[...]) + br_ref[...]
    lane = lax.broadcasted_iota(jnp.int32, logits.shape, 1)
    is_group = jnp.logical_and(lane >= N_EXPERTS, lane < N_EXPERTS + N_GROUPS)
    gl = jnp.where(is_group, logits, NEG_INF)
    gmax = jnp.max(gl, axis=-1, keepdims=True)
    gsum = jnp.sum(jnp.where(is_group, jnp.exp(gl - gmax), 0.0), axis=-1, keepdims=True)
    g_val = 1.0 / gsum
    g_idx = jnp.min(jnp.where(jnp.logical_and(is_group, gl == gmax), lane, 4 * LANES),
                    axis=-1, keepdims=True) - N_EXPERTS
    first = g_idx * EXPERTS_PER_GROUP
    in_group = jnp.logical_and(lane >= first, lane < first + EXPERTS_PER_GROUP)
    el = jnp.where(in_group, logits, NEG_INF)
    e1 = jnp.max(el, axis=-1, keepdims=True)
    i1 = jnp.min(jnp.where(jnp.logical_and(in_group, el == e1), lane, 4 * LANES), axis=-1, keepdims=True)
    el2 = jnp.where(lane == i1, NEG_INF, el)
    e2 = jnp.max(el2, axis=-1, keepdims=True)
    i2 = jnp.min(jnp.where(jnp.logical_and(in_group, el2 == e2), lane, 4 * LANES), axis=-1, keepdims=True)
    p2 = jnp.exp(e2 - e1)
    w1 = 1.0 / (1.0 + p2)
    w2 = p2 / (1.0 + p2)
    comb_ref[...] = g_val * (jnp.where(lane == i1, w1, 0.0) + jnp.where(lane == i2, w2, 0.0))


def _merge(x2, ya, yb, p2, w_g, w_bd, w_bs, w_o, g1, b1, w_r, b_r, w_pg, w_p, tm, alpha):
    n, d = x2.shape
    tok = lambda width: pl.BlockSpec((tm, width), lambda i: (i, 0))
    full = lambda a: pl.BlockSpec(a.shape, lambda i: (0, 0))
    weights = (w_g, w_bd, w_bs, w_o, g1, b1, w_r, b_r, w_pg, w_p)
    return pl.pallas_call(
        functools.partial(_merge_kernel, alpha=alpha, d_model=d),
        grid=(n // tm,),
        in_specs=[tok(d), tok(BRANCH_WIDTH), tok(BRANCH_WIDTH), tok(p2.shape[1])] + [full(w) for w in weights],
        out_specs=[tok(d), tok(d), tok(LANES)],
        out_shape=[jax.ShapeDtypeStruct((n, d), BF16), jax.ShapeDtypeStruct((n, d), F32),
                   jax.ShapeDtypeStruct((n, LANES), F32)],
        compiler_params=pltpu.CompilerParams(dimension_semantics=("arbitrary",),
                                             vmem_limit_bytes=VMEM_LIMIT),
        name="merge",
    )(x2, ya, yb, p2, *weights)


def _moe_kernel(x1_ref, res_ref, comb_ref, wgu_ref, wd_ref, g2_ref, b2_ref, o_ref, h_ref, acc_ref, *,
                d_expert, eps):
    s = pl.program_id(1)
    x = x1_ref[...]
    comb = comb_ref[...]
    lane = lax.broadcasted_iota(jnp.int32, comb.shape, 1)
    for e in range(eps):
        gu = _dot(x, wgu_ref[e])
        ce = jnp.sum(jnp.where(lane == s * eps + e, comb, 0.0), axis=-1, keepdims=True)
        g = gu[:, :d_expert]
        h = g * _sigmoid(g) * gu[:, d_expert:] * ce
        h_ref[:, e * d_expert:(e + 1) * d_expert] = h.astype(BF16)
    y = _dot(h_ref[...], wd_ref[...])

    @pl.when(s == 0)
    def _():
        acc_ref[...] = y

    @pl.when(s > 0)
    def _():
        acc_ref[...] += y

    @pl.when(s == pl.num_programs(1) - 1)
    def _():
        o_ref[...] = _layer_norm(res_ref[...] + acc_ref[...], g2_ref[...], b2_ref[...])


def _moe(x1b, res, comb, w_gu, w_d, g2, b2, tm, eps):
    n, d = x1b.shape
    n_exp, _, two_f = w_gu.shape
    d_expert = two_f // 2
    assert n_exp % eps == 0 and w_d.shape == (n_exp * d_expert, d)
    tok = lambda width: pl.BlockSpec((tm, width), lambda i, s: (i, 0))
    return pl.pallas_call(
        functools.partial(_moe_kernel, d_expert=d_expert, eps=eps),
        grid=(n // tm, n_exp // eps),
        in_specs=[tok(d), tok(d), tok(LANES),
                  pl.BlockSpec((eps, d, two_f), lambda i, s: (s, 0, 0)),
                  pl.BlockSpec((eps * d_expert, d), lambda i, s: (s, 0)),
                  pl.BlockSpec((1, d), lambda i, s: (0, 0)),
                  pl.BlockSpec((1, d), lambda i, s: (0, 0))],
        out_specs=tok(d),
        out_shape=jax.ShapeDtypeStruct((n, d), F32),
        scratch_shapes=[pltpu.VMEM((tm, eps * d_expert), BF16), pltpu.VMEM((tm, d), F32)],
        compiler_params=pltpu.CompilerParams(dimension_semantics=("arbitrary", "arbitrary"),
                                             vmem_limit_bytes=VMEM_LIMIT),
        name="moe",
    )(x1b, res, comb, w_gu, w_d, g2, b2)


def _rope_tables(positions):
    inv_freq = 1.0 / (ROPE_THETA ** (jnp.arange(0, ROT_DIM, 2, dtype=F32) / ROT_DIM))
    ang = positions.astype(F32)[..., None] * inv_freq
    cos, sin = jnp.cos(ang), jnp.sin(ang)
    n = positions.size
    cos, sin = cos.reshape(n, ROT_HALF), sin.reshape(n, ROT_HALF)
    rest = HEAD_DIM - ROT_DIM
    cos_t = jnp.concatenate([cos, cos, jnp.ones((n, rest), F32)], axis=1)
    sin_a = jnp.concatenate([-sin, jnp.zeros((n, rest + ROT_HALF), F32)], axis=1)
    sin_b = jnp.concatenate([jnp.zeros((n, ROT_HALF), F32), sin, jnp.zeros((n, rest), F32)], axis=1)
    rep = LANES // HEAD_DIM
    return tuple(jnp.tile(t, (1, rep)) for t in (cos_t, sin_a, sin_b))


def _tile(n, want):
    t = min(n, want)
    assert n % t == 0, (n, t)
    return t


def kernel(x, p, positions, w_in, diff_lambda, diff_subln_g, w_branch_diff, w_branch_dsa, w_out, ln1_g, ln1_b, w_route_group, b_route_group, w_route_expert, b_route_expert, w_exp_gate, w_exp_up, w_exp_down, w_ple, w_ple_gate, ln2_g, ln2_b):
    batch, seq, d = x.shape
    depth = w_in.shape[0]
    n = batch * seq
    alpha = (2 * depth) ** 0.25
    topk = min(TOPK_MAX, seq // 4)
    d_expert = w_exp_gate.shape[-1]
    assert BRANCH_WIDTH == DIFF_HEADS * 2 * HEAD_DIM == DSA_HEADS * HEAD_DIM == IDX_HEADS * HEAD_DIM
    n_qkv = 7 * BRANCH_WIDTH
    o_ik, o_iw = n_qkv, n_qkv + HEAD_DIM
    o_ga = o_iw + IDX_HEADS
    assert w_in.shape[2] == o_ga + 2 * d

    tm_proj = _tile(n, 512)
    tq_diff = _tile(seq, 1024)
    tk_diff = _tile(seq, 1024)
    tq_select = _tile(seq, 512)
    tk_dsa = _tile(seq, 1024)
    tm_merge = _tile(n, 256)
    tm_moe = _tile(n, 1024)

    cos_t, sin_a, sin_b = _rope_tables(positions)
    scale = HEAD_DIM ** -0.5
    col_scale = jnp.ones((n_qkv,), F32)
    for seg in (0, 3, 6):
        col_scale = col_scale.at[seg * BRANCH_WIDTH:(seg + 1) * BRANCH_WIDTH].set(scale)

    x2 = x.reshape(n, d)
    for i in range(depth):
        lam_init = 0.8 - 0.6 * math.exp(-0.3 * i)
        wi = w_in[i]
        w_ik = wi[:, o_ik:o_ik + HEAD_DIM]
        w_iw = jnp.pad(wi[:, o_iw:o_iw + IDX_HEADS], ((0, 0), (0, LANES - IDX_HEADS)))
        w_a = jnp.concatenate([wi[:, :n_qkv] * col_scale, w_ik, w_ik, w_iw], axis=1).astype(BF16)
        w_g = wi[:, o_ga:].astype(BF16)
        w_r = jnp.pad(jnp.concatenate([w_route_expert[i], w_route_group[i]], axis=1),
                      ((0, 0), (0, LANES - N_EXPERTS - N_GROUPS))).astype(BF16)
        b_r = jnp.pad(jnp.concatenate([b_route_expert[i], b_route_group[i]]),
                      (0, LANES - N_EXPERTS - N_GROUPS)).reshape(1, LANES)
        w_gu = jnp.concatenate([w_exp_gate[i], w_exp_up[i]], axis=-1).reshape(
            N_EXPERTS, d, 2 * d_expert).astype(BF16)
        w_d = w_exp_down[i].reshape(N_EXPERTS * d_expert, d).astype(BF16)

        dq, dk, dv, sq, sk, sv, iq, ik2, iw = _inproj(x2, w_a, cos_t, sin_a, sin_b, tm_proj)
        ya = _diff_attn(dq, dk, dv, diff_lambda[i], diff_subln_g[i], batch, seq, tq_diff, tk_diff, lam_init)
        bias = _select(iq, ik2, iw, batch, seq, tq_select, tk_dsa, topk)
        yb = _sparse_attn(sq, sk, sv, bias, batch, seq, tk_dsa, tk_dsa)
        x1b, res, comb = _merge(
            x2, ya, yb, p[i].reshape(n, -1), w_g,
            w_branch_diff[i].astype(BF16), w_branch_dsa[i].astype(BF16), w_out[i].astype(BF16),
            ln1_g[i].reshape(1, d), ln1_b[i].reshape(1, d), w_r, b_r,
            w_ple_gate[i].astype(BF16), w_ple[i].astype(BF16), tm_merge, alpha)
        x2 = _moe(x1b, res, comb, w_gu, w_d, ln2_g[i].reshape(1, d), ln2_b[i].reshape(1, d), tm_moe,
                   MOE_EXPERTS_PER_STEP)
    return x2.reshape(batch, seq, d)
```

```python
import functools
import math

import jax
import jax.numpy as jnp
import numpy as np
from jax import lax
from jax.experimental import pallas as pl
from jax.experimental.pallas import tpu as pltpu

F32 = jnp.float32
BF16 = jnp.bfloat16

HEAD_DIM = 64
DIFF_HEADS = 4
DSA_HEADS = 8
IDX_HEADS = 8
TOPK_MAX = 256
N_GROUPS = 4
EXPERTS_PER_GROUP = 8
N_EXPERTS = N_GROUPS * EXPERTS_PER_GROUP
MOE_EXPERTS_PER_STEP = 4
ROPE_THETA = 500000.0
ROT_DIM = HEAD_DIM // 4
ROT_HALF = ROT_DIM // 2
LN_EPS = 1e-5
NEG_INF = -1e30
LOG2E = math.log2(math.e)
LANES = 128
BRANCH_WIDTH = 512
VMEM_LIMIT = 62 * 1024 * 1024


SIGN_BIT = np.int32(-2 ** 31)
MAGNITUDE_BITS = np.int32(2 ** 31 - 1)
SUBNORMAL_BITS = np.int32(2 ** 23 - 1)


def _float_key(v):
    bits = int(np.float32(v).view(np.int32))
    mag = max((bits & int(MAGNITUDE_BITS)) - int(SUBNORMAL_BITS), 0)
    return -mag if bits < 0 else mag


KEY_LO_INIT = _float_key(NEG_INF)
VALUE_STEPS = 24
TIGHTEN_START = 13
TIGHTEN_EVERY = 4
SCORE_CHUNK = 512


def _dot_nt(a, b):
    return lax.dot_general(a, b, (((1,), (1,)), ((), ())), preferred_element_type=F32)


def _dot(a, b):
    return jnp.dot(a, b, preferred_element_type=F32)


def _sigmoid(v):
    return 1.0 / (1.0 + jnp.exp(-v))


def _layer_norm(v, g, b):
    mu = jnp.mean(v, axis=-1, keepdims=True)
    d = v - mu
    var = jnp.mean(d * d, axis=-1, keepdims=True)
    return d * lax.rsqrt(var + LN_EPS) * g + b


def _inproj_kernel(x_ref, w_ref, c_ref, sa_ref, sb_ref,
                   dq_ref, dk_ref, dv_ref, sq_ref, sk_ref, sv_ref, iq_ref, ik_ref, iw_ref):
    xb = x_ref[...].astype(BF16)
    cos_t = c_ref[...]
    sin_a = sa_ref[...]
    sin_b = sb_ref[...]

    def rope(h):
        cols = []
        for c in range(h.shape[1] // LANES):
            hc = h[:, c * LANES:(c + 1) * LANES]
            cols.append(hc * cos_t
                        + pltpu.roll(hc, LANES - ROT_HALF, 1) * sin_a
                        + pltpu.roll(hc, ROT_HALF, 1) * sin_b)
        return cols[0] if len(cols) == 1 else jnp.concatenate(cols, axis=1)

    outs = ((dq_ref, True, LOG2E), (dk_ref, True, None), (dv_ref, False, None), (sq_ref, True, LOG2E),
            (sk_ref, True, None), (sv_ref, False, None), (iq_ref, True, None))
    for i, (ref, rotary, mult) in enumerate(outs):
        h = _dot(xb, w_ref[:, i * BRANCH_WIDTH:(i + 1) * BRANCH_WIDTH])
        if mult is not None:
            h = h * mult
        if rotary:
            h = rope(h)
        ref[...] = h.astype(BF16)
    base = len(outs) * BRANCH_WIDTH
    h = _dot(xb, w_ref[:, base:base + 2 * LANES])
    ik_ref[...] = rope(h[:, :LANES]).astype(BF16)
    iw_ref[...] = h[:, LANES:] * (IDX_HEADS ** -0.5)


def _inproj(x2, w_a, cos_t, sin_a, sin_b, tm):
    n, d = x2.shape
    wc = w_a.shape[1]
    tok = lambda width: pl.BlockSpec((tm, width), lambda i: (i, 0))
    out_shape = [jax.ShapeDtypeStruct((n, BRANCH_WIDTH), BF16)] * 7 + [
        jax.ShapeDtypeStruct((n, LANES), BF16), jax.ShapeDtypeStruct((n, LANES), F32)]
    return pl.pallas_call(
        _inproj_kernel,
        grid=(n // tm,),
        in_specs=[tok(d), pl.BlockSpec((d, wc), lambda i: (0, 0)), tok(LANES), tok(LANES), tok(LANES)],
        out_specs=[tok(BRANCH_WIDTH)] * 7 + [tok(LANES), tok(LANES)],
        out_shape=out_shape,
        compiler_params=pltpu.CompilerParams(dimension_semantics=("arbitrary",),
                                             vmem_limit_bytes=VMEM_LIMIT),
        name="inproj",
    )(x2, w_a, cos_t, sin_a, sin_b)


def _lane_cols(s):
    return [s[:, c * LANES:(c + 1) * LANES] for c in range(s.shape[1] // LANES)]


def _softmax_step(cols, vc, m_ref, l_ref, idx):
    mx = functools.reduce(jnp.maximum, cols)
    m_prev = m_ref[idx]
    m_new = jnp.maximum(m_prev, jnp.max(mx, axis=-1, keepdims=True))
    alpha = jnp.exp2(m_prev - m_new)
    ps = [jnp.exp2(c - m_new) for c in cols]
    l_ref[idx] = alpha * l_ref[idx] + functools.reduce(jnp.add, ps)
    m_ref[idx] = m_new
    p = ps[0] if len(ps) == 1 else jnp.concatenate(ps, axis=1)
    return alpha, _dot(p.astype(BF16), vc)


def _diff_kernel(q_ref, k_ref, v_ref, lam_ref, g_ref, o_ref, qm_ref, m_ref, l_ref, acc_ref, *,
                 tq, tk, lam_init):
    qi = pl.program_id(2)
    q = q_ref[...]
    lane = lax.broadcasted_iota(jnp.int32, q.shape, 1)
    zero = jnp.zeros_like(q)
    qm_ref[0] = jnp.where(lane < HEAD_DIM, q, zero)
    qm_ref[1] = jnp.where(lane >= HEAD_DIM, q, zero)
    m_ref[...] = jnp.full(m_ref.shape, NEG_INF, F32)
    l_ref[...] = jnp.zeros(l_ref.shape, F32)
    acc_ref[...] = jnp.zeros(acc_ref.shape, F32)

    def step(j, causal):
        start = pl.multiple_of(j * tk, tk)
        kc = k_ref[pl.ds(start, tk), :]
        vc = v_ref[pl.ds(start, tk), :]
        for c in range(2):
            s = _dot_nt(qm_ref[c], kc)
            if causal:
                row = qi * tq + lax.broadcasted_iota(jnp.int32, s.shape, 0)
                col = j * tk + lax.broadcasted_iota(jnp.int32, s.shape, 1)
                s = jnp.where(col <= row, s, NEG_INF)
            alpha, pv = _softmax_step(_lane_cols(s), vc, m_ref, l_ref, c)
            acc_ref[c] = alpha * acc_ref[c] + pv

    def body(j, carry):
        step(j, False)
        return carry

    n_full = (qi * tq) // tk
    lax.fori_loop(0, n_full, body, 0)
    step(n_full, True)

    lp = lam_ref[...]
    lam = (jnp.exp(jnp.sum(lp[0:1] * lp[1:2], axis=-1, keepdims=True))
           - jnp.exp(jnp.sum(lp[2:3] * lp[3:4], axis=-1, keepdims=True)) + lam_init)
    l0 = jnp.sum(l_ref[0], axis=-1, keepdims=True)
    l1 = jnp.sum(l_ref[1], axis=-1, keepdims=True)
    o = acc_ref[0] / l0 - lam * (acc_ref[1] / l1)
    o = o * lax.rsqrt(jnp.mean(o * o, axis=-1, keepdims=True) + LN_EPS)
    o_ref[...] = (o * g_ref[...] * (1.0 - lam_init)).astype(BF16)


def _diff_attn(dq, dk, dv, lam_params, subln_g, batch, seq, tq, tk, lam_init):
    assert tk % tq == 0 and seq % tk == 0
    nq = seq // tq
    hw = 2 * HEAD_DIM
    kv_spec = pl.BlockSpec((seq, hw), lambda b, h, i: (b, h))
    q_spec = pl.BlockSpec((tq, hw), lambda b, h, i: (b * nq + i, h))
    return pl.pallas_call(
        functools.partial(_diff_kernel, tq=tq, tk=tk, lam_init=lam_init),
        grid=(batch, DIFF_HEADS, nq),
        in_specs=[q_spec, kv_spec, kv_spec,
                  pl.BlockSpec(lam_params.shape, lambda b, h, i: (0, 0)),
                  pl.BlockSpec((1, hw), lambda b, h, i: (0, 0))],
        out_specs=q_spec,
        out_shape=jax.ShapeDtypeStruct(dq.shape, BF16),
        scratch_shapes=[pltpu.VMEM((2, tq, hw), BF16), pltpu.VMEM((2, tq, LANES), F32),
                        pltpu.VMEM((2, tq, LANES), F32), pltpu.VMEM((2, tq, hw), F32)],
        compiler_params=pltpu.CompilerParams(
            dimension_semantics=("arbitrary", "arbitrary", "arbitrary"),
            vmem_limit_bytes=VMEM_LIMIT),
        name="diff_attn",
    )(dq, dk, dv, lam_params, subln_g.reshape(1, hw))


def _key_to_float(key):
    mag = jnp.abs(key)
    bits = jnp.where(mag > 0, mag + SUBNORMAL_BITS, 0)
    return lax.bitcast_convert_type(jnp.where(key < 0, bits | SIGN_BIT, bits), F32)


def _float_to_key(v):
    bits = lax.bitcast_convert_type(v, jnp.int32)
    mag = jnp.maximum((bits & MAGNITUDE_BITS) - SUBNORMAL_BITS, 0)
    return jnp.where(bits < 0, -mag, mag)


def _causal_steps(seq, tq, tk):
    pairs = [(i, g) for i in range(seq // tq) for g in range(-(-(i + 1) * tq // tk))]
    return (jnp.asarray([p[0] for p in pairs], jnp.int32), jnp.asarray([p[1] for p in pairs], jnp.int32))


def _select_kernel(it_ref, gt_ref, iq_ref, iw_ref, ik_ref, tri_ref, o_ref,
                   sc_ref, iqm_ref, wb_ref, thr_ref, val_ref, need_ref, seen_ref, flag_ref, *,
                   tq, tk, topk):
    t = pl.program_id(1)
    qi = it_ref[t]
    g = gt_ref[t]
    ck = SCORE_CHUNK
    cps = tk // ck
    bands = [slice(b * LANES, (b + 1) * LANES) for b in range(tq // LANES)]
    tile = (LANES, LANES)

    def rows_of(vec):
        return jnp.broadcast_to(vec, tile).T

    def per_row(mat, reduce):
        return reduce(mat.T, axis=0, keepdims=True)

    @pl.when(g == 0)
    def _search():
        lane = lax.broadcasted_iota(jnp.int32, (tq, LANES), 1)
        low_half = lane < HEAD_DIM
        nch = (qi + 1) * tq // ck
        n_plain = qi * tq // ck

        iw = iw_ref[...]
        for h in range(IDX_HEADS):
            pair = h // 2
            keep = low_half if h % 2 == 0 else jnp.logical_not(low_half)
            iq_pair = iq_ref[:, pair * LANES:(pair + 1) * LANES]
            iqm_ref[h] = jnp.where(keep, iq_pair, jnp.zeros_like(iq_pair))
            wb_ref[h] = jnp.broadcast_to(iw[:, h:h + 1], (tq, LANES))

        def score_chunk(j, masked, top2):
            start = pl.multiple_of(j * ck, ck)
            kc = ik_ref[pl.ds(start, ck), :]
            parts = [jnp.zeros((tq, LANES), F32) for _ in range(ck // LANES)]
            for h in range(IDX_HEADS):
                d = _dot_nt(iqm_ref[h], kc)
                w = wb_ref[h]
                for c in range(ck // LANES):
                    parts[c] = parts[c] + w * jnp.maximum(d[:, c * LANES:(c + 1) * LANES], 0.0)
            sc = jnp.concatenate(parts, axis=1) + 0.0
            if masked:
                row = qi * tq + lax.broadcasted_iota(jnp.int32, (tq, ck), 0)
                col = start + lax.broadcasted_iota(jnp.int32, (tq, ck), 1)
                sc = jnp.where(col <= row, sc, NEG_INF)
            sc_ref[j] = sc
            max1, max2 = top2
            for part in _lane_cols(sc):
                max2 = jnp.maximum(max2, jnp.minimum(max1, part))
                max1 = jnp.maximum(max1, part)
            return max1, max2

        neg = jnp.full((tq, LANES), NEG_INF, F32)
        top2 = lax.fori_loop(0, n_plain, lambda j, s: score_chunk(j, False, s), (neg, neg))
        max1, max2 = lax.fori_loop(n_plain, nch, lambda j, s: score_chunk(j, True, s), top2)

        n_steps = (nch + cps - 1) // cps

        def fill(j, carry):
            sc_ref[j] = jnp.full((tq, ck), NEG_INF, F32)
            return carry
        lax.fori_loop(nch, n_steps * cps, fill, 0)

        lane_row = lax.broadcasted_iota(jnp.int32, (1, LANES), 1)

        def sweep(b, fn, init):
            def body(j, acc):
                for c, part in enumerate(_lane_cols(sc_ref[j, bands[b], :])):
                    acc = fn(acc, part, j, c)
                return acc
            return lax.fori_loop(0, nch, body, init)

        def count_above(thrs):
            thr_bs = [rows_of(v) for v in thrs]
            cnts = [sweep(b, lambda acc, part, j, c, v=v: acc + jnp.where(part > v, 1.0, 0.0),
                          jnp.zeros(tile, F32)) for b, v in enumerate(thr_bs)]
            return [per_row(c, jnp.sum) for c in cnts]

        def band_extent(lo_fs, hi_fs):
            edges = [(rows_of(lo_f), rows_of(hi_f)) for lo_f, hi_f in zip(lo_fs, hi_fs)]

            def fn(acc, part, j, c, lo_b, hi_b):
                top, bot = acc
                return (jnp.maximum(top, jnp.where(part <= hi_b, part, -jnp.inf)),
                        jnp.minimum(bot, jnp.where(part > lo_b, part, jnp.inf)))
            init = (jnp.full(tile, -jnp.inf, F32), jnp.full(tile, jnp.inf, F32))
            ext = [sweep(b, functools.partial(fn, lo_b=lo_b, hi_b=hi_b), init)
                   for b, (lo_b, hi_b) in enumerate(edges)]
            return [(per_row(top, jnp.max), per_row(bot, jnp.min)) for top, bot in ext]

        kks = [jnp.minimum(qi * tq + b * LANES + lane_row + 1, topk).astype(F32) for b in range(len(bands))]

        def unresolved(b, lo, hi, flo):
            return jnp.logical_and(flo != kks[b], hi - 1 > lo)

        def any_set(flags):
            return functools.reduce(jnp.maximum, [jnp.max(jnp.where(f, 1.0, 0.0)) for f in flags])

        def midpoint(it, lo, hi):
            mid_key = (lo >> 1) + (hi >> 1) + (lo & hi & 1)
            mid_val = _float_to_key(0.5 * _key_to_float(lo) + 0.5 * _key_to_float(hi))
            by_value = jnp.logical_and(jnp.logical_and(mid_val > lo, mid_val < hi), it < VALUE_STEPS)
            return jnp.where(by_value, mid_val, mid_key)

        def bisect(it, state):
            mids = [midpoint(it, lo, hi) for lo, hi, _, _ in state]
            cnts = count_above([_key_to_float(mid) for mid in mids])
            out = []
            for b, (lo, hi, flo, fhi) in enumerate(state):
                mid, cnt = mids[b], cnts[b]
                less = cnt < kks[b]
                out.append((jnp.where(less, lo, mid), jnp.where(less, mid, hi),
                            jnp.where(less, flo, cnt), jnp.where(less, cnt, fhi)))
            return tuple(out)

        def tighten(state):
            ext = band_extent([_key_to_float(s[0]) for s in state], [_key_to_float(s[1]) for s in state])
            out = []
            for b, (lo, hi, flo, fhi) in enumerate(state):
                top, bot = ext[b]
                todo = unresolved(b, lo, hi, flo)
                new_lo = jnp.maximum(_float_to_key(bot) - 1, lo)
                new_hi = jnp.minimum(_float_to_key(top), hi)
                out.append((jnp.where(todo, new_lo, lo), jnp.where(todo, new_hi, hi), flo, fhi))
            return tuple(out)

        def search_body(st):
            it, _, state = st
            do_tighten = jnp.logical_and(it >= TIGHTEN_START, (it - TIGHTEN_START) % TIGHTEN_EVERY == 0)
            state = lax.cond(do_tighten, tighten, functools.partial(bisect, it), state)
            todo = any_set([unresolved(b, s[0], s[1], s[2]) for b, s in enumerate(state)])
            return it + 1, todo, state

        hi0s = [_float_to_key(per_row(max1[rows], jnp.max)) for rows in bands]
        lo0s = [jnp.maximum(_float_to_key(per_row(max2[rows], jnp.min)) - 1, KEY_LO_INIT) for rows in bands]
        flo0s = count_above([_key_to_float(lo0) for lo0 in lo0s])
        state0 = tuple((lo0s[b], hi0s[b], flo0s[b], jnp.zeros((1, LANES), F32)) for b in range(len(bands)))
        todo0 = any_set([unresolved(b, s[0], s[1], s[2]) for b, s in enumerate(state0)])
        _, _, state = lax.while_loop(lambda st: st[1] > 0.0, search_body, (jnp.int32(0), todo0, state0))

        tie_rows = [s[2] > kks[b] for b, s in enumerate(state)]
        stack_rows = lambda vecs: jnp.concatenate([rows_of(v) for v in vecs], axis=0)
        thr_ref[...] = stack_rows([_key_to_float(jnp.where(tie_rows[b], s[1], s[0])) for b, s in enumerate(state)])
        val_ref[...] = stack_rows([_key_to_float(s[1]) for s in state])
        need_ref[...] = stack_rows([jnp.where(tie_rows[b], kks[b] - s[3], 0.0) for b, s in enumerate(state)])
        seen_ref[...] = jnp.zeros(seen_ref.shape, F32)
        flag_ref[0] = (any_set(tie_rows) > 0.0).astype(jnp.int32)

    thr_b = thr_ref[...]

    def write_tile(with_ties):
        def run(_):
            seen = seen_ref[...] if with_ties else None
            for u in range(cps):
                blk = sc_ref[g * cps + u]
                for c, part in enumerate(_lane_cols(blk)):
                    hit = part > thr_b
                    if with_ties:
                        tied = part == val_ref[...]
                        counts = _dot(jnp.where(tied, 1.0, 0.0).astype(BF16), tri_ref[...])
                        rank = seen + counts[:, :LANES]
                        hit = jnp.logical_or(hit, jnp.logical_and(tied, rank <= need_ref[...]))
                        seen = seen + counts[:, LANES:]
                    o_ref[:, u * ck + c * LANES:u * ck + (c + 1) * LANES] = jnp.where(hit, 0.0, NEG_INF)
            if with_ties:
                seen_ref[...] = seen
            return 0
        return run

    lax.cond(flag_ref[0] > 0, write_tile(True), write_tile(False), 0)


def _select(iq, ik2, iw, batch, seq, tq, tk, topk):
    ck = SCORE_CHUNK
    assert tq % LANES == 0 and tq % ck == 0 and tk % ck == 0 and seq % tk == 0 and topk <= 2 * LANES
    nq = seq // tq
    i_tab, g_tab = _causal_steps(seq, tq, tk)
    k_idx = jnp.arange(LANES)
    tri = jnp.concatenate([(k_idx[:, None] <= k_idx[None, :]).astype(BF16),
                           jnp.ones((LANES, LANES), BF16)], axis=1)
    q_spec = lambda width: pl.BlockSpec((tq, width), lambda b, t, it, gt: (b * nq + it[t], 0))
    stat = pltpu.VMEM((tq, LANES), F32)
    return pl.pallas_call(
        functools.partial(_select_kernel, tq=tq, tk=tk, topk=topk),
        grid_spec=pltpu.PrefetchScalarGridSpec(
            num_scalar_prefetch=2,
            grid=(batch, i_tab.shape[0]),
            in_specs=[q_spec(BRANCH_WIDTH), q_spec(LANES),
                      pl.BlockSpec((seq, LANES), lambda b, t, it, gt: (b, 0), pipeline_mode=pl.Buffered(1)),
                      pl.BlockSpec(tri.shape, lambda b, t, it, gt: (0, 0))],
            out_specs=pl.BlockSpec((tq, tk), lambda b, t, it, gt: (b * nq + it[t], gt[t])),
            scratch_shapes=[pltpu.VMEM((seq // ck, tq, ck), F32),
                            pltpu.VMEM((IDX_HEADS, tq, LANES), BF16),
                            pltpu.VMEM((IDX_HEADS, tq, LANES), F32),
                            stat, stat, stat, stat,
                            pltpu.SMEM((1,), jnp.int32)]),
        out_shape=jax.ShapeDtypeStruct((batch * seq, seq), F32),
        compiler_params=pltpu.CompilerParams(
            dimension_semantics=("arbitrary", "arbitrary"),
            vmem_limit_bytes=VMEM_LIMIT),
        name="select",
    )(i_tab, g_tab, iq, iw, ik2, tri)


def _sparse_attn_kernel(it_ref, gt_ref, q_ref, k_ref, v_ref, bias_ref, o_ref,
                        qm_ref, m_ref, l_ref, acc_ref, *, tq, tk):
    t = pl.program_id(1)
    qi = it_ref[t]
    g = gt_ref[t]
    lane = lax.broadcasted_iota(jnp.int32, (tq, LANES), 1)
    low_half = lane < HEAD_DIM

    @pl.when(g == 0)
    def _():
        for h in range(DSA_HEADS):
            pair = h // 2
            keep = low_half if h % 2 == 0 else jnp.logical_not(low_half)
            q_pair = q_ref[:, pair * LANES:(pair + 1) * LANES]
            qm_ref[h] = jnp.where(keep, q_pair, jnp.zeros_like(q_pair))
        m_ref[...] = jnp.full(m_ref.shape, NEG_INF, F32)
        l_ref[...] = jnp.zeros(l_ref.shape, F32)
        acc_ref[...] = jnp.zeros(acc_ref.shape, F32)

    for pair in range(DSA_HEADS // 2):
        kc = k_ref[:, pair * LANES:(pair + 1) * LANES]
        vc = v_ref[:, pair * LANES:(pair + 1) * LANES]
        alphas, pvs = [], []
        for h in (2 * pair, 2 * pair + 1):
            s = _dot_nt(qm_ref[h], kc)
            cols = [a + b for a, b in zip(_lane_cols(s), _lane_cols(bias_ref[...]))]
            alpha, pv = _softmax_step(cols, vc, m_ref, l_ref, h)
            alphas.append(alpha)
            pvs.append(pv)
        alpha_pair = jnp.where(low_half, alphas[0], alphas[1])
        pv_pair = jnp.where(low_half, pvs[0], pvs[1])
        acc_ref[pair] = alpha_pair * acc_ref[pair] + pv_pair

    @pl.when((g + 1) * tk >= (qi + 1) * tq)
    def _():
        for pair in range(DSA_HEADS // 2):
            l_pair = jnp.where(low_half, jnp.sum(l_ref[2 * pair], axis=-1, keepdims=True),
                               jnp.sum(l_ref[2 * pair + 1], axis=-1, keepdims=True))
            o_ref[:, pair * LANES:(pair + 1) * LANES] = (acc_ref[pair] / l_pair).astype(BF16)


def _sparse_attn(sq, sk, sv, bias, batch, seq, tq, tk):
    assert tq == tk and seq % tk == 0
    nq, nk = seq // tq, seq // tk
    i_tab, g_tab = _causal_steps(seq, tq, tk)
    q_spec = pl.BlockSpec((tq, BRANCH_WIDTH), lambda b, t, it, gt: (b * nq + it[t], 0))
    kv_spec = pl.BlockSpec((tk, BRANCH_WIDTH), lambda b, t, it, gt: (b * nk + gt[t], 0))
    return pl.pallas_call(
        functools.partial(_sparse_attn_kernel, tq=tq, tk=tk),
        grid_spec=pltpu.PrefetchScalarGridSpec(
            num_scalar_prefetch=2,
            grid=(batch, i_tab.shape[0]),
            in_specs=[q_spec, kv_spec, kv_spec,
                      pl.BlockSpec((tq, tk), lambda b, t, it, gt: (b * nq + it[t], gt[t]))],
            out_specs=q_spec,
            scratch_shapes=[pltpu.VMEM((DSA_HEADS, tq, LANES), BF16),
                            pltpu.VMEM((DSA_HEADS, tq, LANES), F32),
                            pltpu.VMEM((DSA_HEADS, tq, LANES), F32),
                            pltpu.VMEM((DSA_HEADS // 2, tq, LANES), F32)]),
        out_shape=jax.ShapeDtypeStruct(sq.shape, BF16),
        compiler_params=pltpu.CompilerParams(
            dimension_semantics=("arbitrary", "arbitrary"),
            vmem_limit_bytes=VMEM_LIMIT),
        name="sparse_attn",
    )(i_tab, g_tab, sq, sk, sv, bias)


def _merge_kernel(x_ref, ya_ref, yb_ref, p_ref, wg_ref, wbd_ref, wbs_ref, wo_ref, g1_ref, b1_ref,
                  wr_ref, br_ref, wpg_ref, wp_ref, x1_ref, res_ref, comb_ref, *, alpha, d_model):
    x = x_ref[...]
    gates = _dot(x.astype(BF16), wg_ref[...])
    merged = (_sigmoid(gates[:, :d_model]) * _dot(ya_ref[...], wbd_ref[...])
              + _sigmoid(gates[:, d_model:]) * _dot(yb_ref[...], wbs_ref[...]))
    mix = _dot(merged.astype(BF16), wo_ref[...])
    x1 = _layer_norm(alpha * x + mix, g1_ref[...], b1_ref[...])
    x1b = x1.astype(BF16)
    x1_ref[...] = x1b
    ple = _sigmoid(_dot(x1b, wpg_ref[...])) * _dot(p_ref[...].astype(BF16), wp_ref[...])
    res_ref[...] = alpha * x1 + ple

    logits = _dot(x1b, wr_ref[...]) + br_ref[...]
    lane = lax.broadcasted_iota(jnp.int32, logits.shape, 1)
    is_group = jnp.logical_and(lane >= N_EXPERTS, lane < N_EXPERTS + N_GROUPS)
    gl = jnp.where(is_group, logits, NEG_INF)
    gmax = jnp.max(gl, axis=-1, keepdims=True)
    gsum = jnp.sum(jnp.where(is_group, jnp.exp(gl - gmax), 0.0), axis=-1, keepdims=True)
    g_val = 1.0 / gsum
    g_idx = jnp.min(jnp.where(jnp.logical_and(is_group, gl == gmax), lane, 4 * LANES),
                    axis=-1, keepdims=True) - N_EXPERTS
    first = g_idx * EXPERTS_PER_GROUP
    in_group = jnp.logical_and(lane >= first, lane < first + EXPERTS_PER_GROUP)
    el = jnp.where(in_group, logits, NEG_INF)
    e1 = jnp.max(el, axis=-1, keepdims=True)
    i1 = jnp.min(jnp.where(jnp.logical_and(in_group, el == e1), lane, 4 * LANES), axis=-1, keepdims=True)
    el2 = jnp.where(lane == i1, NEG_INF, el)
    e2 = jnp.max(el2, axis=-1, keepdims=True)
    i2 = jnp.min(jnp.where(jnp.logical_and(in_group, el2 == e2), lane, 4 * LANES), axis=-1, keepdims=True)
    p2 = jnp.exp(e2 - e1)
    w1 = 1.0 / (1.0 + p2)
    w2 = p2 / (1.0 + p2)
    comb_ref[...] = g_val * (jnp.where(lane == i1, w1, 0.0) + jnp.where(lane == i2, w2, 0.0))


def _merge(x2, ya, yb, p2, w_g, w_bd, w_bs, w_o, g1, b1, w_r, b_r, w_pg, w_p, tm, alpha):
    n, d = x2.shape
    tok = lambda width: pl.BlockSpec((tm, width), lambda i: (i, 0))
    full = lambda a: pl.BlockSpec(a.shape, lambda i: (0, 0))
    weights = (w_g, w_bd, w_bs, w_o, g1, b1, w_r, b_r, w_pg, w_p)
    return pl.pallas_call(
        functools.partial(_merge_kernel, alpha=alpha, d_model=d),
        grid=(n // tm,),
        in_specs=[tok(d), tok(BRANCH_WIDTH), tok(BRANCH_WIDTH), tok(p2.shape[1])] + [full(w) for w in weights],
        out_specs=[tok(d), tok(d), tok(LANES)],
        out_shape=[jax.ShapeDtypeStruct((n, d), BF16), jax.ShapeDtypeStruct((n, d), F32),
                   jax.ShapeDtypeStruct((n, LANES), F32)],
        compiler_params=pltpu.CompilerParams(dimension_semantics=("arbitrary",),
                                             vmem_limit_bytes=VMEM_LIMIT),
        name="merge",
    )(x2, ya, yb, p2, *weights)


def _moe_kernel(x1_ref, res_ref, comb_ref, wgu_ref, wd_ref, g2_ref, b2_ref, o_ref, h_ref, acc_ref, *,
                d_expert, eps):
    s = pl.program_id(1)
    x = x1_ref[...]
    comb = comb_ref[...]
    lane = lax.broadcasted_iota(jnp.int32, comb.shape, 1)
    for e in range(eps):
        gu = _dot(x, wgu_ref[e])
        ce = jnp.sum(jnp.where(lane == s * eps + e, comb, 0.0), axis=-1, keepdims=True)
        g = gu[:, :d_expert]
        h = g * _sigmoid(g) * gu[:, d_expert:] * ce
        h_ref[:, e * d_expert:(e + 1) * d_expert] = h.astype(BF16)
    y = _dot(h_ref[...], wd_ref[...])

    @pl.when(s == 0)
    def _():
        acc_ref[...] = y

    @pl.when(s > 0)
    def _():
        acc_ref[...] += y

    @pl.when(s == pl.num_programs(1) - 1)
    def _():
        o_ref[...] = _layer_norm(res_ref[...] + acc_ref[...], g2_ref[...], b2_ref[...])


def _moe(x1b, res, comb, w_gu, w_d, g2, b2, tm, eps):
    n, d = x1b.shape
    n_exp, _, two_f = w_gu.shape
    d_expert = two_f // 2
    assert n_exp % eps == 0 and w_d.shape == (n_exp * d_expert, d)
    tok = lambda width: pl.BlockSpec((tm, width), lambda i, s: (i, 0))
    return pl.pallas_call(
        functools.partial(_moe_kernel, d_expert=d_expert, eps=eps),
        grid=(n // tm, n_exp // eps),
        in_specs=[tok(d), tok(d), tok(LANES),
                  pl.BlockSpec((eps, d, two_f), lambda i, s: (s, 0, 0)),
                  pl.BlockSpec((eps * d_expert, d), lambda i, s: (s, 0)),
                  pl.BlockSpec((1, d), lambda i, s: (0, 0)),
                  pl.BlockSpec((1, d), lambda i, s: (0, 0))],
        out_specs=tok(d),
        out_shape=jax.ShapeDtypeStruct((n, d), F32),
        scratch_shapes=[pltpu.VMEM((tm, eps * d_expert), BF16), pltpu.VMEM((tm, d), F32)],
        compiler_params=pltpu.CompilerParams(dimension_semantics=("arbitrary", "arbitrary"),
                                             vmem_limit_bytes=VMEM_LIMIT),
        name="moe",
    )(x1b, res, comb, w_gu, w_d, g2, b2)


def _rope_tables(positions):
    inv_freq = 1.0 / (ROPE_THETA ** (jnp.arange(0, ROT_DIM, 2, dtype=F32) / ROT_DIM))
    ang = positions.astype(F32)[..., None] * inv_freq
    cos, sin = jnp.cos(ang), jnp.sin(ang)
    n = positions.size
    cos, sin = cos.reshape(n, ROT_HALF), sin.reshape(n, ROT_HALF)
    rest = HEAD_DIM - ROT_DIM
    cos_t = jnp.concatenate([cos, cos, jnp.ones((n, rest), F32)], axis=1)
    sin_a = jnp.concatenate([-sin, jnp.zeros((n, rest + ROT_HALF), F32)], axis=1)
    sin_b = jnp.concatenate([jnp.zeros((n, ROT_HALF), F32), sin, jnp.zeros((n, rest), F32)], axis=1)
    rep = LANES // HEAD_DIM
    return tuple(jnp.tile(t, (1, rep)) for t in (cos_t, sin_a, sin_b))


def _tile(n, want):
    t = min(n, want)
    assert n % t == 0, (n, t)
    return t


def kernel(x, p, positions, w_in, diff_lambda, diff_subln_g, w_branch_diff, w_branch_dsa, w_out, ln1_g, ln1_b, w_route_group, b_route_group, w_route_expert, b_route_expert, w_exp_gate, w_exp_up, w_exp_down, w_ple, w_ple_gate, ln2_g, ln2_b):
    batch, seq, d = x.shape
    depth = w_in.shape[0]
    n = batch * seq
    alpha = (2 * depth) ** 0.25
    topk = min(TOPK_MAX, seq // 4)
    d_expert = w_exp_gate.shape[-1]
    assert BRANCH_WIDTH == DIFF_HEADS * 2 * HEAD_DIM == DSA_HEADS * HEAD_DIM == IDX_HEADS * HEAD_DIM
    n_qkv = 7 * BRANCH_WIDTH
    o_ik, o_iw = n_qkv, n_qkv + HEAD_DIM
    o_ga = o_iw + IDX_HEADS
    assert w_in.shape[2] == o_ga + 2 * d

    tm_proj = _tile(n, 512)
    tq_diff = _tile(seq, 1024)
    tk_diff = _tile(seq, 1024)
    tq_select = _tile(seq, 512)
    tk_dsa = _tile(seq, 1024)
    tm_merge = _tile(n, 256)
    tm_moe = _tile(n, 1024)

    cos_t, sin_a, sin_b = _rope_tables(positions)
    scale = HEAD_DIM ** -0.5
    col_scale = jnp.ones((n_qkv,), F32)
    for seg in (0, 3, 6):
        col_scale = col_scale.at[seg * BRANCH_WIDTH:(seg + 1) * BRANCH_WIDTH].set(scale)

    x2 = x.reshape(n, d)
    for i in range(depth):
        lam_init = 0.8 - 0.6 * math.exp(-0.3 * i)
        wi = w_in[i]
        w_ik = wi[:, o_ik:o_ik + HEAD_DIM]
        w_iw = jnp.pad(wi[:, o_iw:o_iw + IDX_HEADS], ((0, 0), (0, LANES - IDX_HEADS)))
        w_a = jnp.concatenate([wi[:, :n_qkv] * col_scale, w_ik, w_ik, w_iw], axis=1).astype(BF16)
        w_g = wi[:, o_ga:].astype(BF16)
        w_r = jnp.pad(jnp.concatenate([w_route_expert[i], w_route_group[i]], axis=1),
                      ((0, 0), (0, LANES - N_EXPERTS - N_GROUPS))).astype(BF16)
        b_r = jnp.pad(jnp.concatenate([b_route_expert[i], b_route_group[i]]),
                      (0, LANES - N_EXPERTS - N_GROUPS)).reshape(1, LANES)
        w_gu = jnp.concatenate([w_exp_gate[i], w_exp_up[i]], axis=-1).reshape(
            N_EXPERTS, d, 2 * d_expert).astype(BF16)
        w_d = w_exp_down[i].reshape(N_EXPERTS * d_expert, d).astype(BF16)

        dq, dk, dv, sq, sk, sv, iq, ik2, iw = _inproj(x2, w_a, cos_t, sin_a, sin_b, tm_proj)
        ya = _diff_attn(dq, dk, dv, diff_lambda[i], diff_subln_g[i], batch, seq, tq_diff, tk_diff, lam_init)
        bias = _select(iq, ik2, iw, batch, seq, tq_select, tk_dsa, topk)
        yb = _sparse_attn(sq, sk, sv, bias, batch, seq, tk_dsa, tk_dsa)
        x1b, res, comb = _merge(
            x2, ya, yb, p[i].reshape(n, -1), w_g,
            w_branch_diff[i].astype(BF16), w_branch_dsa[i].astype(BF16), w_out[i].astype(BF16),
            ln1_g[i].reshape(1, d), ln1_b[i].reshape(1, d), w_r, b_r,
            w_ple_gate[i].astype(BF16), w_ple[i].astype(BF16), tm_merge, alpha)
        x2 = _moe(x1b, res, comb, w_gu, w_d, ln2_g[i].reshape(1, d), ln2_b[i].reshape(1, d), tm_moe,
                   MOE_EXPERTS_PER_STEP)
    return x2.reshape(batch, seq, d)
```

```python
import functools
import math

import jax
import jax.numpy as jnp
import numpy as np
from jax import lax
from jax.experimental import pallas as pl
from jax.experimental.pallas import tpu as pltpu

F32 = jnp.float32
BF16 = jnp.bfloat16

HEAD_DIM = 64
DIFF_HEADS = 4
DSA_HEADS = 8
IDX_HEADS = 8
TOPK_MAX = 256
N_GROUPS = 4
EXPERTS_PER_GROUP = 8
N_EXPERTS = N_GROUPS * EXPERTS_PER_GROUP
MOE_EXPERTS_PER_STEP = 4
ROPE_THETA = 500000.0
ROT_DIM = HEAD_DIM // 4
ROT_HALF = ROT_DIM // 2
LN_EPS = 1e-5
NEG_INF = -1e30
LOG2E = math.log2(math.e)
LANES = 128
BRANCH_WIDTH = 512
VMEM_LIMIT = 62 * 1024 * 1024


SIGN_BIT = np.int32(-2 ** 31)
MAGNITUDE_BITS = np.int32(2 ** 31 - 1)
SUBNORMAL_BITS = np.int32(2 ** 23 - 1)


def _float_key(v):
    bits = int(np.float32(v).view(np.int32))
    mag = max((bits & int(MAGNITUDE_BITS)) - int(SUBNORMAL_BITS), 0)
    return -mag if bits < 0 else mag


KEY_LO_INIT = _float_key(NEG_INF)
VALUE_STEPS = 24
TIGHTEN_START = 13
TIGHTEN_EVERY = 4
SCORE_CHUNK = 512


def _dot_nt(a, b):
    return lax.dot_general(a, b, (((1,), (1,)), ((), ())), preferred_element_type=F32)


def _dot(a, b):
    return jnp.dot(a, b, preferred_element_type=F32)


def _sigmoid(v):
    return 1.0 / (1.0 + jnp.exp(-v))


def _layer_norm(v, g, b):
    mu = jnp.mean(v, axis=-1, keepdims=True)
    d = v - mu
    var = jnp.mean(d * d, axis=-1, keepdims=True)
    return d * lax.rsqrt(var + LN_EPS) * g + b


def _inproj_kernel(x_ref, w_ref, c_ref, sa_ref, sb_ref,
                   dq_ref, dk_ref, dv_ref, sq_ref, sk_ref, sv_ref, iq_ref, ik_ref, iw_ref):
    xb = x_ref[...].astype(BF16)
    cos_t = c_ref[...]
    sin_a = sa_ref[...]
    sin_b = sb_ref[...]

    def rope(h):
        cols = []
        for c in range(h.shape[1] // LANES):
            hc = h[:, c * LANES:(c + 1) * LANES]
            cols.append(hc * cos_t
                        + pltpu.roll(hc, LANES - ROT_HALF, 1) * sin_a
                        + pltpu.roll(hc, ROT_HALF, 1) * sin_b)
        return cols[0] if len(cols) == 1 else jnp.concatenate(cols, axis=1)

    outs = ((dq_ref, True, LOG2E), (dk_ref, True, None), (dv_ref, False, None), (sq_ref, True, LOG2E),
            (sk_ref, True, None), (sv_ref, False, None), (iq_ref, True, None))
    for i, (ref, rotary, mult) in enumerate(outs):
        h = _dot(xb, w_ref[:, i * BRANCH_WIDTH:(i + 1) * BRANCH_WIDTH])
        if mult is not None:
            h = h * mult
        if rotary:
            h = rope(h)
        ref[...] = h.astype(BF16)
    base = len(outs) * BRANCH_WIDTH
    h = _dot(xb, w_ref[:, base:base + 2 * LANES])
    ik_ref[...] = rope(h[:, :LANES]).astype(BF16)
    iw_ref[...] = h[:, LANES:] * (IDX_HEADS ** -0.5)


def _inproj(x2, w_a, cos_t, sin_a, sin_b, tm):
    n, d = x2.shape
    wc = w_a.shape[1]
    tok = lambda width: pl.BlockSpec((tm, width), lambda i: (i, 0))
    out_shape = [jax.ShapeDtypeStruct((n, BRANCH_WIDTH), BF16)] * 7 + [
        jax.ShapeDtypeStruct((n, LANES), BF16), jax.ShapeDtypeStruct((n, LANES), F32)]
    return pl.pallas_call(
        _inproj_kernel,
        grid=(n // tm,),
        in_specs=[tok(d), pl.BlockSpec((d, wc), lambda i: (0, 0)), tok(LANES), tok(LANES), tok(LANES)],
        out_specs=[tok(BRANCH_WIDTH)] * 7 + [tok(LANES), tok(LANES)],
        out_shape=out_shape,
        compiler_params=pltpu.CompilerParams(dimension_semantics=("arbitrary",),
                                             vmem_limit_bytes=VMEM_LIMIT),
        name="inproj",
    )(x2, w_a, cos_t, sin_a, sin_b)


def _lane_cols(s):
    return [s[:, c * LANES:(c + 1) * LANES] for c in range(s.shape[1] // LANES)]


def _softmax_step(cols, vc, m_ref, l_ref, idx):
    mx = functools.reduce(jnp.maximum, cols)
    m_prev = m_ref[idx]
    m_new = jnp.maximum(m_prev, jnp.max(mx, axis=-1, keepdims=True))
    alpha = jnp.exp2(m_prev - m_new)
    ps = [jnp.exp2(c - m_new) for c in cols]
    l_ref[idx] = alpha * l_ref[idx] + functools.reduce(jnp.add, ps)
    m_ref[idx] = m_new
    p = ps[0] if len(ps) == 1 else jnp.concatenate(ps, axis=1)
    return alpha, _dot(p.astype(BF16), vc)


def _diff_kernel(q_ref, k_ref, v_ref, lam_ref, g_ref, o_ref, qm_ref, m_ref, l_ref, acc_ref, *,
                 tq, tk, lam_init):
    qi = pl.program_id(2)
    q = q_ref[...]
    lane = lax.broadcasted_iota(jnp.int32, q.shape, 1)
    zero = jnp.zeros_like(q)
    qm_ref[0] = jnp.where(lane < HEAD_DIM, q, zero)
    qm_ref[1] = jnp.where(lane >= HEAD_DIM, q, zero)
    m_ref[...] = jnp.full(m_ref.shape, NEG_INF, F32)
    l_ref[...] = jnp.zeros(l_ref.shape, F32)
    acc_ref[...] = jnp.zeros(acc_ref.shape, F32)

    def step(j, causal):
        start = pl.multiple_of(j * tk, tk)
        kc = k_ref[pl.ds(start, tk), :]
        vc = v_ref[pl.ds(start, tk), :]
        for c in range(2):
            s = _dot_nt(qm_ref[c], kc)
            if causal:
                row = qi * tq + lax.broadcasted_iota(jnp.int32, s.shape, 0)
                col = j * tk + lax.broadcasted_iota(jnp.int32, s.shape, 1)
                s = jnp.where(col <= row, s, NEG_INF)
            alpha, pv = _softmax_step(_lane_cols(s), vc, m_ref, l_ref, c)
            acc_ref[c] = alpha * acc_ref[c] + pv

    def body(j, carry):
        step(j, False)
        return carry

    n_full = (qi * tq) // tk
    lax.fori_loop(0, n_full, body, 0)
    step(n_full, True)

    lp = lam_ref[...]
    lam = (jnp.exp(jnp.sum(lp[0:1] * lp[1:2], axis=-1, keepdims=True))
           - jnp.exp(jnp.sum(lp[2:3] * lp[3:4], axis=-1, keepdims=True)) + lam_init)
    l0 = jnp.sum(l_ref[0], axis=-1, keepdims=True)
    l1 = jnp.sum(l_ref[1], axis=-1, keepdims=True)
    o = acc_ref[0] / l0 - lam * (acc_ref[1] / l1)
    o = o * lax.rsqrt(jnp.mean(o * o, axis=-1, keepdims=True) + LN_EPS)
    o_ref[...] = (o * g_ref[...] * (1.0 - lam_init)).astype(BF16)


def _diff_attn(dq, dk, dv, lam_params, subln_g, batch, seq, tq, tk, lam_init):
    assert tk % tq == 0 and seq % tk == 0
    nq = seq // tq
    hw = 2 * HEAD_DIM
    kv_spec = pl.BlockSpec((seq, hw), lambda b, h, i: (b, h))
    q_spec = pl.BlockSpec((tq, hw), lambda b, h, i: (b * nq + i, h))
    return pl.pallas_call(
        functools.partial(_diff_kernel, tq=tq, tk=tk, lam_init=lam_init),
        grid=(batch, DIFF_HEADS, nq),
        in_specs=[q_spec, kv_spec, kv_spec,
                  pl.BlockSpec(lam_params.shape, lambda b, h, i: (0, 0)),
                  pl.BlockSpec((1, hw), lambda b, h, i: (0, 0))],
        out_specs=q_spec,
        out_shape=jax.ShapeDtypeStruct(dq.shape, BF16),
        scratch_shapes=[pltpu.VMEM((2, tq, hw), BF16), pltpu.VMEM((2, tq, LANES), F32),
                        pltpu.VMEM((2, tq, LANES), F32), pltpu.VMEM((2, tq, hw), F32)],
        compiler_params=pltpu.CompilerParams(
            dimension_semantics=("arbitrary", "arbitrary", "arbitrary"),
            vmem_limit_bytes=VMEM_LIMIT),
        name="diff_attn",
    )(dq, dk, dv, lam_params, subln_g.reshape(1, hw))


def _key_to_float(key):
    mag = jnp.abs(key)
    bits = jnp.where(mag > 0, mag + SUBNORMAL_BITS, 0)
    return lax.bitcast_convert_type(jnp.where(key < 0, bits | SIGN_BIT, bits), F32)


def _float_to_key(v):
    bits = lax.bitcast_convert_type(v, jnp.int32)
    mag = jnp.maximum((bits & MAGNITUDE_BITS) - SUBNORMAL_BITS, 0)
    return jnp.where(bits < 0, -mag, mag)


def _causal_steps(seq, tq, tk):
    pairs = [(i, g) for i in range(seq // tq) for g in range(-(-(i + 1) * tq // tk))]
    return (jnp.asarray([p[0] for p in pairs], jnp.int32), jnp.asarray([p[1] for p in pairs], jnp.int32))


def _select_kernel(it_ref, gt_ref, iq_ref, iw_ref, ik_ref, tri_ref, o_ref,
                   sc_ref, iqm_ref, wb_ref, thr_ref, val_ref, need_ref, seen_ref, flag_ref, *,
                   tq, tk, topk):
    t = pl.program_id(1)
    qi = it_ref[t]
    g = gt_ref[t]
    ck = SCORE_CHUNK
    cps = tk // ck
    bands = [slice(b * LANES, (b + 1) * LANES) for b in range(tq // LANES)]
    tile = (LANES, LANES)

    def rows_of(vec):
        return jnp.broadcast_to(vec, tile).T

    def per_row(mat, reduce):
        return reduce(mat.T, axis=0, keepdims=True)

    @pl.when(g == 0)
    def _search():
        lane = lax.broadcasted_iota(jnp.int32, (tq, LANES), 1)
        low_half = lane < HEAD_DIM
        nch = (qi + 1) * tq // ck
        n_plain = qi * tq // ck

        iw = iw_ref[...]
        for h in range(IDX_HEADS):
            pair = h // 2
            keep = low_half if h % 2 == 0 else jnp.logical_not(low_half)
            iq_pair = iq_ref[:, pair * LANES:(pair + 1) * LANES]
            iqm_ref[h] = jnp.where(keep, iq_pair, jnp.zeros_like(iq_pair))
            wb_ref[h] = jnp.broadcast_to(iw[:, h:h + 1], (tq, LANES))

        def score_chunk(j, masked, top2):
            start = pl.multiple_of(j * ck, ck)
            kc = ik_ref[pl.ds(start, ck), :]
            parts = [jnp.zeros((tq, LANES), F32) for _ in range(ck // LANES)]
            for h in range(IDX_HEADS):
                d = _dot_nt(iqm_ref[h], kc)
                w = wb_ref[h]
                for c in range(ck // LANES):
                    parts[c] = parts[c] + w * jnp.maximum(d[:, c * LANES:(c + 1) * LANES], 0.0)
            sc = jnp.concatenate(parts, axis=1) + 0.0
            if masked:
                row = qi * tq + lax.broadcasted_iota(jnp.int32, (tq, ck), 0)
                col = start + lax.broadcasted_iota(jnp.int32, (tq, ck), 1)
                sc = jnp.where(col <= row, sc, NEG_INF)
            sc_ref[j] = sc
            max1, max2 = top2
            for part in _lane_cols(sc):
                max2 = jnp.maximum(max2, jnp.minimum(max1, part))
                max1 = jnp.maximum(max1, part)
            return max1, max2

        neg = jnp.full((tq, LANES), NEG_INF, F32)
        top2 = lax.fori_loop(0, n_plain, lambda j, s: score_chunk(j, False, s), (neg, neg))
        max1, max2 = lax.fori_loop(n_plain, nch, lambda j, s: score_chunk(j, True, s), top2)

        n_steps = (nch + cps - 1) // cps

        def fill(j, carry):
            sc_ref[j] = jnp.full((tq, ck), NEG_INF, F32)
            return carry
        lax.fori_loop(nch, n_steps * cps, fill, 0)

        lane_row = lax.broadcasted_iota(jnp.int32, (1, LANES), 1)

        def sweep(b, fn, init):
            def body(j, acc):
                for c, part in enumerate(_lane_cols(sc_ref[j, bands[b], :])):
                    acc = fn(acc, part, j, c)
                return acc
            return lax.fori_loop(0, nch, body, init)

        def sweep_if(active, b, fn, init):
            if active is None:
                return sweep(b, fn, init)
            return lax.cond(active[b] > 0.0, lambda: sweep(b, fn, init), lambda: init)

        def count_above(thrs, active=None):
            thr_bs = [rows_of(v) for v in thrs]
            cnts = [sweep_if(active, b, lambda acc, part, j, c, v=v: acc + jnp.where(part > v, 1.0, 0.0),
                             jnp.zeros(tile, F32)) for b, v in enumerate(thr_bs)]
            return [per_row(c, jnp.sum) for c in cnts]

        def band_extent(lo_fs, hi_fs, active):
            edges = [(rows_of(lo_f), rows_of(hi_f)) for lo_f, hi_f in zip(lo_fs, hi_fs)]

            def fn(acc, part, j, c, lo_b, hi_b):
                top, bot = acc
                return (jnp.maximum(top, jnp.where(part <= hi_b, part, -jnp.inf)),
                        jnp.minimum(bot, jnp.where(part > lo_b, part, jnp.inf)))
            init = (jnp.full(tile, -jnp.inf, F32), jnp.full(tile, jnp.inf, F32))
            ext = [sweep_if(active, b, functools.partial(fn, lo_b=lo_b, hi_b=hi_b), init)
                   for b, (lo_b, hi_b) in enumerate(edges)]
            return [(per_row(top, jnp.max), per_row(bot, jnp.min)) for top, bot in ext]

        kks = [jnp.minimum(qi * tq + b * LANES + lane_row + 1, topk).astype(F32) for b in range(len(bands))]

        def unresolved(b, lo, hi, flo):
            return jnp.logical_and(flo != kks[b], hi - 1 > lo)

        def any_set(flags):
            return functools.reduce(jnp.maximum, [jnp.max(jnp.where(f, 1.0, 0.0)) for f in flags])

        def active_bands(state):
            return tuple(jnp.max(jnp.where(unresolved(b, s[0], s[1], s[2]), 1.0, 0.0))
                         for b, s in enumerate(state))

        def midpoint(it, lo, hi):
            mid_key = (lo >> 1) + (hi >> 1) + (lo & hi & 1)
            mid_val = _float_to_key(0.5 * _key_to_float(lo) + 0.5 * _key_to_float(hi))
            by_value = jnp.logical_and(jnp.logical_and(mid_val > lo, mid_val < hi), it < VALUE_STEPS)
            return jnp.where(by_value, mid_val, mid_key)

        def bisect(it, active, state):
            mids = [midpoint(it, lo, hi) for lo, hi, _, _ in state]
            cnts = count_above([_key_to_float(mid) for mid in mids], active)
            out = []
            for b, (lo, hi, flo, fhi) in enumerate(state):
                mid, cnt = mids[b], cnts[b]
                less = cnt < kks[b]
                up = jnp.logical_and(active[b] > 0.0, jnp.logical_not(less))
                down = jnp.logical_and(active[b] > 0.0, less)
                out.append((jnp.where(up, mid, lo), jnp.where(down, mid, hi),
                            jnp.where(up, cnt, flo), jnp.where(down, cnt, fhi)))
            return tuple(out)

        def tighten(active, state):
            ext = band_extent([_key_to_float(s[0]) for s in state], [_key_to_float(s[1]) for s in state],
                              active)
            out = []
            for b, (lo, hi, flo, fhi) in enumerate(state):
                top, bot = ext[b]
                todo = jnp.logical_and(active[b] > 0.0, unresolved(b, lo, hi, flo))
                new_lo = jnp.maximum(_float_to_key(bot) - 1, lo)
                new_hi = jnp.minimum(_float_to_key(top), hi)
                out.append((jnp.where(todo, new_lo, lo), jnp.where(todo, new_hi, hi), flo, fhi))
            return tuple(out)

        def search_body(st):
            it, active, state = st
            do_tighten = jnp.logical_and(it >= TIGHTEN_START, (it - TIGHTEN_START) % TIGHTEN_EVERY == 0)
            state = lax.cond(do_tighten, functools.partial(tighten, active),
                             functools.partial(bisect, it, active), state)
            return it + 1, active_bands(state), state

        hi0s = [_float_to_key(per_row(max1[rows], jnp.max)) for rows in bands]
        lo0s = [jnp.maximum(_float_to_key(per_row(max2[rows], jnp.min)) - 1, KEY_LO_INIT) for rows in bands]
        flo0s = count_above([_key_to_float(lo0) for lo0 in lo0s])
        state0 = tuple((lo0s[b], hi0s[b], flo0s[b], jnp.zeros((1, LANES), F32)) for b in range(len(bands)))
        _, _, state = lax.while_loop(lambda st: functools.reduce(jnp.maximum, st[1]) > 0.0, search_body,
                                     (jnp.int32(0), active_bands(state0), state0))

        tie_rows = [s[2] > kks[b] for b, s in enumerate(state)]
        stack_rows = lambda vecs: jnp.concatenate([rows_of(v) for v in vecs], axis=0)
        thr_ref[...] = stack_rows([_key_to_float(jnp.where(tie_rows[b], s[1], s[0])) for b, s in enumerate(state)])
        val_ref[...] = stack_rows([_key_to_float(s[1]) for s in state])
        need_ref[...] = stack_rows([jnp.where(tie_rows[b], kks[b] - s[3], 0.0) for b, s in enumerate(state)])
        seen_ref[...] = jnp.zeros(seen_ref.shape, F32)
        flag_ref[0] = (any_set(tie_rows) > 0.0).astype(jnp.int32)

    thr_b = thr_ref[...]

    def write_tile(with_ties):
        def run(_):
            seen = seen_ref[...] if with_ties else None
            for u in range(cps):
                blk = sc_ref[g * cps + u]
                for c, part in enumerate(_lane_cols(blk)):
                    hit = part > thr_b
                    if with_ties:
                        tied = part == val_ref[...]
                        counts = _dot(jnp.where(tied, 1.0, 0.0).astype(BF16), tri_ref[...])
                        rank = seen + counts[:, :LANES]
                        hit = jnp.logical_or(hit, jnp.logical_and(tied, rank <= need_ref[...]))
                        seen = seen + counts[:, LANES:]
                    o_ref[:, u * ck + c * LANES:u * ck + (c + 1) * LANES] = jnp.where(hit, 0.0, NEG_INF)
            if with_ties:
                seen_ref[...] = seen
            return 0
        return run

    lax.cond(flag_ref[0] > 0, write_tile(True), write_tile(False), 0)


def _select(iq, ik2, iw, batch, seq, tq, tk, topk):
    ck = SCORE_CHUNK
    assert tq % LANES == 0 and tq % ck == 0 and tk % ck == 0 and seq % tk == 0 and topk <= 2 * LANES
    nq = seq // tq
    i_tab, g_tab = _causal_steps(seq, tq, tk)
    k_idx = jnp.arange(LANES)
    tri = jnp.concatenate([(k_idx[:, None] <= k_idx[None, :]).astype(BF16),
                           jnp.ones((LANES, LANES), BF16)], axis=1)
    q_spec = lambda width: pl.BlockSpec((tq, width), lambda b, t, it, gt: (b * nq + it[t], 0))
    stat = pltpu.VMEM((tq, LANES), F32)
    return pl.pallas_call(
        functools.partial(_select_kernel, tq=tq, tk=tk, topk=topk),
        grid_spec=pltpu.PrefetchScalarGridSpec(
            num_scalar_prefetch=2,
            grid=(batch, i_tab.shape[0]),
            in_specs=[q_spec(BRANCH_WIDTH), q_spec(LANES),
                      pl.BlockSpec((seq, LANES), lambda b, t, it, gt: (b, 0), pipeline_mode=pl.Buffered(1)),
                      pl.BlockSpec(tri.shape, lambda b, t, it, gt: (0, 0))],
            out_specs=pl.BlockSpec((tq, tk), lambda b, t, it, gt: (b * nq + it[t], gt[t])),
            scratch_shapes=[pltpu.VMEM((seq // ck, tq, ck), F32),
                            pltpu.VMEM((IDX_HEADS, tq, LANES), BF16),
                            pltpu.VMEM((IDX_HEADS, tq, LANES), F32),
                            stat, stat, stat, stat,
                            pltpu.SMEM((1,), jnp.int32)]),
        out_shape=jax.ShapeDtypeStruct((batch * seq, seq), F32),
        compiler_params=pltpu.CompilerParams(
            dimension_semantics=("arbitrary", "arbitrary"),
            vmem_limit_bytes=VMEM_LIMIT),
        name="select",
    )(i_tab, g_tab, iq, iw, ik2, tri)


def _sparse_attn_kernel(it_ref, gt_ref, q_ref, k_ref, v_ref, bias_ref, o_ref,
                        qm_ref, m_ref, l_ref, acc_ref, *, tq, tk):
    t = pl.program_id(1)
    qi = it_ref[t]
    g = gt_ref[t]
    lane = lax.broadcasted_iota(jnp.int32, (tq, LANES), 1)
    low_half = lane < HEAD_DIM

    @pl.when(g == 0)
    def _():
        for h in range(DSA_HEADS):
            pair = h // 2
            keep = low_half if h % 2 == 0 else jnp.logical_not(low_half)
            q_pair = q_ref[:, pair * LANES:(pair + 1) * LANES]
            qm_ref[h] = jnp.where(keep, q_pair, jnp.zeros_like(q_pair))
        m_ref[...] = jnp.full(m_ref.shape, NEG_INF, F32)
        l_ref[...] = jnp.zeros(l_ref.shape, F32)
        acc_ref[...] = jnp.zeros(acc_ref.shape, F32)

    for pair in range(DSA_HEADS // 2):
        kc = k_ref[:, pair * LANES:(pair + 1) * LANES]
        vc = v_ref[:, pair * LANES:(pair + 1) * LANES]
        alphas, pvs = [], []
        for h in (2 * pair, 2 * pair + 1):
            s = _dot_nt(qm_ref[h], kc)
            cols = [a + b for a, b in zip(_lane_cols(s), _lane_cols(bias_ref[...]))]
            alpha, pv = _softmax_step(cols, vc, m_ref, l_ref, h)
            alphas.append(alpha)
            pvs.append(pv)
        alpha_pair = jnp.where(low_half, alphas[0], alphas[1])
        pv_pair = jnp.where(low_half, pvs[0], pvs[1])
        acc_ref[pair] = alpha_pair * acc_ref[pair] + pv_pair

    @pl.when((g + 1) * tk >= (qi + 1) * tq)
    def _():
        for pair in range(DSA_HEADS // 2):
            l_pair = jnp.where(low_half, jnp.sum(l_ref[2 * pair], axis=-1, keepdims=True),
                               jnp.sum(l_ref[2 * pair + 1], axis=-1, keepdims=True))
            o_ref[:, pair * LANES:(pair + 1) * LANES] = (acc_ref[pair] / l_pair).astype(BF16)


def _sparse_attn(sq, sk, sv, bias, batch, seq, tq, tk):
    assert tq == tk and seq % tk == 0
    nq, nk = seq // tq, seq // tk
    i_tab, g_tab = _causal_steps(seq, tq, tk)
    q_spec = pl.BlockSpec((tq, BRANCH_WIDTH), lambda b, t, it, gt: (b * nq + it[t], 0))
    kv_spec = pl.BlockSpec((tk, BRANCH_WIDTH), lambda b, t, it, gt: (b * nk + gt[t], 0))
    return pl.pallas_call(
        functools.partial(_sparse_attn_kernel, tq=tq, tk=tk),
        grid_spec=pltpu.PrefetchScalarGridSpec(
            num_scalar_prefetch=2,
            grid=(batch, i_tab.shape[0]),
            in_specs=[q_spec, kv_spec, kv_spec,
                      pl.BlockSpec((tq, tk), lambda b, t, it, gt: (b * nq + it[t], gt[t]))],
            out_specs=q_spec,
            scratch_shapes=[pltpu.VMEM((DSA_HEADS, tq, LANES), BF16),
                            pltpu.VMEM((DSA_HEADS, tq, LANES), F32),
                            pltpu.VMEM((DSA_HEADS, tq, LANES), F32),
                            pltpu.VMEM((DSA_HEADS // 2, tq, LANES), F32)]),
        out_shape=jax.ShapeDtypeStruct(sq.shape, BF16),
        compiler_params=pltpu.CompilerParams(
            dimension_semantics=("arbitrary", "arbitrary"),
            vmem_limit_bytes=VMEM_LIMIT),
        name="sparse_attn",
    )(i_tab, g_tab, sq, sk, sv, bias)


def _merge_kernel(x_ref, ya_ref, yb_ref, p_ref, wg_ref, wbd_ref, wbs_ref, wo_ref, g1_ref, b1_ref,
                  wr_ref, br_ref, wpg_ref, wp_ref, x1_ref, res_ref, comb_ref, *, alpha, d_model):
    x = x_ref[...]
    gates = _dot(x.astype(BF16), wg_ref[...])
    merged = (_sigmoid(gates[:, :d_model]) * _dot(ya_ref[...], wbd_ref[...])
              + _sigmoid(gates[:, d_model:]) * _dot(yb_ref[...], wbs_ref[...]))
    mix = _dot(merged.astype(BF16), wo_ref[...])
    x1 = _layer_norm(alpha * x + mix, g1_ref[...], b1_ref[...])
    x1b = x1.astype(BF16)
    x1_ref[...] = x1b
    ple = _sigmoid(_dot(x1b, wpg_ref[...])) * _dot(p_ref[...].astype(BF16), wp_ref[...])
    res_ref[...] = alpha * x1 + ple

    logits = _dot(x1b, wr_ref[...]) + br_ref[...]
    lane = lax.broadcasted_iota(jnp.int32, logits.shape, 1)
    is_group = jnp.logical_and(lane >= N_EXPERTS, lane < N_EXPERTS + N_GROUPS)
    gl = jnp.where(is_group, logits, NEG_INF)
    gmax = jnp.max(gl, axis=-1, keepdims=True)
    gsum = jnp.sum(jnp.where(is_group, jnp.exp(gl - gmax), 0.0), axis=-1, keepdims=True)
    g_val = 1.0 / gsum
    g_idx = jnp.min(jnp.where(jnp.logical_and(is_group, gl == gmax), lane, 4 * LANES),
                    axis=-1, keepdims=True) - N_EXPERTS
    first = g_idx * EXPERTS_PER_GROUP
    in_group = jnp.logical_and(lane >= first, lane < first + EXPERTS_PER_GROUP)
    el = jnp.where(in_group, logits, NEG_INF)
    e1 = jnp.max(el, axis=-1, keepdims=True)
    i1 = jnp.min(jnp.where(jnp.logical_and(in_group, el == e1), lane, 4 * LANES), axis=-1, keepdims=True)
    el2 = jnp.where(lane == i1, NEG_INF, el)
    e2 = jnp.max(el2, axis=-1, keepdims=True)
    i2 = jnp.min(jnp.where(jnp.logical_and(in_group, el2 == e2), lane, 4 * LANES), axis=-1, keepdims=True)
    p2 = jnp.exp(e2 - e1)
    w1 = 1.0 / (1.0 + p2)
    w2 = p2 / (1.0 + p2)
    comb_ref[...] = g_val * (jnp.where(lane == i1, w1, 0.0) + jnp.where(lane == i2, w2, 0.0))


def _merge(x2, ya, yb, p2, w_g, w_bd, w_bs, w_o, g1, b1, w_r, b_r, w_pg, w_p, tm, alpha):
    n, d = x2.shape
    tok = lambda width: pl.BlockSpec((tm, width), lambda i: (i, 0))
    full = lambda a: pl.BlockSpec(a.shape, lambda i: (0, 0))
    weights = (w_g, w_bd, w_bs, w_o, g1, b1, w_r, b_r, w_pg, w_p)
    return pl.pallas_call(
        functools.partial(_merge_kernel, alpha=alpha, d_model=d),
        grid=(n // tm,),
        in_specs=[tok(d), tok(BRANCH_WIDTH), tok(BRANCH_WIDTH), tok(p2.shape[1])] + [full(w) for w in weights],
        out_specs=[tok(d), tok(d), tok(LANES)],
        out_shape=[jax.ShapeDtypeStruct((n, d), BF16), jax.ShapeDtypeStruct((n, d), F32),
                   jax.ShapeDtypeStruct((n, LANES), F32)],
        compiler_params=pltpu.CompilerParams(dimension_semantics=("arbitrary",),
                                             vmem_limit_bytes=VMEM_LIMIT),
        name="merge",
    )(x2, ya, yb, p2, *weights)


def _moe_kernel(x1_ref, res_ref, comb_ref, wgu_ref, wd_ref, g2_ref, b2_ref, o_ref, h_ref, acc_ref, *,
                d_expert, eps):
    s = pl.program_id(1)
    x = x1_ref[...]
    comb = comb_ref[...]
    lane = lax.broadcasted_iota(jnp.int32, comb.shape, 1)
    for e in range(eps):
        gu = _dot(x, wgu_ref[e])
        ce = jnp.sum(jnp.where(lane == s * eps + e, comb, 0.0), axis=-1, keepdims=True)
        g = gu[:, :d_expert]
        h = g * _sigmoid(g) * gu[:, d_expert:] * ce
        h_ref[:, e * d_expert:(e + 1) * d_expert] = h.astype(BF16)
    y = _dot(h_ref[...], wd_ref[...])

    @pl.when(s == 0)
    def _():
        acc_ref[...] = y

    @pl.when(s > 0)
    def _():
        acc_ref[...] += y

    @pl.when(s == pl.num_programs(1) - 1)
    def _():
        o_ref[...] = _layer_norm(res_ref[...] + acc_ref[...], g2_ref[...], b2_ref[...])


def _moe(x1b, res, comb, w_gu, w_d, g2, b2, tm, eps):
    n, d = x1b.shape
    n_exp, _, two_f = w_gu.shape
    d_expert = two_f // 2
    assert n_exp % eps == 0 and w_d.shape == (n_exp * d_expert, d)
    tok = lambda width: pl.BlockSpec((tm, width), lambda i, s: (i, 0))
    return pl.pallas_call(
        functools.partial(_moe_kernel, d_expert=d_expert, eps=eps),
        grid=(n // tm, n_exp // eps),
        in_specs=[tok(d), tok(d), tok(LANES),
                  pl.BlockSpec((eps, d, two_f), lambda i, s: (s, 0, 0)),
                  pl.BlockSpec((eps * d_expert, d), lambda i, s: (s, 0)),
                  pl.BlockSpec((1, d), lambda i, s: (0, 0)),
                  pl.BlockSpec((1, d), lambda i, s: (0, 0))],
        out_specs=tok(d),
        out_shape=jax.ShapeDtypeStruct((n, d), F32),
        scratch_shapes=[pltpu.VMEM((tm, eps * d_expert), BF16), pltpu.VMEM((tm, d), F32)],
        compiler_params=pltpu.CompilerParams(dimension_semantics=("arbitrary", "arbitrary"),
                                             vmem_limit_bytes=VMEM_LIMIT),
        name="moe",
    )(x1b, res, comb, w_gu, w_d, g2, b2)


def _rope_tables(positions):
    inv_freq = 1.0 / (ROPE_THETA ** (jnp.arange(0, ROT_DIM, 2, dtype=F32) / ROT_DIM))
    ang = positions.astype(F32)[..., None] * inv_freq
    cos, sin = jnp.cos(ang), jnp.sin(ang)
    n = positions.size
    cos, sin = cos.reshape(n, ROT_HALF), sin.reshape(n, ROT_HALF)
    rest = HEAD_DIM - ROT_DIM
    cos_t = jnp.concatenate([cos, cos, jnp.ones((n, rest), F32)], axis=1)
    sin_a = jnp.concatenate([-sin, jnp.zeros((n, rest + ROT_HALF), F32)], axis=1)
    sin_b = jnp.concatenate([jnp.zeros((n, ROT_HALF), F32), sin, jnp.zeros((n, rest), F32)], axis=1)
    rep = LANES // HEAD_DIM
    return tuple(jnp.tile(t, (1, rep)) for t in (cos_t, sin_a, sin_b))


def _tile(n, want):
    t = min(n, want)
    assert n % t == 0, (n, t)
    return t


def kernel(x, p, positions, w_in, diff_lambda, diff_subln_g, w_branch_diff, w_branch_dsa, w_out, ln1_g, ln1_b, w_route_group, b_route_group, w_route_expert, b_route_expert, w_exp_gate, w_exp_up, w_exp_down, w_ple, w_ple_gate, ln2_g, ln2_b):
    batch, seq, d = x.shape
    depth = w_in.shape[0]
    n = batch * seq
    alpha = (2 * depth) ** 0.25
    topk = min(TOPK_MAX, seq // 4)
    d_expert = w_exp_gate.shape[-1]
    assert BRANCH_WIDTH == DIFF_HEADS * 2 * HEAD_DIM == DSA_HEADS * HEAD_DIM == IDX_HEADS * HEAD_DIM
    n_qkv = 7 * BRANCH_WIDTH
    o_ik, o_iw = n_qkv, n_qkv + HEAD_DIM
    o_ga = o_iw + IDX_HEADS
    assert w_in.shape[2] == o_ga + 2 * d

    tm_proj = _tile(n, 512)
    tq_diff = _tile(seq, 1024)
    tk_diff = _tile(seq, 1024)
    tq_select = _tile(seq, 512)
    tk_dsa = _tile(seq, 1024)
    tm_merge = _tile(n, 256)
    tm_moe = _tile(n, 1024)

    cos_t, sin_a, sin_b = _rope_tables(positions)
    scale = HEAD_DIM ** -0.5
    col_scale = jnp.ones((n_qkv,), F32)
    for seg in (0, 3, 6):
        col_scale = col_scale.at[seg * BRANCH_WIDTH:(seg + 1) * BRANCH_WIDTH].set(scale)

    x2 = x.reshape(n, d)
    for i in range(depth):
        lam_init = 0.8 - 0.6 * math.exp(-0.3 * i)
        wi = w_in[i]
        w_ik = wi[:, o_ik:o_ik + HEAD_DIM]
        w_iw = jnp.pad(wi[:, o_iw:o_iw + IDX_HEADS], ((0, 0), (0, LANES - IDX_HEADS)))
        w_a = jnp.concatenate([wi[:, :n_qkv] * col_scale, w_ik, w_ik, w_iw], axis=1).astype(BF16)
        w_g = wi[:, o_ga:].astype(BF16)
        w_r = jnp.pad(jnp.concatenate([w_route_expert[i], w_route_group[i]], axis=1),
                      ((0, 0), (0, LANES - N_EXPERTS - N_GROUPS))).astype(BF16)
        b_r = jnp.pad(jnp.concatenate([b_route_expert[i], b_route_group[i]]),
                      (0, LANES - N_EXPERTS - N_GROUPS)).reshape(1, LANES)
        w_gu = jnp.concatenate([w_exp_gate[i], w_exp_up[i]], axis=-1).reshape(
            N_EXPERTS, d, 2 * d_expert).astype(BF16)
        w_d = w_exp_down[i].reshape(N_EXPERTS * d_expert, d).astype(BF16)

        dq, dk, dv, sq, sk, sv, iq, ik2, iw = _inproj(x2, w_a, cos_t, sin_a, sin_b, tm_proj)
        ya = _diff_attn(dq, dk, dv, diff_lambda[i], diff_subln_g[i], batch, seq, tq_diff, tk_diff, lam_init)
        bias = _select(iq, ik2, iw, batch, seq, tq_select, tk_dsa, topk)
        yb = _sparse_attn(sq, sk, sv, bias, batch, seq, tk_dsa, tk_dsa)
        x1b, res, comb = _merge(
            x2, ya, yb, p[i].reshape(n, -1), w_g,
            w_branch_diff[i].astype(BF16), w_branch_dsa[i].astype(BF16), w_out[i].astype(BF16),
            ln1_g[i].reshape(1, d), ln1_b[i].reshape(1, d), w_r, b_r,
            w_ple_gate[i].astype(BF16), w_ple[i].astype(BF16), tm_merge, alpha)
        x2 = _moe(x1b, res, comb, w_gu, w_d, ln2_g[i].reshape(1, d), ln2_b[i].reshape(1, d), tm_moe,
                   MOE_EXPERTS_PER_STEP)
    return x2.reshape(batch, seq, d)
```

```python
import functools
import math

import jax
import jax.numpy as jnp
import numpy as np
from jax import lax
from jax.experimental import pallas as pl
from jax.experimental.pallas import tpu as pltpu

F32 = jnp.float32
BF16 = jnp.bfloat16

HEAD_DIM = 64
DIFF_HEADS = 4
DSA_HEADS = 8
IDX_HEADS = 8
TOPK_MAX = 256
N_GROUPS = 4
EXPERTS_PER_GROUP = 8
N_EXPERTS = N_GROUPS * EXPERTS_PER_GROUP
MOE_EXPERTS_PER_STEP = 4
ROPE_THETA = 500000.0
ROT_DIM = HEAD_DIM // 4
ROT_HALF = ROT_DIM // 2
LN_EPS = 1e-5
NEG_INF = -1e30
LOG2E = math.log2(math.e)
LANES = 128
BRANCH_WIDTH = 512
VMEM_LIMIT = 62 * 1024 * 1024


SIGN_BIT = np.int32(-2 ** 31)
MAGNITUDE_BITS = np.int32(2 ** 31 - 1)
SUBNORMAL_BITS = np.int32(2 ** 23 - 1)


def _float_key(v):
    bits = int(np.float32(v).view(np.int32))
    mag = max((bits & int(MAGNITUDE_BITS)) - int(SUBNORMAL_BITS), 0)
    return -mag if bits < 0 else mag


KEY_LO_INIT = _float_key(NEG_INF)
VALUE_STEPS = 24
TIGHTEN_START = 13
TIGHTEN_EVERY = 4
SCORE_CHUNK = 512


def _dot_nt(a, b):
    return lax.dot_general(a, b, (((1,), (1,)), ((), ())), preferred_element_type=F32)


def _dot(a, b):
    return jnp.dot(a, b, preferred_element_type=F32)


def _sigmoid(v):
    return 1.0 / (1.0 + jnp.exp(-v))


def _layer_norm(v, g, b):
    mu = jnp.mean(v, axis=-1, keepdims=True)
    d = v - mu
    var = jnp.mean(d * d, axis=-1, keepdims=True)
    return d * lax.rsqrt(var + LN_EPS) * g + b


def _inproj_kernel(x_ref, w_ref, c_ref, sa_ref, sb_ref,
                   dq_ref, dk_ref, dv_ref, sq_ref, sk_ref, sv_ref, iq_ref, ik_ref, iw_ref):
    xb = x_ref[...].astype(BF16)
    cos_t = c_ref[...]
    sin_a = sa_ref[...]
    sin_b = sb_ref[...]

    def rope(h):
        cols = []
        for c in range(h.shape[1] // LANES):
            hc = h[:, c * LANES:(c + 1) * LANES]
            cols.append(hc * cos_t
                        + pltpu.roll(hc, LANES - ROT_HALF, 1) * sin_a
                        + pltpu.roll(hc, ROT_HALF, 1) * sin_b)
        return cols[0] if len(cols) == 1 else jnp.concatenate(cols, axis=1)

    outs = ((dq_ref, True, LOG2E), (dk_ref, True, None), (dv_ref, False, None), (sq_ref, True, LOG2E),
            (sk_ref, True, None), (sv_ref, False, None), (iq_ref, True, None))
    for i, (ref, rotary, mult) in enumerate(outs):
        h = _dot(xb, w_ref[:, i * BRANCH_WIDTH:(i + 1) * BRANCH_WIDTH])
        if mult is not None:
            h = h * mult
        if rotary:
            h = rope(h)
        ref[...] = h.astype(BF16)
    base = len(outs) * BRANCH_WIDTH
    h = _dot(xb, w_ref[:, base:base + 2 * LANES])
    ik_ref[...] = rope(h[:, :LANES]).astype(BF16)
    iw_ref[...] = h[:, LANES:] * (IDX_HEADS ** -0.5)


def _inproj(x2, w_a, cos_t, sin_a, sin_b, tm):
    n, d = x2.shape
    wc = w_a.shape[1]
    tok = lambda width: pl.BlockSpec((tm, width), lambda i: (i, 0))
    out_shape = [jax.ShapeDtypeStruct((n, BRANCH_WIDTH), BF16)] * 7 + [
        jax.ShapeDtypeStruct((n, LANES), BF16), jax.ShapeDtypeStruct((n, LANES), F32)]
    return pl.pallas_call(
        _inproj_kernel,
        grid=(n // tm,),
        in_specs=[tok(d), pl.BlockSpec((d, wc), lambda i: (0, 0)), tok(LANES), tok(LANES), tok(LANES)],
        out_specs=[tok(BRANCH_WIDTH)] * 7 + [tok(LANES), tok(LANES)],
        out_shape=out_shape,
        compiler_params=pltpu.CompilerParams(dimension_semantics=("arbitrary",),
                                             vmem_limit_bytes=VMEM_LIMIT),
        name="inproj",
    )(x2, w_a, cos_t, sin_a, sin_b)


def _lane_cols(s):
    return [s[:, c * LANES:(c + 1) * LANES] for c in range(s.shape[1] // LANES)]


def _softmax_step(cols, vc, m_ref, l_ref, idx):
    mx = functools.reduce(jnp.maximum, cols)
    m_prev = m_ref[idx]
    m_new = jnp.maximum(m_prev, jnp.max(mx, axis=-1, keepdims=True))
    alpha = jnp.exp2(m_prev - m_new)
    ps = [jnp.exp2(c - m_new) for c in cols]
    l_ref[idx] = alpha * l_ref[idx] + functools.reduce(jnp.add, ps)
    m_ref[idx] = m_new
    p = ps[0] if len(ps) == 1 else jnp.concatenate(ps, axis=1)
    return alpha, _dot(p.astype(BF16), vc)


def _diff_kernel(q_ref, k_ref, v_ref, lam_ref, g_ref, o_ref, qm_ref, m_ref, l_ref, acc_ref, *,
                 tq, lam_init):
    qi = pl.program_id(2)
    q = q_ref[...]
    lane = lax.broadcasted_iota(jnp.int32, q.shape, 1)
    zero = jnp.zeros_like(q)
    qm_ref[0] = jnp.where(lane < HEAD_DIM, q, zero)
    qm_ref[1] = jnp.where(lane >= HEAD_DIM, q, zero)
    m_ref[...] = jnp.full(m_ref.shape, NEG_INF, F32)
    l_ref[...] = jnp.zeros(l_ref.shape, F32)
    acc_ref[...] = jnp.zeros(acc_ref.shape, F32)

    def tile(rows, start, n_keys, causal):
        kc = k_ref[pl.ds(start, n_keys), :]
        vc = v_ref[pl.ds(start, n_keys), :]
        for c in range(2):
            s = _dot_nt(qm_ref[c, rows], kc)
            if causal:
                row = qi * tq + rows.start + lax.broadcasted_iota(jnp.int32, s.shape, 0)
                col = start + lax.broadcasted_iota(jnp.int32, s.shape, 1)
                s = jnp.where(col <= row, s, NEG_INF)
            alpha, pv = _softmax_step(_lane_cols(s), vc, m_ref, l_ref, (c, rows))
            acc_ref[c, rows] = alpha * acc_ref[c, rows] + pv

    def body(j, carry):
        tile(slice(0, tq), pl.multiple_of(j * tq, tq), tq, False)
        return carry

    lax.fori_loop(0, qi, body, 0)
    diag = pl.multiple_of(qi * tq, tq)
    half = tq // 2
    tile(slice(0, half), diag, half, True)
    tile(slice(half, tq), diag, tq, True)

    lp = lam_ref[...]
    lam = (jnp.exp(jnp.sum(lp[0:1] * lp[1:2], axis=-1, keepdims=True))
           - jnp.exp(jnp.sum(lp[2:3] * lp[3:4], axis=-1, keepdims=True)) + lam_init)
    l0 = jnp.sum(l_ref[0], axis=-1, keepdims=True)
    l1 = jnp.sum(l_ref[1], axis=-1, keepdims=True)
    o = acc_ref[0] / l0 - lam * (acc_ref[1] / l1)
    o = o * lax.rsqrt(jnp.mean(o * o, axis=-1, keepdims=True) + LN_EPS)
    o_ref[...] = (o * g_ref[...] * (1.0 - lam_init)).astype(BF16)


def _diff_attn(dq, dk, dv, lam_params, subln_g, batch, seq, tq, lam_init):
    assert seq % tq == 0 and tq % (2 * LANES) == 0
    nq = seq // tq
    hw = 2 * HEAD_DIM
    kv_spec = pl.BlockSpec((seq, hw), lambda b, h, i: (b, h))
    q_spec = pl.BlockSpec((tq, hw), lambda b, h, i: (b * nq + i, h))
    return pl.pallas_call(
        functools.partial(_diff_kernel, tq=tq, lam_init=lam_init),
        grid=(batch, DIFF_HEADS, nq),
        in_specs=[q_spec, kv_spec, kv_spec,
                  pl.BlockSpec(lam_params.shape, lambda b, h, i: (0, 0)),
                  pl.BlockSpec((1, hw), lambda b, h, i: (0, 0))],
        out_specs=q_spec,
        out_shape=jax.ShapeDtypeStruct(dq.shape, BF16),
        scratch_shapes=[pltpu.VMEM((2, tq, hw), BF16), pltpu.VMEM((2, tq, LANES), F32),
                        pltpu.VMEM((2, tq, LANES), F32), pltpu.VMEM((2, tq, hw), F32)],
        compiler_params=pltpu.CompilerParams(
            dimension_semantics=("arbitrary", "arbitrary", "arbitrary"),
            vmem_limit_bytes=VMEM_LIMIT),
        name="diff_attn",
    )(dq, dk, dv, lam_params, subln_g.reshape(1, hw))


def _key_to_float(key):
    mag = jnp.abs(key)
    bits = jnp.where(mag > 0, mag + SUBNORMAL_BITS, 0)
    return lax.bitcast_convert_type(jnp.where(key < 0, bits | SIGN_BIT, bits), F32)


def _float_to_key(v):
    bits = lax.bitcast_convert_type(v, jnp.int32)
    mag = jnp.maximum((bits & MAGNITUDE_BITS) - SUBNORMAL_BITS, 0)
    return jnp.where(bits < 0, -mag, mag)


def _causal_steps(seq, tq, tk):
    pairs = [(i, g) for i in range(seq // tq) for g in range(-(-(i + 1) * tq // tk))]
    return (jnp.asarray([p[0] for p in pairs], jnp.int32), jnp.asarray([p[1] for p in pairs], jnp.int32))


def _select_kernel(it_ref, gt_ref, iq_ref, iw_ref, ik_ref, tri_ref, o_ref,
                   sc_ref, iqm_ref, wb_ref, thr_ref, val_ref, need_ref, seen_ref, flag_ref, *,
                   tq, tk, topk):
    t = pl.program_id(1)
    qi = it_ref[t]
    g = gt_ref[t]
    ck = SCORE_CHUNK
    cps = tk // ck
    bands = [slice(b * LANES, (b + 1) * LANES) for b in range(tq // LANES)]
    tile = (LANES, LANES)

    def rows_of(vec):
        return jnp.broadcast_to(vec, tile).T

    def per_row(mat, reduce):
        return reduce(mat.T, axis=0, keepdims=True)

    @pl.when(g == 0)
    def _search():
        lane = lax.broadcasted_iota(jnp.int32, (tq, LANES), 1)
        low_half = lane < HEAD_DIM
        nch = (qi + 1) * tq // ck
        n_plain = qi * tq // ck

        iw = iw_ref[...]
        for h in range(IDX_HEADS):
            pair = h // 2
            keep = low_half if h % 2 == 0 else jnp.logical_not(low_half)
            iq_pair = iq_ref[:, pair * LANES:(pair + 1) * LANES]
            iqm_ref[h] = jnp.where(keep, iq_pair, jnp.zeros_like(iq_pair))
            wb_ref[h] = jnp.broadcast_to(iw[:, h:h + 1], (tq, LANES))

        def score_chunk(j, masked, top2):
            start = pl.multiple_of(j * ck, ck)
            kc = ik_ref[pl.ds(start, ck), :]
            parts = [jnp.zeros((tq, LANES), F32) for _ in range(ck // LANES)]
            for h in range(IDX_HEADS):
                d = _dot_nt(iqm_ref[h], kc)
                w = wb_ref[h]
                for c in range(ck // LANES):
                    parts[c] = parts[c] + w * jnp.maximum(d[:, c * LANES:(c + 1) * LANES], 0.0)
            sc = jnp.concatenate(parts, axis=1) + 0.0
            if masked:
                row = qi * tq + lax.broadcasted_iota(jnp.int32, (tq, ck), 0)
                col = start + lax.broadcasted_iota(jnp.int32, (tq, ck), 1)
                sc = jnp.where(col <= row, sc, NEG_INF)
            sc_ref[j] = sc
            max1, max2 = top2
            for part in _lane_cols(sc):
                max2 = jnp.maximum(max2, jnp.minimum(max1, part))
                max1 = jnp.maximum(max1, part)
            return max1, max2

        neg = jnp.full((tq, LANES), NEG_INF, F32)
        top2 = lax.fori_loop(0, n_plain, lambda j, s: score_chunk(j, False, s), (neg, neg))
        max1, max2 = lax.fori_loop(n_plain, nch, lambda j, s: score_chunk(j, True, s), top2)

        n_steps = (nch + cps - 1) // cps

        def fill(j, carry):
            sc_ref[j] = jnp.full((tq, ck), NEG_INF, F32)
            return carry
        lax.fori_loop(nch, n_steps * cps, fill, 0)

        lane_row = lax.broadcasted_iota(jnp.int32, (1, LANES), 1)

        def sweep(b, fn, init):
            def body(j, acc):
                for c, part in enumerate(_lane_cols(sc_ref[j, bands[b], :])):
                    acc = fn(acc, part, j, c)
                return acc
            return lax.fori_loop(0, nch, body, init)

        def sweep_if(active, b, fn, init):
            if active is None:
                return sweep(b, fn, init)
            return lax.cond(active[b] > 0.0, lambda: sweep(b, fn, init), lambda: init)

        def count_above(thrs, active=None):
            thr_bs = [rows_of(v) for v in thrs]
            cnts = [sweep_if(active, b, lambda acc, part, j, c, v=v: acc + jnp.where(part > v, 1.0, 0.0),
                             jnp.zeros(tile, F32)) for b, v in enumerate(thr_bs)]
            return [per_row(c, jnp.sum) for c in cnts]

        def band_extent(lo_fs, hi_fs, active):
            edges = [(rows_of(lo_f), rows_of(hi_f)) for lo_f, hi_f in zip(lo_fs, hi_fs)]

            def fn(acc, part, j, c, lo_b, hi_b):
                top, bot = acc
                return (jnp.maximum(top, jnp.where(part <= hi_b, part, -jnp.inf)),
                        jnp.minimum(bot, jnp.where(part > lo_b, part, jnp.inf)))
            init = (jnp.full(tile, -jnp.inf, F32), jnp.full(tile, jnp.inf, F32))
            ext = [sweep_if(active, b, functools.partial(fn, lo_b=lo_b, hi_b=hi_b), init)
                   for b, (lo_b, hi_b) in enumerate(edges)]
            return [(per_row(top, jnp.max), per_row(bot, jnp.min)) for top, bot in ext]

        kks = [jnp.minimum(qi * tq + b * LANES + lane_row + 1, topk).astype(F32) for b in range(len(bands))]

        def unresolved(b, lo, hi, flo):
            return jnp.logical_and(flo != kks[b], hi - 1 > lo)

        def any_set(flags):
            return functools.reduce(jnp.maximum, [jnp.max(jnp.where(f, 1.0, 0.0)) for f in flags])

        def active_bands(state):
            return tuple(jnp.max(jnp.where(unresolved(b, s[0], s[1], s[2]), 1.0, 0.0))
                         for b, s in enumerate(state))

        def midpoint(it, lo, hi):
            mid_key = (lo >> 1) + (hi >> 1) + (lo & hi & 1)
            mid_val = _float_to_key(0.5 * _key_to_float(lo) + 0.5 * _key_to_float(hi))
            by_value = jnp.logical_and(jnp.logical_and(mid_val > lo, mid_val < hi), it < VALUE_STEPS)
            return jnp.where(by_value, mid_val, mid_key)

        def bisect(it, active, state):
            mids = [midpoint(it, lo, hi) for lo, hi, _, _ in state]
            cnts = count_above([_key_to_float(mid) for mid in mids], active)
            out = []
            for b, (lo, hi, flo, fhi) in enumerate(state):
                mid, cnt = mids[b], cnts[b]
                less = cnt < kks[b]
                up = jnp.logical_and(active[b] > 0.0, jnp.logical_not(less))
                down = jnp.logical_and(active[b] > 0.0, less)
                out.append((jnp.where(up, mid, lo), jnp.where(down, mid, hi),
                            jnp.where(up, cnt, flo), jnp.where(down, cnt, fhi)))
            return tuple(out)

        def tighten(active, state):
            ext = band_extent([_key_to_float(s[0]) for s in state], [_key_to_float(s[1]) for s in state],
                              active)
            out = []
            for b, (lo, hi, flo, fhi) in enumerate(state):
                top, bot = ext[b]
                todo = jnp.logical_and(active[b] > 0.0, unresolved(b, lo, hi, flo))
                new_lo = jnp.maximum(_float_to_key(bot) - 1, lo)
                new_hi = jnp.minimum(_float_to_key(top), hi)
                out.append((jnp.where(todo, new_lo, lo), jnp.where(todo, new_hi, hi), flo, fhi))
            return tuple(out)

        def search_body(st):
            it, active, state = st
            do_tighten = jnp.logical_and(it >= TIGHTEN_START, (it - TIGHTEN_START) % TIGHTEN_EVERY == 0)
            state = lax.cond(do_tighten, functools.partial(tighten, active),
                             functools.partial(bisect, it, active), state)
            return it + 1, active_bands(state), state

        hi0s = [_float_to_key(per_row(max1[rows], jnp.max)) for rows in bands]
        lo0s = [jnp.maximum(_float_to_key(per_row(max2[rows], jnp.min)) - 1, KEY_LO_INIT) for rows in bands]
        flo0s = count_above([_key_to_float(lo0) for lo0 in lo0s])
        state0 = tuple((lo0s[b], hi0s[b], flo0s[b], jnp.zeros((1, LANES), F32)) for b in range(len(bands)))
        _, _, state = lax.while_loop(lambda st: functools.reduce(jnp.maximum, st[1]) > 0.0, search_body,
                                     (jnp.int32(0), active_bands(state0), state0))

        tie_rows = [s[2] > kks[b] for b, s in enumerate(state)]
        stack_rows = lambda vecs: jnp.concatenate([rows_of(v) for v in vecs], axis=0)
        thr_ref[...] = stack_rows([_key_to_float(jnp.where(tie_rows[b], s[1], s[0])) for b, s in enumerate(state)])
        val_ref[...] = stack_rows([_key_to_float(s[1]) for s in state])
        need_ref[...] = stack_rows([jnp.where(tie_rows[b], kks[b] - s[3], 0.0) for b, s in enumerate(state)])
        seen_ref[...] = jnp.zeros(seen_ref.shape, F32)
        flag_ref[0] = (any_set(tie_rows) > 0.0).astype(jnp.int32)

    thr_b = thr_ref[...]

    def write_tile(with_ties):
        def run(_):
            seen = seen_ref[...] if with_ties else None
            for u in range(cps):
                blk = sc_ref[g * cps + u]
                for c, part in enumerate(_lane_cols(blk)):
                    hit = part > thr_b
                    if with_ties:
                        tied = part == val_ref[...]
                        counts = _dot(jnp.where(tied, 1.0, 0.0).astype(BF16), tri_ref[...])
                        rank = seen + counts[:, :LANES]
                        hit = jnp.logical_or(hit, jnp.logical_and(tied, rank <= need_ref[...]))
                        seen = seen + counts[:, LANES:]
                    o_ref[:, u * ck + c * LANES:u * ck + (c + 1) * LANES] = jnp.where(hit, 0.0, NEG_INF)
            if with_ties:
                seen_ref[...] = seen
            return 0
        return run

    lax.cond(flag_ref[0] > 0, write_tile(True), write_tile(False), 0)


def _select(iq, ik2, iw, batch, seq, tq, tk, topk):
    ck = SCORE_CHUNK
    assert tq % LANES == 0 and tq % ck == 0 and tk % ck == 0 and seq % tk == 0 and topk <= 2 * LANES
    nq = seq // tq
    i_tab, g_tab = _causal_steps(seq, tq, tk)
    k_idx = jnp.arange(LANES)
    tri = jnp.concatenate([(k_idx[:, None] <= k_idx[None, :]).astype(BF16),
                           jnp.ones((LANES, LANES), BF16)], axis=1)
    q_spec = lambda width: pl.BlockSpec((tq, width), lambda b, t, it, gt: (b * nq + it[t], 0))
    stat = pltpu.VMEM((tq, LANES), F32)
    return pl.pallas_call(
        functools.partial(_select_kernel, tq=tq, tk=tk, topk=topk),
        grid_spec=pltpu.PrefetchScalarGridSpec(
            num_scalar_prefetch=2,
            grid=(batch, i_tab.shape[0]),
            in_specs=[q_spec(BRANCH_WIDTH), q_spec(LANES),
                      pl.BlockSpec((seq, LANES), lambda b, t, it, gt: (b, 0), pipeline_mode=pl.Buffered(1)),
                      pl.BlockSpec(tri.shape, lambda b, t, it, gt: (0, 0))],
            out_specs=pl.BlockSpec((tq, tk), lambda b, t, it, gt: (b * nq + it[t], gt[t])),
            scratch_shapes=[pltpu.VMEM((seq // ck, tq, ck), F32),
                            pltpu.VMEM((IDX_HEADS, tq, LANES), BF16),
                            pltpu.VMEM((IDX_HEADS, tq, LANES), F32),
                            stat, stat, stat, stat,
                            pltpu.SMEM((1,), jnp.int32)]),
        out_shape=jax.ShapeDtypeStruct((batch * seq, seq), F32),
        compiler_params=pltpu.CompilerParams(
            dimension_semantics=("arbitrary", "arbitrary"),
            vmem_limit_bytes=VMEM_LIMIT),
        name="select",
    )(i_tab, g_tab, iq, iw, ik2, tri)


def _sparse_attn_kernel(it_ref, gt_ref, q_ref, k_ref, v_ref, bias_ref, o_ref,
                        qm_ref, m_ref, l_ref, acc_ref, *, tq, tk):
    t = pl.program_id(1)
    qi = it_ref[t]
    g = gt_ref[t]
    lane = lax.broadcasted_iota(jnp.int32, (tq, LANES), 1)
    low_half = lane < HEAD_DIM

    @pl.when(g == 0)
    def _():
        for h in range(DSA_HEADS):
            pair = h // 2
            keep = low_half if h % 2 == 0 else jnp.logical_not(low_half)
            q_pair = q_ref[:, pair * LANES:(pair + 1) * LANES]
            qm_ref[h] = jnp.where(keep, q_pair, jnp.zeros_like(q_pair))
        m_ref[...] = jnp.full(m_ref.shape, NEG_INF, F32)
        l_ref[...] = jnp.zeros(l_ref.shape, F32)
        acc_ref[...] = jnp.zeros(acc_ref.shape, F32)

    def attend(rows, n_keys):
        low = lax.broadcasted_iota(jnp.int32, (rows.stop - rows.start, LANES), 1) < HEAD_DIM
        for pair in range(DSA_HEADS // 2):
            kc = k_ref[0:n_keys, pair * LANES:(pair + 1) * LANES]
            vc = v_ref[0:n_keys, pair * LANES:(pair + 1) * LANES]
            alphas, pvs = [], []
            for h in (2 * pair, 2 * pair + 1):
                s = _dot_nt(qm_ref[h, rows], kc)
                cols = [a + b for a, b in zip(_lane_cols(s), _lane_cols(bias_ref[rows, 0:n_keys]))]
                alpha, pv = _softmax_step(cols, vc, m_ref, l_ref, (h, rows))
                alphas.append(alpha)
                pvs.append(pv)
            alpha_pair = jnp.where(low, alphas[0], alphas[1])
            pv_pair = jnp.where(low, pvs[0], pvs[1])
            acc_ref[pair, rows] = alpha_pair * acc_ref[pair, rows] + pv_pair

    @pl.when(g < qi)
    def _():
        attend(slice(0, tq), tk)

    @pl.when(g == qi)
    def _():
        half = tq // 2
        attend(slice(0, half), half)
        attend(slice(half, tq), tk)
        for pair in range(DSA_HEADS // 2):
            l_pair = jnp.where(low_half, jnp.sum(l_ref[2 * pair], axis=-1, keepdims=True),
                               jnp.sum(l_ref[2 * pair + 1], axis=-1, keepdims=True))
            o_ref[:, pair * LANES:(pair + 1) * LANES] = (acc_ref[pair] / l_pair).astype(BF16)


def _sparse_attn(sq, sk, sv, bias, batch, seq, tq, tk):
    assert tq == tk and seq % tk == 0
    nq, nk = seq // tq, seq // tk
    i_tab, g_tab = _causal_steps(seq, tq, tk)
    q_spec = pl.BlockSpec((tq, BRANCH_WIDTH), lambda b, t, it, gt: (b * nq + it[t], 0))
    kv_spec = pl.BlockSpec((tk, BRANCH_WIDTH), lambda b, t, it, gt: (b * nk + gt[t], 0))
    return pl.pallas_call(
        functools.partial(_sparse_attn_kernel, tq=tq, tk=tk),
        grid_spec=pltpu.PrefetchScalarGridSpec(
            num_scalar_prefetch=2,
            grid=(batch, i_tab.shape[0]),
            in_specs=[q_spec, kv_spec, kv_spec,
                      pl.BlockSpec((tq, tk), lambda b, t, it, gt: (b * nq + it[t], gt[t]))],
            out_specs=q_spec,
            scratch_shapes=[pltpu.VMEM((DSA_HEADS, tq, LANES), BF16),
                            pltpu.VMEM((DSA_HEADS, tq, LANES), F32),
                            pltpu.VMEM((DSA_HEADS, tq, LANES), F32),
                            pltpu.VMEM((DSA_HEADS // 2, tq, LANES), F32)]),
        out_shape=jax.ShapeDtypeStruct(sq.shape, BF16),
        compiler_params=pltpu.CompilerParams(
            dimension_semantics=("arbitrary", "arbitrary"),
            vmem_limit_bytes=VMEM_LIMIT),
        name="sparse_attn",
    )(i_tab, g_tab, sq, sk, sv, bias)


def _merge_kernel(x_ref, ya_ref, yb_ref, p_ref, wg_ref, wbd_ref, wbs_ref, wo_ref, g1_ref, b1_ref,
                  wr_ref, br_ref, wpg_ref, wp_ref, x1_ref, res_ref, comb_ref, *, alpha, d_model):
    x = x_ref[...]
    gates = _dot(x.astype(BF16), wg_ref[...])
    merged = (_sigmoid(gates[:, :d_model]) * _dot(ya_ref[...], wbd_ref[...])
              + _sigmoid(gates[:, d_model:]) * _dot(yb_ref[...], wbs_ref[...]))
    mix = _dot(merged.astype(BF16), wo_ref[...])
    x1 = _layer_norm(alpha * x + mix, g1_ref[...], b1_ref[...])
    x1b = x1.astype(BF16)
    x1_ref[...] = x1b
    ple = _sigmoid(_dot(x1b, wpg_ref[...])) * _dot(p_ref[...].astype(BF16), wp_ref[...])
    res_ref[...] = alpha * x1 + ple

    logits = _dot(x1b, wr_ref[...]) + br_ref[...]
    lane = lax.broadcasted_iota(jnp.int32, logits.shape, 1)
    is_group = jnp.logical_and(lane >= N_EXPERTS, lane < N_EXPERTS + N_GROUPS)
    gl = jnp.where(is_group, logits, NEG_INF)
    gmax = jnp.max(gl, axis=-1, keepdims=True)
    gsum = jnp.sum(jnp.where(is_group, jnp.exp(gl - gmax), 0.0), axis=-1, keepdims=True)
    g_val = 1.0 / gsum
    g_idx = jnp.min(jnp.where(jnp.logical_and(is_group, gl == gmax), lane, 4 * LANES),
                    axis=-1, keepdims=True) - N_EXPERTS
    first = g_idx * EXPERTS_PER_GROUP
    in_group = jnp.logical_and(lane >= first, lane < first + EXPERTS_PER_GROUP)
    el = jnp.where(in_group, logits, NEG_INF)
    e1 = jnp.max(el, axis=-1, keepdims=True)
    i1 = jnp.min(jnp.where(jnp.logical_and(in_group, el == e1), lane, 4 * LANES), axis=-1, keepdims=True)
    el2 = jnp.where(lane == i1, NEG_INF, el)
    e2 = jnp.max(el2, axis=-1, keepdims=True)
    i2 = jnp.min(jnp.where(jnp.logical_and(in_group, el2 == e2), lane, 4 * LANES), axis=-1, keepdims=True)
    p2 = jnp.exp(e2 - e1)
    w1 = 1.0 / (1.0 + p2)
    w2 = p2 / (1.0 + p2)
    comb_ref[...] = g_val * (jnp.where(lane == i1, w1, 0.0) + jnp.where(lane == i2, w2, 0.0))


def _merge(x2, ya, yb, p2, w_g, w_bd, w_bs, w_o, g1, b1, w_r, b_r, w_pg, w_p, tm, alpha):
    n, d = x2.shape
    tok = lambda width: pl.BlockSpec((tm, width), lambda i: (i, 0))
    full = lambda a: pl.BlockSpec(a.shape, lambda i: (0, 0))
    weights = (w_g, w_bd, w_bs, w_o, g1, b1, w_r, b_r, w_pg, w_p)
    return pl.pallas_call(
        functools.partial(_merge_kernel, alpha=alpha, d_model=d),
        grid=(n // tm,),
        in_specs=[tok(d), tok(BRANCH_WIDTH), tok(BRANCH_WIDTH), tok(p2.shape[1])] + [full(w) for w in weights],
        out_specs=[tok(d), tok(d), tok(LANES)],
        out_shape=[jax.ShapeDtypeStruct((n, d), BF16), jax.ShapeDtypeStruct((n, d), F32),
                   jax.ShapeDtypeStruct((n, LANES), F32)],
        compiler_params=pltpu.CompilerParams(dimension_semantics=("arbitrary",),
                                             vmem_limit_bytes=VMEM_LIMIT),
        name="merge",
    )(x2, ya, yb, p2, *weights)


def _moe_kernel(x1_ref, res_ref, comb_ref, wgu_ref, wd_ref, g2_ref, b2_ref, o_ref, h_ref, acc_ref, *,
                d_expert, eps):
    s = pl.program_id(1)
    x = x1_ref[...]
    comb = comb_ref[...]
    lane = lax.broadcasted_iota(jnp.int32, comb.shape, 1)
    for e in range(eps):
        gu = _dot(x, wgu_ref[e])
        ce = jnp.sum(jnp.where(lane == s * eps + e, comb, 0.0), axis=-1, keepdims=True)
        g = gu[:, :d_expert]
        h = g * _sigmoid(g) * gu[:, d_expert:] * ce
        h_ref[:, e * d_expert:(e + 1) * d_expert] = h.astype(BF16)
    y = _dot(h_ref[...], wd_ref[...])

    @pl.when(s == 0)
    def _():
        acc_ref[...] = y

    @pl.when(s > 0)
    def _():
        acc_ref[...] += y

    @pl.when(s == pl.num_programs(1) - 1)
    def _():
        o_ref[...] = _layer_norm(res_ref[...] + acc_ref[...], g2_ref[...], b2_ref[...])


def _moe(x1b, res, comb, w_gu, w_d, g2, b2, tm, eps):
    n, d = x1b.shape
    n_exp, _, two_f = w_gu.shape
    d_expert = two_f // 2
    assert n_exp % eps == 0 and w_d.shape == (n_exp * d_expert, d)
    tok = lambda width: pl.BlockSpec((tm, width), lambda i, s: (i, 0))
    return pl.pallas_call(
        functools.partial(_moe_kernel, d_expert=d_expert, eps=eps),
        grid=(n // tm, n_exp // eps),
        in_specs=[tok(d), tok(d), tok(LANES),
                  pl.BlockSpec((eps, d, two_f), lambda i, s: (s, 0, 0)),
                  pl.BlockSpec((eps * d_expert, d), lambda i, s: (s, 0)),
                  pl.BlockSpec((1, d), lambda i, s: (0, 0)),
                  pl.BlockSpec((1, d), lambda i, s: (0, 0))],
        out_specs=tok(d),
        out_shape=jax.ShapeDtypeStruct((n, d), F32),
        scratch_shapes=[pltpu.VMEM((tm, eps * d_expert), BF16), pltpu.VMEM((tm, d), F32)],
        compiler_params=pltpu.CompilerParams(dimension_semantics=("arbitrary", "arbitrary"),
                                             vmem_limit_bytes=VMEM_LIMIT),
        name="moe",
    )(x1b, res, comb, w_gu, w_d, g2, b2)


def _rope_tables(positions):
    inv_freq = 1.0 / (ROPE_THETA ** (jnp.arange(0, ROT_DIM, 2, dtype=F32) / ROT_DIM))
    ang = positions.astype(F32)[..., None] * inv_freq
    cos, sin = jnp.cos(ang), jnp.sin(ang)
    n = positions.size
    cos, sin = cos.reshape(n, ROT_HALF), sin.reshape(n, ROT_HALF)
    rest = HEAD_DIM - ROT_DIM
    cos_t = jnp.concatenate([cos, cos, jnp.ones((n, rest), F32)], axis=1)
    sin_a = jnp.concatenate([-sin, jnp.zeros((n, rest + ROT_HALF), F32)], axis=1)
    sin_b = jnp.concatenate([jnp.zeros((n, ROT_HALF), F32), sin, jnp.zeros((n, rest), F32)], axis=1)
    rep = LANES // HEAD_DIM
    return tuple(jnp.tile(t, (1, rep)) for t in (cos_t, sin_a, sin_b))


def _tile(n, want):
    t = min(n, want)
    assert n % t == 0, (n, t)
    return t


def kernel(x, p, positions, w_in, diff_lambda, diff_subln_g, w_branch_diff, w_branch_dsa, w_out, ln1_g, ln1_b, w_route_group, b_route_group, w_route_expert, b_route_expert, w_exp_gate, w_exp_up, w_exp_down, w_ple, w_ple_gate, ln2_g, ln2_b):
    batch, seq, d = x.shape
    depth = w_in.shape[0]
    n = batch * seq
    alpha = (2 * depth) ** 0.25
    topk = min(TOPK_MAX, seq // 4)
    d_expert = w_exp_gate.shape[-1]
    assert BRANCH_WIDTH == DIFF_HEADS * 2 * HEAD_DIM == DSA_HEADS * HEAD_DIM == IDX_HEADS * HEAD_DIM
    n_qkv = 7 * BRANCH_WIDTH
    o_ik, o_iw = n_qkv, n_qkv + HEAD_DIM
    o_ga = o_iw + IDX_HEADS
    assert w_in.shape[2] == o_ga + 2 * d

    tm_proj = _tile(n, 512)
    tq_diff = _tile(seq, 1024)
    tq_select = _tile(seq, 512)
    tk_dsa = _tile(seq, 1024)
    tm_merge = _tile(n, 256)
    tm_moe = _tile(n, 1024)

    cos_t, sin_a, sin_b = _rope_tables(positions)
    scale = HEAD_DIM ** -0.5
    col_scale = jnp.ones((n_qkv,), F32)
    for seg in (0, 3, 6):
        col_scale = col_scale.at[seg * BRANCH_WIDTH:(seg + 1) * BRANCH_WIDTH].set(scale)

    x2 = x.reshape(n, d)
    for i in range(depth):
        lam_init = 0.8 - 0.6 * math.exp(-0.3 * i)
        wi = w_in[i]
        w_ik = wi[:, o_ik:o_ik + HEAD_DIM]
        w_iw = jnp.pad(wi[:, o_iw:o_iw + IDX_HEADS], ((0, 0), (0, LANES - IDX_HEADS)))
        w_a = jnp.concatenate([wi[:, :n_qkv] * col_scale, w_ik, w_ik, w_iw], axis=1).astype(BF16)
        w_g = wi[:, o_ga:].astype(BF16)
        w_r = jnp.pad(jnp.concatenate([w_route_expert[i], w_route_group[i]], axis=1),
                      ((0, 0), (0, LANES - N_EXPERTS - N_GROUPS))).astype(BF16)
        b_r = jnp.pad(jnp.concatenate([b_route_expert[i], b_route_group[i]]),
                      (0, LANES - N_EXPERTS - N_GROUPS)).reshape(1, LANES)
        w_gu = jnp.concatenate([w_exp_gate[i], w_exp_up[i]], axis=-1).reshape(
            N_EXPERTS, d, 2 * d_expert).astype(BF16)
        w_d = w_exp_down[i].reshape(N_EXPERTS * d_expert, d).astype(BF16)

        dq, dk, dv, sq, sk, sv, iq, ik2, iw = _inproj(x2, w_a, cos_t, sin_a, sin_b, tm_proj)
        ya = _diff_attn(dq, dk, dv, diff_lambda[i], diff_subln_g[i], batch, seq, tq_diff, lam_init)
        bias = _select(iq, ik2, iw, batch, seq, tq_select, tk_dsa, topk)
        yb = _sparse_attn(sq, sk, sv, bias, batch, seq, tk_dsa, tk_dsa)
        x1b, res, comb = _merge(
            x2, ya, yb, p[i].reshape(n, -1), w_g,
            w_branch_diff[i].astype(BF16), w_branch_dsa[i].astype(BF16), w_out[i].astype(BF16),
            ln1_g[i].reshape(1, d), ln1_b[i].reshape(1, d), w_r, b_r,
            w_ple_gate[i].astype(BF16), w_ple[i].astype(BF16), tm_merge, alpha)
        x2 = _moe(x1b, res, comb, w_gu, w_d, ln2_g[i].reshape(1, d), ln2_b[i].reshape(1, d), tm_moe,
                   MOE_EXPERTS_PER_STEP)
    return x2.reshape(batch, seq, d)
```

```python
import functools
import math

import jax
import jax.numpy as jnp
import numpy as np
from jax import lax
from jax.experimental import pallas as pl
from jax.experimental.pallas import tpu as pltpu

F32 = jnp.float32
BF16 = jnp.bfloat16

HEAD_DIM = 64
DIFF_HEADS = 4
DSA_HEADS = 8
IDX_HEADS = 8
TOPK_MAX = 256
N_GROUPS = 4
EXPERTS_PER_GROUP = 8
N_EXPERTS = N_GROUPS * EXPERTS_PER_GROUP
MOE_EXPERTS_PER_STEP = 8
ROPE_THETA = 500000.0
ROT_DIM = HEAD_DIM // 4
ROT_HALF = ROT_DIM // 2
LN_EPS = 1e-5
NEG_INF = -1e30
LOG2E = math.log2(math.e)
LANES = 128
BRANCH_WIDTH = 512
VMEM_LIMIT = 62 * 1024 * 1024


SIGN_BIT = np.int32(-2 ** 31)
MAGNITUDE_BITS = np.int32(2 ** 31 - 1)
SUBNORMAL_BITS = np.int32(2 ** 23 - 1)


def _float_key(v):
    bits = int(np.float32(v).view(np.int32))
    mag = max((bits & int(MAGNITUDE_BITS)) - int(SUBNORMAL_BITS), 0)
    return -mag if bits < 0 else mag


KEY_LO_INIT = _float_key(NEG_INF)
VALUE_STEPS = 24
TIGHTEN_START = 13
TIGHTEN_EVERY = 4
SCORE_CHUNK = 512


def _dot_nt(a, b):
    return lax.dot_general(a, b, (((1,), (1,)), ((), ())), preferred_element_type=F32)


def _dot(a, b):
    return jnp.dot(a, b, preferred_element_type=F32)


def _sigmoid(v):
    return 1.0 / (1.0 + jnp.exp(-v))


def _layer_norm(v, g, b):
    mu = jnp.mean(v, axis=-1, keepdims=True)
    d = v - mu
    var = jnp.mean(d * d, axis=-1, keepdims=True)
    return d * lax.rsqrt(var + LN_EPS) * g + b


def _inproj_kernel(x_ref, w_ref, c_ref, sa_ref, sb_ref,
                   dq_ref, dk_ref, dv_ref, sq_ref, sk_ref, sv_ref, iq_ref, ik_ref, iw_ref):
    xb = x_ref[...].astype(BF16)
    cos_t = c_ref[...]
    sin_a = sa_ref[...]
    sin_b = sb_ref[...]

    def rope(h):
        cols = []
        for c in range(h.shape[1] // LANES):
            hc = h[:, c * LANES:(c + 1) * LANES]
            cols.append(hc * cos_t
                        + pltpu.roll(hc, LANES - ROT_HALF, 1) * sin_a
                        + pltpu.roll(hc, ROT_HALF, 1) * sin_b)
        return cols[0] if len(cols) == 1 else jnp.concatenate(cols, axis=1)

    outs = ((dq_ref, True, LOG2E), (dk_ref, True, None), (dv_ref, False, None), (sq_ref, True, LOG2E),
            (sk_ref, True, None), (sv_ref, False, None), (iq_ref, True, None))
    for i, (ref, rotary, mult) in enumerate(outs):
        h = _dot(xb, w_ref[:, i * BRANCH_WIDTH:(i + 1) * BRANCH_WIDTH])
        if mult is not None:
            h = h * mult
        if rotary:
            h = rope(h)
        ref[...] = h.astype(BF16)
    base = len(outs) * BRANCH_WIDTH
    h = _dot(xb, w_ref[:, base:base + 2 * LANES])
    ik_ref[...] = rope(h[:, :LANES]).astype(BF16)
    iw_ref[...] = h[:, LANES:] * (IDX_HEADS ** -0.5)


def _inproj(x2, w_a, cos_t, sin_a, sin_b, tm):
    n, d = x2.shape
    wc = w_a.shape[1]
    tok = lambda width: pl.BlockSpec((tm, width), lambda i: (i, 0))
    out_shape = [jax.ShapeDtypeStruct((n, BRANCH_WIDTH), BF16)] * 7 + [
        jax.ShapeDtypeStruct((n, LANES), BF16), jax.ShapeDtypeStruct((n, LANES), F32)]
    return pl.pallas_call(
        _inproj_kernel,
        grid=(n // tm,),
        in_specs=[tok(d), pl.BlockSpec((d, wc), lambda i: (0, 0)), tok(LANES), tok(LANES), tok(LANES)],
        out_specs=[tok(BRANCH_WIDTH)] * 7 + [tok(LANES), tok(LANES)],
        out_shape=out_shape,
        compiler_params=pltpu.CompilerParams(dimension_semantics=("arbitrary",),
                                             vmem_limit_bytes=VMEM_LIMIT),
        name="inproj",
    )(x2, w_a, cos_t, sin_a, sin_b)


def _lane_cols(s):
    return [s[:, c * LANES:(c + 1) * LANES] for c in range(s.shape[1] // LANES)]


def _softmax_step(cols, vc, m_ref, l_ref, idx):
    mx = functools.reduce(jnp.maximum, cols)
    m_prev = m_ref[idx]
    m_new = jnp.maximum(m_prev, jnp.max(mx, axis=-1, keepdims=True))
    alpha = jnp.exp2(m_prev - m_new)
    ps = [jnp.exp2(c - m_new) for c in cols]
    l_ref[idx] = alpha * l_ref[idx] + functools.reduce(jnp.add, ps)
    m_ref[idx] = m_new
    p = ps[0] if len(ps) == 1 else jnp.concatenate(ps, axis=1)
    return alpha, _dot(p.astype(BF16), vc)


def _diff_kernel(q_ref, k_ref, v_ref, lam_ref, g_ref, o_ref, qm_ref, m_ref, l_ref, acc_ref, *,
                 tq, lam_init):
    qi = pl.program_id(2)
    q = q_ref[...]
    lane = lax.broadcasted_iota(jnp.int32, q.shape, 1)
    zero = jnp.zeros_like(q)
    qm_ref[0] = jnp.where(lane < HEAD_DIM, q, zero)
    qm_ref[1] = jnp.where(lane >= HEAD_DIM, q, zero)
    m_ref[...] = jnp.full(m_ref.shape, NEG_INF, F32)
    l_ref[...] = jnp.zeros(l_ref.shape, F32)
    acc_ref[...] = jnp.zeros(acc_ref.shape, F32)

    def tile(rows, start, n_keys, causal):
        kc = k_ref[pl.ds(start, n_keys), :]
        vc = v_ref[pl.ds(start, n_keys), :]
        for c in range(2):
            s = _dot_nt(qm_ref[c, rows], kc)
            if causal:
                row = qi * tq + rows.start + lax.broadcasted_iota(jnp.int32, s.shape, 0)
                col = start + lax.broadcasted_iota(jnp.int32, s.shape, 1)
                s = jnp.where(col <= row, s, NEG_INF)
            alpha, pv = _softmax_step(_lane_cols(s), vc, m_ref, l_ref, (c, rows))
            acc_ref[c, rows] = alpha * acc_ref[c, rows] + pv

    def body(j, carry):
        tile(slice(0, tq), pl.multiple_of(j * tq, tq), tq, False)
        return carry

    lax.fori_loop(0, qi, body, 0)
    diag = pl.multiple_of(qi * tq, tq)
    half = tq // 2
    tile(slice(0, half), diag, half, True)
    tile(slice(half, tq), diag, tq, True)

    lp = lam_ref[...]
    lam = (jnp.exp(jnp.sum(lp[0:1] * lp[1:2], axis=-1, keepdims=True))
           - jnp.exp(jnp.sum(lp[2:3] * lp[3:4], axis=-1, keepdims=True)) + lam_init)
    l0 = jnp.sum(l_ref[0], axis=-1, keepdims=True)
    l1 = jnp.sum(l_ref[1], axis=-1, keepdims=True)
    o = acc_ref[0] / l0 - lam * (acc_ref[1] / l1)
    o = o * lax.rsqrt(jnp.mean(o * o, axis=-1, keepdims=True) + LN_EPS)
    o_ref[...] = (o * g_ref[...] * (1.0 - lam_init)).astype(BF16)


def _diff_attn(dq, dk, dv, lam_params, subln_g, batch, seq, tq, lam_init):
    assert seq % tq == 0 and tq % (2 * LANES) == 0
    nq = seq // tq
    hw = 2 * HEAD_DIM
    kv_spec = pl.BlockSpec((seq, hw), lambda b, h, i: (b, h))
    q_spec = pl.BlockSpec((tq, hw), lambda b, h, i: (b * nq + i, h))
    return pl.pallas_call(
        functools.partial(_diff_kernel, tq=tq, lam_init=lam_init),
        grid=(batch, DIFF_HEADS, nq),
        in_specs=[q_spec, kv_spec, kv_spec,
                  pl.BlockSpec(lam_params.shape, lambda b, h, i: (0, 0)),
                  pl.BlockSpec((1, hw), lambda b, h, i: (0, 0))],
        out_specs=q_spec,
        out_shape=jax.ShapeDtypeStruct(dq.shape, BF16),
        scratch_shapes=[pltpu.VMEM((2, tq, hw), BF16), pltpu.VMEM((2, tq, LANES), F32),
                        pltpu.VMEM((2, tq, LANES), F32), pltpu.VMEM((2, tq, hw), F32)],
        compiler_params=pltpu.CompilerParams(
            dimension_semantics=("arbitrary", "arbitrary", "arbitrary"),
            vmem_limit_bytes=VMEM_LIMIT),
        name="diff_attn",
    )(dq, dk, dv, lam_params, subln_g.reshape(1, hw))


def _key_to_float(key):
    mag = jnp.abs(key)
    bits = jnp.where(mag > 0, mag + SUBNORMAL_BITS, 0)
    return lax.bitcast_convert_type(jnp.where(key < 0, bits | SIGN_BIT, bits), F32)


def _float_to_key(v):
    bits = lax.bitcast_convert_type(v, jnp.int32)
    mag = jnp.maximum((bits & MAGNITUDE_BITS) - SUBNORMAL_BITS, 0)
    return jnp.where(bits < 0, -mag, mag)


def _causal_steps(seq, tq, tk):
    pairs = [(i, g) for i in range(seq // tq) for g in range(-(-(i + 1) * tq // tk))]
    return (jnp.asarray([p[0] for p in pairs], jnp.int32), jnp.asarray([p[1] for p in pairs], jnp.int32))


def _select_kernel(it_ref, gt_ref, iq_ref, iw_ref, ik_ref, tri_ref, o_ref,
                   sc_ref, iqm_ref, wb_ref, thr_ref, val_ref, need_ref, seen_ref, flag_ref, *,
                   tq, tk, topk):
    t = pl.program_id(1)
    qi = it_ref[t]
    g = gt_ref[t]
    ck = SCORE_CHUNK
    cps = tk // ck
    bands = [slice(b * LANES, (b + 1) * LANES) for b in range(tq // LANES)]
    tile = (LANES, LANES)

    def rows_of(vec):
        return jnp.broadcast_to(vec, tile).T

    def per_row(mat, reduce):
        return reduce(mat.T, axis=0, keepdims=True)

    @pl.when(g == 0)
    def _search():
        lane = lax.broadcasted_iota(jnp.int32, (tq, LANES), 1)
        low_half = lane < HEAD_DIM
        nch = (qi + 1) * tq // ck
        n_plain = qi * tq // ck

        iw = iw_ref[...]
        for h in range(IDX_HEADS):
            pair = h // 2
            keep = low_half if h % 2 == 0 else jnp.logical_not(low_half)
            iq_pair = iq_ref[:, pair * LANES:(pair + 1) * LANES]
            iqm_ref[h] = jnp.where(keep, iq_pair, jnp.zeros_like(iq_pair))
            wb_ref[h] = jnp.broadcast_to(iw[:, h:h + 1], (tq, LANES))

        def score_chunk(j, masked, top2):
            start = pl.multiple_of(j * ck, ck)
            kc = ik_ref[pl.ds(start, ck), :]
            parts = [jnp.zeros((tq, LANES), F32) for _ in range(ck // LANES)]
            for h in range(IDX_HEADS):
                d = _dot_nt(iqm_ref[h], kc)
                w = wb_ref[h]
                for c in range(ck // LANES):
                    parts[c] = parts[c] + w * jnp.maximum(d[:, c * LANES:(c + 1) * LANES], 0.0)
            sc = jnp.concatenate(parts, axis=1) + 0.0
            if masked:
                row = qi * tq + lax.broadcasted_iota(jnp.int32, (tq, ck), 0)
                col = start + lax.broadcasted_iota(jnp.int32, (tq, ck), 1)
                sc = jnp.where(col <= row, sc, NEG_INF)
            sc_ref[j] = sc
            max1, max2 = top2
            for part in _lane_cols(sc):
                max2 = jnp.maximum(max2, jnp.minimum(max1, part))
                max1 = jnp.maximum(max1, part)
            return max1, max2

        neg = jnp.full((tq, LANES), NEG_INF, F32)
        top2 = lax.fori_loop(0, n_plain, lambda j, s: score_chunk(j, False, s), (neg, neg))
        max1, max2 = lax.fori_loop(n_plain, nch, lambda j, s: score_chunk(j, True, s), top2)

        n_steps = (nch + cps - 1) // cps

        def fill(j, carry):
            sc_ref[j] = jnp.full((tq, ck), NEG_INF, F32)
            return carry
        lax.fori_loop(nch, n_steps * cps, fill, 0)

        lane_row = lax.broadcasted_iota(jnp.int32, (1, LANES), 1)

        def sweep(b, fn, init):
            def body(j, acc):
                for c, part in enumerate(_lane_cols(sc_ref[j, bands[b], :])):
                    acc = fn(acc, part, j, c)
                return acc
            return lax.fori_loop(0, nch, body, init)

        def sweep_if(active, b, fn, init):
            if active is None:
                return sweep(b, fn, init)
            return lax.cond(active[b] > 0.0, lambda: sweep(b, fn, init), lambda: init)

        def count_above(thrs, active=None):
            thr_bs = [rows_of(v) for v in thrs]
            cnts = [sweep_if(active, b, lambda acc, part, j, c, v=v: acc + jnp.where(part > v, 1.0, 0.0),
                             jnp.zeros(tile, F32)) for b, v in enumerate(thr_bs)]
            return [per_row(c, jnp.sum) for c in cnts]

        def band_extent(lo_fs, hi_fs, active):
            edges = [(rows_of(lo_f), rows_of(hi_f)) for lo_f, hi_f in zip(lo_fs, hi_fs)]

            def fn(acc, part, j, c, lo_b, hi_b):
                top, bot = acc
                return (jnp.maximum(top, jnp.where(part <= hi_b, part, -jnp.inf)),
                        jnp.minimum(bot, jnp.where(part > lo_b, part, jnp.inf)))
            init = (jnp.full(tile, -jnp.inf, F32), jnp.full(tile, jnp.inf, F32))
            ext = [sweep_if(active, b, functools.partial(fn, lo_b=lo_b, hi_b=hi_b), init)
                   for b, (lo_b, hi_b) in enumerate(edges)]
            return [(per_row(top, jnp.max), per_row(bot, jnp.min)) for top, bot in ext]

        kks = [jnp.minimum(qi * tq + b * LANES + lane_row + 1, topk).astype(F32) for b in range(len(bands))]

        def unresolved(b, lo, hi, flo):
            return jnp.logical_and(flo != kks[b], hi - 1 > lo)

        def any_set(flags):
            return functools.reduce(jnp.maximum, [jnp.max(jnp.where(f, 1.0, 0.0)) for f in flags])

        def active_bands(state):
            return tuple(jnp.max(jnp.where(unresolved(b, s[0], s[1], s[2]), 1.0, 0.0))
                         for b, s in enumerate(state))

        def midpoint(it, lo, hi):
            mid_key = (lo >> 1) + (hi >> 1) + (lo & hi & 1)
            mid_val = _float_to_key(0.5 * _key_to_float(lo) + 0.5 * _key_to_float(hi))
            by_value = jnp.logical_and(jnp.logical_and(mid_val > lo, mid_val < hi), it < VALUE_STEPS)
            return jnp.where(by_value, mid_val, mid_key)

        def bisect(it, active, state):
            mids = [midpoint(it, lo, hi) for lo, hi, _, _ in state]
            cnts = count_above([_key_to_float(mid) for mid in mids], active)
            out = []
            for b, (lo, hi, flo, fhi) in enumerate(state):
                mid, cnt = mids[b], cnts[b]
                less = cnt < kks[b]
                up = jnp.logical_and(active[b] > 0.0, jnp.logical_not(less))
                down = jnp.logical_and(active[b] > 0.0, less)
                out.append((jnp.where(up, mid, lo), jnp.where(down, mid, hi),
                            jnp.where(up, cnt, flo), jnp.where(down, cnt, fhi)))
            return tuple(out)

        def tighten(active, state):
            ext = band_extent([_key_to_float(s[0]) for s in state], [_key_to_float(s[1]) for s in state],
                              active)
            out = []
            for b, (lo, hi, flo, fhi) in enumerate(state):
                top, bot = ext[b]
                todo = jnp.logical_and(active[b] > 0.0, unresolved(b, lo, hi, flo))
                new_lo = jnp.maximum(_float_to_key(bot) - 1, lo)
                new_hi = jnp.minimum(_float_to_key(top), hi)
                out.append((jnp.where(todo, new_lo, lo), jnp.where(todo, new_hi, hi), flo, fhi))
            return tuple(out)

        def search_body(st):
            it, active, state = st
            do_tighten = jnp.logical_and(it >= TIGHTEN_START, (it - TIGHTEN_START) % TIGHTEN_EVERY == 0)
            state = lax.cond(do_tighten, functools.partial(tighten, active),
                             functools.partial(bisect, it, active), state)
            return it + 1, active_bands(state), state

        hi0s = [_float_to_key(per_row(max1[rows], jnp.max)) for rows in bands]
        lo0s = [jnp.maximum(_float_to_key(per_row(max2[rows], jnp.min)) - 1, KEY_LO_INIT) for rows in bands]
        flo0s = count_above([_key_to_float(lo0) for lo0 in lo0s])
        state0 = tuple((lo0s[b], hi0s[b], flo0s[b], jnp.zeros((1, LANES), F32)) for b in range(len(bands)))
        _, _, state = lax.while_loop(lambda st: functools.reduce(jnp.maximum, st[1]) > 0.0, search_body,
                                     (jnp.int32(0), active_bands(state0), state0))

        tie_rows = [s[2] > kks[b] for b, s in enumerate(state)]
        stack_rows = lambda vecs: jnp.concatenate([rows_of(v) for v in vecs], axis=0)
        thr_ref[...] = stack_rows([_key_to_float(jnp.where(tie_rows[b], s[1], s[0])) for b, s in enumerate(state)])
        val_ref[...] = stack_rows([_key_to_float(s[1]) for s in state])
        need_ref[...] = stack_rows([jnp.where(tie_rows[b], kks[b] - s[3], 0.0) for b, s in enumerate(state)])
        seen_ref[...] = jnp.zeros(seen_ref.shape, F32)
        for b, tied_rows in enumerate(tie_rows):
            flag_ref[b] = (any_set([tied_rows]) > 0.0).astype(jnp.int32)

    def write_band(rows, with_ties):
        def run(_):
            thr_b = thr_ref[rows]
            seen = seen_ref[rows] if with_ties else None
            for u in range(cps):
                blk = sc_ref[g * cps + u, rows, :]
                for c, part in enumerate(_lane_cols(blk)):
                    hit = part > thr_b
                    if with_ties:
                        tied = part == val_ref[rows]
                        counts = _dot(jnp.where(tied, 1.0, 0.0).astype(BF16), tri_ref[...])
                        rank = seen + counts[:, :LANES]
                        hit = jnp.logical_or(hit, jnp.logical_and(tied, rank <= need_ref[rows]))
                        seen = seen + counts[:, LANES:]
                    o_ref[rows, u * ck + c * LANES:u * ck + (c + 1) * LANES] = jnp.where(hit, 0.0, NEG_INF)
            if with_ties:
                seen_ref[rows] = seen
            return 0
        return run

    for b, rows in enumerate(bands):
        lax.cond(flag_ref[b] > 0, write_band(rows, True), write_band(rows, False), 0)


def _select(iq, ik2, iw, batch, seq, tq, tk, topk):
    ck = SCORE_CHUNK
    assert tq % LANES == 0 and tq % ck == 0 and tk % ck == 0 and seq % tk == 0 and topk <= 2 * LANES
    nq = seq // tq
    i_tab, g_tab = _causal_steps(seq, tq, tk)
    k_idx = jnp.arange(LANES)
    tri = jnp.concatenate([(k_idx[:, None] <= k_idx[None, :]).astype(BF16),
                           jnp.ones((LANES, LANES), BF16)], axis=1)
    q_spec = lambda width: pl.BlockSpec((tq, width), lambda b, t, it, gt: (b * nq + it[t], 0))
    stat = pltpu.VMEM((tq, LANES), F32)
    return pl.pallas_call(
        functools.partial(_select_kernel, tq=tq, tk=tk, topk=topk),
        grid_spec=pltpu.PrefetchScalarGridSpec(
            num_scalar_prefetch=2,
            grid=(batch, i_tab.shape[0]),
            in_specs=[q_spec(BRANCH_WIDTH), q_spec(LANES),
                      pl.BlockSpec((seq, LANES), lambda b, t, it, gt: (b, 0), pipeline_mode=pl.Buffered(1)),
                      pl.BlockSpec(tri.shape, lambda b, t, it, gt: (0, 0))],
            out_specs=pl.BlockSpec((tq, tk), lambda b, t, it, gt: (b * nq + it[t], gt[t])),
            scratch_shapes=[pltpu.VMEM((seq // ck, tq, ck), F32),
                            pltpu.VMEM((IDX_HEADS, tq, LANES), BF16),
                            pltpu.VMEM((IDX_HEADS, tq, LANES), F32),
                            stat, stat, stat, stat,
                            pltpu.SMEM((tq // LANES,), jnp.int32)]),
        out_shape=jax.ShapeDtypeStruct((batch * seq, seq), F32),
        compiler_params=pltpu.CompilerParams(
            dimension_semantics=("arbitrary", "arbitrary"),
            vmem_limit_bytes=VMEM_LIMIT),
        name="select",
    )(i_tab, g_tab, iq, iw, ik2, tri)


def _sparse_attn_kernel(it_ref, gt_ref, q_ref, k_ref, v_ref, bias_ref, o_ref,
                        qm_ref, m_ref, l_ref, acc_ref, *, tq, tk):
    t = pl.program_id(1)
    qi = it_ref[t]
    g = gt_ref[t]
    lane = lax.broadcasted_iota(jnp.int32, (tq, LANES), 1)
    low_half = lane < HEAD_DIM

    @pl.when(g == 0)
    def _():
        for h in range(DSA_HEADS):
            pair = h // 2
            keep = low_half if h % 2 == 0 else jnp.logical_not(low_half)
            q_pair = q_ref[:, pair * LANES:(pair + 1) * LANES]
            qm_ref[h] = jnp.where(keep, q_pair, jnp.zeros_like(q_pair))
        m_ref[...] = jnp.full(m_ref.shape, NEG_INF, F32)
        l_ref[...] = jnp.zeros(l_ref.shape, F32)
        acc_ref[...] = jnp.zeros(acc_ref.shape, F32)

    def attend(rows, n_keys):
        low = lax.broadcasted_iota(jnp.int32, (rows.stop - rows.start, LANES), 1) < HEAD_DIM
        for pair in range(DSA_HEADS // 2):
            kc = k_ref[0:n_keys, pair * LANES:(pair + 1) * LANES]
            vc = v_ref[0:n_keys, pair * LANES:(pair + 1) * LANES]
            alphas, pvs = [], []
            for h in (2 * pair, 2 * pair + 1):
                s = _dot_nt(qm_ref[h, rows], kc)
                cols = [a + b for a, b in zip(_lane_cols(s), _lane_cols(bias_ref[rows, 0:n_keys]))]
                alpha, pv = _softmax_step(cols, vc, m_ref, l_ref, (h, rows))
                alphas.append(alpha)
                pvs.append(pv)
            alpha_pair = jnp.where(low, alphas[0], alphas[1])
            pv_pair = jnp.where(low, pvs[0], pvs[1])
            acc_ref[pair, rows] = alpha_pair * acc_ref[pair, rows] + pv_pair

    @pl.when(g < qi)
    def _():
        attend(slice(0, tq), tk)

    @pl.when(g == qi)
    def _():
        half = tq // 2
        attend(slice(0, half), half)
        attend(slice(half, tq), tk)
        for pair in range(DSA_HEADS // 2):
            l_pair = jnp.where(low_half, jnp.sum(l_ref[2 * pair], axis=-1, keepdims=True),
                               jnp.sum(l_ref[2 * pair + 1], axis=-1, keepdims=True))
            o_ref[:, pair * LANES:(pair + 1) * LANES] = (acc_ref[pair] / l_pair).astype(BF16)


def _sparse_attn(sq, sk, sv, bias, batch, seq, tq, tk):
    assert tq == tk and seq % tk == 0
    nq, nk = seq // tq, seq // tk
    i_tab, g_tab = _causal_steps(seq, tq, tk)
    q_spec = pl.BlockSpec((tq, BRANCH_WIDTH), lambda b, t, it, gt: (b * nq + it[t], 0))
    kv_spec = pl.BlockSpec((tk, BRANCH_WIDTH), lambda b, t, it, gt: (b * nk + gt[t], 0))
    return pl.pallas_call(
        functools.partial(_sparse_attn_kernel, tq=tq, tk=tk),
        grid_spec=pltpu.PrefetchScalarGridSpec(
            num_scalar_prefetch=2,
            grid=(batch, i_tab.shape[0]),
            in_specs=[q_spec, kv_spec, kv_spec,
                      pl.BlockSpec((tq, tk), lambda b, t, it, gt: (b * nq + it[t], gt[t]))],
            out_specs=q_spec,
            scratch_shapes=[pltpu.VMEM((DSA_HEADS, tq, LANES), BF16),
                            pltpu.VMEM((DSA_HEADS, tq, LANES), F32),
                            pltpu.VMEM((DSA_HEADS, tq, LANES), F32),
                            pltpu.VMEM((DSA_HEADS // 2, tq, LANES), F32)]),
        out_shape=jax.ShapeDtypeStruct(sq.shape, BF16),
        compiler_params=pltpu.CompilerParams(
            dimension_semantics=("arbitrary", "arbitrary"),
            vmem_limit_bytes=VMEM_LIMIT),
        name="sparse_attn",
    )(i_tab, g_tab, sq, sk, sv, bias)


def _merge_kernel(x_ref, ya_ref, yb_ref, p_ref, wg_ref, wbd_ref, wbs_ref, wo_ref, g1_ref, b1_ref,
                  wr_ref, br_ref, wpg_ref, wp_ref, x1_ref, res_ref, comb_ref, *, alpha, d_model):
    x = x_ref[...]
    gates = _dot(x.astype(BF16), wg_ref[...])
    merged = (_sigmoid(gates[:, :d_model]) * _dot(ya_ref[...], wbd_ref[...])
              + _sigmoid(gates[:, d_model:]) * _dot(yb_ref[...], wbs_ref[...]))
    mix = _dot(merged.astype(BF16), wo_ref[...])
    x1 = _layer_norm(alpha * x + mix, g1_ref[...], b1_ref[...])
    x1b = x1.astype(BF16)
    x1_ref[...] = x1b
    ple = _sigmoid(_dot(x1b, wpg_ref[...])) * _dot(p_ref[...].astype(BF16), wp_ref[...])
    res_ref[...] = alpha * x1 + ple

    logits = _dot(x1b, wr_ref[...]) + br_ref[...]
    lane = lax.broadcasted_iota(jnp.int32, logits.shape, 1)
    is_group = jnp.logical_and(lane >= N_EXPERTS, lane < N_EXPERTS + N_GROUPS)
    gl = jnp.where(is_group, logits, NEG_INF)
    gmax = jnp.max(gl, axis=-1, keepdims=True)
    gsum = jnp.sum(jnp.where(is_group, jnp.exp(gl - gmax), 0.0), axis=-1, keepdims=True)
    g_val = 1.0 / gsum
    g_idx = jnp.min(jnp.where(jnp.logical_and(is_group, gl == gmax), lane, 4 * LANES),
                    axis=-1, keepdims=True) - N_EXPERTS
    first = g_idx * EXPERTS_PER_GROUP
    in_group = jnp.logical_and(lane >= first, lane < first + EXPERTS_PER_GROUP)
    el = jnp.where(in_group, logits, NEG_INF)
    e1 = jnp.max(el, axis=-1, keepdims=True)
    i1 = jnp.min(jnp.where(jnp.logical_and(in_group, el == e1), lane, 4 * LANES), axis=-1, keepdims=True)
    el2 = jnp.where(lane == i1, NEG_INF, el)
    e2 = jnp.max(el2, axis=-1, keepdims=True)
    i2 = jnp.min(jnp.where(jnp.logical_and(in_group, el2 == e2), lane, 4 * LANES), axis=-1, keepdims=True)
    p2 = jnp.exp(e2 - e1)
    w1 = 1.0 / (1.0 + p2)
    w2 = p2 / (1.0 + p2)
    comb_ref[...] = g_val * (jnp.where(lane == i1, w1, 0.0) + jnp.where(lane == i2, w2, 0.0))


def _merge(x2, ya, yb, p2, w_g, w_bd, w_bs, w_o, g1, b1, w_r, b_r, w_pg, w_p, tm, alpha):
    n, d = x2.shape
    tok = lambda width: pl.BlockSpec((tm, width), lambda i: (i, 0))
    full = lambda a: pl.BlockSpec(a.shape, lambda i: (0, 0))
    weights = (w_g, w_bd, w_bs, w_o, g1, b1, w_r, b_r, w_pg, w_p)
    return pl.pallas_call(
        functools.partial(_merge_kernel, alpha=alpha, d_model=d),
        grid=(n // tm,),
        in_specs=[tok(d), tok(BRANCH_WIDTH), tok(BRANCH_WIDTH), tok(p2.shape[1])] + [full(w) for w in weights],
        out_specs=[tok(d), tok(d), tok(LANES)],
        out_shape=[jax.ShapeDtypeStruct((n, d), BF16), jax.ShapeDtypeStruct((n, d), F32),
                   jax.ShapeDtypeStruct((n, LANES), F32)],
        compiler_params=pltpu.CompilerParams(dimension_semantics=("arbitrary",),
                                             vmem_limit_bytes=VMEM_LIMIT),
        name="merge",
    )(x2, ya, yb, p2, *weights)


def _moe_kernel(x1_ref, res_ref, comb_ref, wgu_ref, wd_ref, g2_ref, b2_ref, o_ref, h_ref, acc_ref, *,
                d_expert, eps):
    s = pl.program_id(1)
    x = x1_ref[...]
    comb = comb_ref[...]
    lane = lax.broadcasted_iota(jnp.int32, comb.shape, 1)
    for e in range(eps):
        gu = _dot(x, wgu_ref[e])
        ce = jnp.sum(jnp.where(lane == s * eps + e, comb, 0.0), axis=-1, keepdims=True)
        g = gu[:, :d_expert]
        h = g * _sigmoid(g) * gu[:, d_expert:] * ce
        h_ref[:, e * d_expert:(e + 1) * d_expert] = h.astype(BF16)
    y = _dot(h_ref[...], wd_ref[...])

    @pl.when(s == 0)
    def _():
        acc_ref[...] = y

    @pl.when(s > 0)
    def _():
        acc_ref[...] += y

    @pl.when(s == pl.num_programs(1) - 1)
    def _():
        o_ref[...] = _layer_norm(res_ref[...] + acc_ref[...], g2_ref[...], b2_ref[...])


def _moe(x1b, res, comb, w_gu, w_d, g2, b2, tm, eps):
    n, d = x1b.shape
    n_exp, _, two_f = w_gu.shape
    d_expert = two_f // 2
    assert n_exp % eps == 0 and w_d.shape == (n_exp * d_expert, d)
    tok = lambda width: pl.BlockSpec((tm, width), lambda i, s: (i, 0))
    return pl.pallas_call(
        functools.partial(_moe_kernel, d_expert=d_expert, eps=eps),
        grid=(n // tm, n_exp // eps),
        in_specs=[tok(d), tok(d), tok(LANES),
                  pl.BlockSpec((eps, d, two_f), lambda i, s: (s, 0, 0)),
                  pl.BlockSpec((eps * d_expert, d), lambda i, s: (s, 0)),
                  pl.BlockSpec((1, d), lambda i, s: (0, 0)),
                  pl.BlockSpec((1, d), lambda i, s: (0, 0))],
        out_specs=tok(d),
        out_shape=jax.ShapeDtypeStruct((n, d), F32),
        scratch_shapes=[pltpu.VMEM((tm, eps * d_expert), BF16), pltpu.VMEM((tm, d), F32)],
        compiler_params=pltpu.CompilerParams(dimension_semantics=("arbitrary", "arbitrary"),
                                             vmem_limit_bytes=VMEM_LIMIT),
        name="moe",
    )(x1b, res, comb, w_gu, w_d, g2, b2)


def _rope_tables(positions):
    inv_freq = 1.0 / (ROPE_THETA ** (jnp.arange(0, ROT_DIM, 2, dtype=F32) / ROT_DIM))
    ang = positions.astype(F32)[..., None] * inv_freq
    cos, sin = jnp.cos(ang), jnp.sin(ang)
    n = positions.size
    cos, sin = cos.reshape(n, ROT_HALF), sin.reshape(n, ROT_HALF)
    rest = HEAD_DIM - ROT_DIM
    cos_t = jnp.concatenate([cos, cos, jnp.ones((n, rest), F32)], axis=1)
    sin_a = jnp.concatenate([-sin, jnp.zeros((n, rest + ROT_HALF), F32)], axis=1)
    sin_b = jnp.concatenate([jnp.zeros((n, ROT_HALF), F32), sin, jnp.zeros((n, rest), F32)], axis=1)
    rep = LANES // HEAD_DIM
    return tuple(jnp.tile(t, (1, rep)) for t in (cos_t, sin_a, sin_b))


def _tile(n, want):
    t = min(n, want)
    assert n % t == 0, (n, t)
    return t


def kernel(x, p, positions, w_in, diff_lambda, diff_subln_g, w_branch_diff, w_branch_dsa, w_out, ln1_g, ln1_b, w_route_group, b_route_group, w_route_expert, b_route_expert, w_exp_gate, w_exp_up, w_exp_down, w_ple, w_ple_gate, ln2_g, ln2_b):
    batch, seq, d = x.shape
    depth = w_in.shape[0]
    n = batch * seq
    alpha = (2 * depth) ** 0.25
    topk = min(TOPK_MAX, seq // 4)
    d_expert = w_exp_gate.shape[-1]
    assert BRANCH_WIDTH == DIFF_HEADS * 2 * HEAD_DIM == DSA_HEADS * HEAD_DIM == IDX_HEADS * HEAD_DIM
    n_qkv = 7 * BRANCH_WIDTH
    o_ik, o_iw = n_qkv, n_qkv + HEAD_DIM
    o_ga = o_iw + IDX_HEADS
    assert w_in.shape[2] == o_ga + 2 * d

    tm_proj = _tile(n, 512)
    tq_diff = _tile(seq, 1024)
    tq_select = _tile(seq, 512)
    tk_dsa = _tile(seq, 1024)
    tm_merge = _tile(n, 256)
    tm_moe = _tile(n, 1024)

    cos_t, sin_a, sin_b = _rope_tables(positions)
    scale = HEAD_DIM ** -0.5
    col_scale = jnp.ones((n_qkv,), F32)
    for seg in (0, 3, 6):
        col_scale = col_scale.at[seg * BRANCH_WIDTH:(seg + 1) * BRANCH_WIDTH].set(scale)

    x2 = x.reshape(n, d)
    for i in range(depth):
        lam_init = 0.8 - 0.6 * math.exp(-0.3 * i)
        wi = w_in[i]
        w_ik = wi[:, o_ik:o_ik + HEAD_DIM]
        w_iw = jnp.pad(wi[:, o_iw:o_iw + IDX_HEADS], ((0, 0), (0, LANES - IDX_HEADS)))
        w_a = jnp.concatenate([wi[:, :n_qkv] * col_scale, w_ik, w_ik, w_iw], axis=1).astype(BF16)
        w_g = wi[:, o_ga:].astype(BF16)
        w_r = jnp.pad(jnp.concatenate([w_route_expert[i], w_route_group[i]], axis=1),
                      ((0, 0), (0, LANES - N_EXPERTS - N_GROUPS))).astype(BF16)
        b_r = jnp.pad(jnp.concatenate([b_route_expert[i], b_route_group[i]]),
                      (0, LANES - N_EXPERTS - N_GROUPS)).reshape(1, LANES)
        w_gu = jnp.concatenate([w_exp_gate[i], w_exp_up[i]], axis=-1).reshape(
            N_EXPERTS, d, 2 * d_expert).astype(BF16)
        w_d = w_exp_down[i].reshape(N_EXPERTS * d_expert, d).astype(BF16)

        dq, dk, dv, sq, sk, sv, iq, ik2, iw = _inproj(x2, w_a, cos_t, sin_a, sin_b, tm_proj)
        ya = _diff_attn(dq, dk, dv, diff_lambda[i], diff_subln_g[i], batch, seq, tq_diff, lam_init)
        bias = _select(iq, ik2, iw, batch, seq, tq_select, tk_dsa, topk)
        yb = _sparse_attn(sq, sk, sv, bias, batch, seq, tk_dsa, tk_dsa)
        x1b, res, comb = _merge(
            x2, ya, yb, p[i].reshape(n, -1), w_g,
            w_branch_diff[i].astype(BF16), w_branch_dsa[i].astype(BF16), w_out[i].astype(BF16),
            ln1_g[i].reshape(1, d), ln1_b[i].reshape(1, d), w_r, b_r,
            w_ple_gate[i].astype(BF16), w_ple[i].astype(BF16), tm_merge, alpha)
        x2 = _moe(x1b, res, comb, w_gu, w_d, ln2_g[i].reshape(1, d), ln2_b[i].reshape(1, d), tm_moe,
                   MOE_EXPERTS_PER_STEP)
    return x2.reshape(batch, seq, d)
```

```python
import functools
import math

import jax
import jax.numpy as jnp
import numpy as np
from jax import lax
from jax.experimental import pallas as pl
from jax.experimental.pallas import tpu as pltpu

F32 = jnp.float32
BF16 = jnp.bfloat16

HEAD_DIM = 64
DIFF_HEADS = 4
DSA_HEADS = 8
IDX_HEADS = 8
TOPK_MAX = 256
N_GROUPS = 4
EXPERTS_PER_GROUP = 8
N_EXPERTS = N_GROUPS * EXPERTS_PER_GROUP
MOE_EXPERTS_PER_STEP = 8
MERGE_SUBTILES = 2
ROPE_THETA = 500000.0
ROT_DIM = HEAD_DIM // 4
ROT_HALF = ROT_DIM // 2
LN_EPS = 1e-5
NEG_INF = -1e30
LOG2E = math.log2(math.e)
LANES = 128
BRANCH_WIDTH = 512
VMEM_LIMIT = 62 * 1024 * 1024


SIGN_BIT = np.int32(-2 ** 31)
MAGNITUDE_BITS = np.int32(2 ** 31 - 1)
SUBNORMAL_BITS = np.int32(2 ** 23 - 1)


def _float_key(v):
    bits = int(np.float32(v).view(np.int32))
    mag = max((bits & int(MAGNITUDE_BITS)) - int(SUBNORMAL_BITS), 0)
    return -mag if bits < 0 else mag


KEY_LO_INIT = _float_key(NEG_INF)
VALUE_STEPS = 24
TIGHTEN_START = 13
TIGHTEN_EVERY = 4
SCORE_CHUNK = 512


def _dot_nt(a, b):
    return lax.dot_general(a, b, (((1,), (1,)), ((), ())), preferred_element_type=F32)


def _dot(a, b):
    return jnp.dot(a, b, preferred_element_type=F32)


def _sigmoid(v):
    return 1.0 / (1.0 + jnp.exp(-v))


def _layer_norm(v, g, b):
    mu = jnp.mean(v, axis=-1, keepdims=True)
    d = v - mu
    var = jnp.mean(d * d, axis=-1, keepdims=True)
    return d * lax.rsqrt(var + LN_EPS) * g + b


def _inproj_kernel(x_ref, w_ref, c_ref, sa_ref, sb_ref,
                   dq_ref, dk_ref, dv_ref, sq_ref, sk_ref, sv_ref, iq_ref, ik_ref, iw_ref):
    xb = x_ref[...].astype(BF16)
    cos_t = c_ref[...]
    sin_a = sa_ref[...]
    sin_b = sb_ref[...]

    def rope(h):
        cols = []
        for c in range(h.shape[1] // LANES):
            hc = h[:, c * LANES:(c + 1) * LANES]
            cols.append(hc * cos_t
                        + pltpu.roll(hc, LANES - ROT_HALF, 1) * sin_a
                        + pltpu.roll(hc, ROT_HALF, 1) * sin_b)
        return cols[0] if len(cols) == 1 else jnp.concatenate(cols, axis=1)

    outs = ((dq_ref, True, LOG2E), (dk_ref, True, None), (dv_ref, False, None), (sq_ref, True, LOG2E),
            (sk_ref, True, None), (sv_ref, False, None), (iq_ref, True, None))
    for i, (ref, rotary, mult) in enumerate(outs):
        h = _dot(xb, w_ref[:, i * BRANCH_WIDTH:(i + 1) * BRANCH_WIDTH])
        if mult is not None:
            h = h * mult
        if rotary:
            h = rope(h)
        ref[...] = h.astype(BF16)
    base = len(outs) * BRANCH_WIDTH
    h = _dot(xb, w_ref[:, base:base + 2 * LANES])
    ik_ref[...] = rope(h[:, :LANES]).astype(BF16)
    iw_ref[...] = h[:, LANES:] * (IDX_HEADS ** -0.5)


def _inproj(x2, w_a, cos_t, sin_a, sin_b, tm):
    n, d = x2.shape
    wc = w_a.shape[1]
    tok = lambda width: pl.BlockSpec((tm, width), lambda i: (i, 0))
    out_shape = [jax.ShapeDtypeStruct((n, BRANCH_WIDTH), BF16)] * 7 + [
        jax.ShapeDtypeStruct((n, LANES), BF16), jax.ShapeDtypeStruct((n, LANES), F32)]
    return pl.pallas_call(
        _inproj_kernel,
        grid=(n // tm,),
        in_specs=[tok(d), pl.BlockSpec((d, wc), lambda i: (0, 0)), tok(LANES), tok(LANES), tok(LANES)],
        out_specs=[tok(BRANCH_WIDTH)] * 7 + [tok(LANES), tok(LANES)],
        out_shape=out_shape,
        compiler_params=pltpu.CompilerParams(dimension_semantics=("arbitrary",),
                                             vmem_limit_bytes=VMEM_LIMIT),
        name="inproj",
    )(x2, w_a, cos_t, sin_a, sin_b)


def _lane_cols(s):
    return [s[:, c * LANES:(c + 1) * LANES] for c in range(s.shape[1] // LANES)]


def _softmax_step(cols, vc, m_ref, l_ref, idx):
    mx = functools.reduce(jnp.maximum, cols)
    m_prev = m_ref[idx]
    m_new = jnp.maximum(m_prev, jnp.max(mx, axis=-1, keepdims=True))
    alpha = jnp.exp2(m_prev - m_new)
    ps = [jnp.exp2(c - m_new) for c in cols]
    l_ref[idx] = alpha * l_ref[idx] + functools.reduce(jnp.add, ps)
    m_ref[idx] = m_new
    p = ps[0] if len(ps) == 1 else jnp.concatenate(ps, axis=1)
    return alpha, _dot(p.astype(BF16), vc)


def _diff_kernel(q_ref, k_ref, v_ref, lam_ref, g_ref, o_ref, qm_ref, m_ref, l_ref, acc_ref, *,
                 tq, lam_init):
    qi = pl.program_id(2)
    q = q_ref[...]
    lane = lax.broadcasted_iota(jnp.int32, q.shape, 1)
    zero = jnp.zeros_like(q)
    qm_ref[0] = jnp.where(lane < HEAD_DIM, q, zero)
    qm_ref[1] = jnp.where(lane >= HEAD_DIM, q, zero)
    m_ref[...] = jnp.full(m_ref.shape, NEG_INF, F32)
    l_ref[...] = jnp.zeros(l_ref.shape, F32)
    acc_ref[...] = jnp.zeros(acc_ref.shape, F32)

    def tile(rows, start, n_keys, causal):
        kc = k_ref[pl.ds(start, n_keys), :]
        vc = v_ref[pl.ds(start, n_keys), :]
        for c in range(2):
            s = _dot_nt(qm_ref[c, rows], kc)
            if causal:
                row = qi * tq + rows.start + lax.broadcasted_iota(jnp.int32, s.shape, 0)
                col = start + lax.broadcasted_iota(jnp.int32, s.shape, 1)
                s = jnp.where(col <= row, s, NEG_INF)
            alpha, pv = _softmax_step(_lane_cols(s), vc, m_ref, l_ref, (c, rows))
            acc_ref[c, rows] = alpha * acc_ref[c, rows] + pv

    def body(j, carry):
        tile(slice(0, tq), pl.multiple_of(j * tq, tq), tq, False)
        return carry

    lax.fori_loop(0, qi, body, 0)
    diag = pl.multiple_of(qi * tq, tq)
    half = tq // 2
    tile(slice(0, half), diag, half, True)
    tile(slice(half, tq), diag, tq, True)

    lp = lam_ref[...]
    lam = (jnp.exp(jnp.sum(lp[0:1] * lp[1:2], axis=-1, keepdims=True))
           - jnp.exp(jnp.sum(lp[2:3] * lp[3:4], axis=-1, keepdims=True)) + lam_init)
    l0 = jnp.sum(l_ref[0], axis=-1, keepdims=True)
    l1 = jnp.sum(l_ref[1], axis=-1, keepdims=True)
    o = acc_ref[0] / l0 - lam * (acc_ref[1] / l1)
    o = o * lax.rsqrt(jnp.mean(o * o, axis=-1, keepdims=True) + LN_EPS)
    o_ref[...] = (o * g_ref[...] * (1.0 - lam_init)).astype(BF16)


def _diff_attn(dq, dk, dv, lam_params, subln_g, batch, seq, tq, lam_init):
    assert seq % tq == 0 and tq % (2 * LANES) == 0
    nq = seq // tq
    hw = 2 * HEAD_DIM
    kv_spec = pl.BlockSpec((seq, hw), lambda b, h, i: (b, h))
    q_spec = pl.BlockSpec((tq, hw), lambda b, h, i: (b * nq + i, h))
    return pl.pallas_call(
        functools.partial(_diff_kernel, tq=tq, lam_init=lam_init),
        grid=(batch, DIFF_HEADS, nq),
        in_specs=[q_spec, kv_spec, kv_spec,
                  pl.BlockSpec(lam_params.shape, lambda b, h, i: (0, 0)),
                  pl.BlockSpec((1, hw), lambda b, h, i: (0, 0))],
        out_specs=q_spec,
        out_shape=jax.ShapeDtypeStruct(dq.shape, BF16),
        scratch_shapes=[pltpu.VMEM((2, tq, hw), BF16), pltpu.VMEM((2, tq, LANES), F32),
                        pltpu.VMEM((2, tq, LANES), F32), pltpu.VMEM((2, tq, hw), F32)],
        compiler_params=pltpu.CompilerParams(
            dimension_semantics=("arbitrary", "arbitrary", "arbitrary"),
            vmem_limit_bytes=VMEM_LIMIT),
        name="diff_attn",
    )(dq, dk, dv, lam_params, subln_g.reshape(1, hw))


def _key_to_float(key):
    mag = jnp.abs(key)
    bits = jnp.where(mag > 0, mag + SUBNORMAL_BITS, 0)
    return lax.bitcast_convert_type(jnp.where(key < 0, bits | SIGN_BIT, bits), F32)


def _float_to_key(v):
    bits = lax.bitcast_convert_type(v, jnp.int32)
    mag = jnp.maximum((bits & MAGNITUDE_BITS) - SUBNORMAL_BITS, 0)
    return jnp.where(bits < 0, -mag, mag)


def _causal_steps(seq, tq, tk):
    pairs = [(i, g) for i in range(seq // tq) for g in range(-(-(i + 1) * tq // tk))]
    return (jnp.asarray([p[0] for p in pairs], jnp.int32), jnp.asarray([p[1] for p in pairs], jnp.int32))


def _select_kernel(it_ref, gt_ref, iq_ref, iw_ref, ik_ref, tri_ref, o_ref,
                   sc_ref, iqm_ref, wb_ref, thr_ref, val_ref, need_ref, seen_ref, flag_ref, *,
                   tq, tk, topk):
    t = pl.program_id(1)
    qi = it_ref[t]
    g = gt_ref[t]
    ck = SCORE_CHUNK
    cps = tk // ck
    bands = [slice(b * LANES, (b + 1) * LANES) for b in range(tq // LANES)]
    tile = (LANES, LANES)

    def rows_of(vec):
        return jnp.broadcast_to(vec, tile).T

    def per_row(mat, reduce):
        return reduce(mat.T, axis=0, keepdims=True)

    @pl.when(g == 0)
    def _search():
        lane = lax.broadcasted_iota(jnp.int32, (tq, LANES), 1)
        low_half = lane < HEAD_DIM
        nch = (qi + 1) * tq // ck
        n_plain = qi * tq // ck

        iw = iw_ref[...]
        for h in range(IDX_HEADS):
            pair = h // 2
            keep = low_half if h % 2 == 0 else jnp.logical_not(low_half)
            iq_pair = iq_ref[:, pair * LANES:(pair + 1) * LANES]
            iqm_ref[h] = jnp.where(keep, iq_pair, jnp.zeros_like(iq_pair))
            wb_ref[h] = jnp.broadcast_to(iw[:, h:h + 1], (tq, LANES))

        def score_chunk(j, masked, top2):
            start = pl.multiple_of(j * ck, ck)
            kc = ik_ref[pl.ds(start, ck), :]
            parts = [jnp.zeros((tq, LANES), F32) for _ in range(ck // LANES)]
            for h in range(IDX_HEADS):
                d = _dot_nt(iqm_ref[h], kc)
                w = wb_ref[h]
                for c in range(ck // LANES):
                    parts[c] = parts[c] + w * jnp.maximum(d[:, c * LANES:(c + 1) * LANES], 0.0)
            sc = jnp.concatenate(parts, axis=1) + 0.0
            if masked:
                row = qi * tq + lax.broadcasted_iota(jnp.int32, (tq, ck), 0)
                col = start + lax.broadcasted_iota(jnp.int32, (tq, ck), 1)
                sc = jnp.where(col <= row, sc, NEG_INF)
            sc_ref[j] = sc
            max1, max2 = top2
            for part in _lane_cols(sc):
                max2 = jnp.maximum(max2, jnp.minimum(max1, part))
                max1 = jnp.maximum(max1, part)
            return max1, max2

        neg = jnp.full((tq, LANES), NEG_INF, F32)
        top2 = lax.fori_loop(0, n_plain, lambda j, s: score_chunk(j, False, s), (neg, neg))
        max1, max2 = lax.fori_loop(n_plain, nch, lambda j, s: score_chunk(j, True, s), top2)

        n_steps = (nch + cps - 1) // cps

        def fill(j, carry):
            sc_ref[j] = jnp.full((tq, ck), NEG_INF, F32)
            return carry
        lax.fori_loop(nch, n_steps * cps, fill, 0)

        lane_row = lax.broadcasted_iota(jnp.int32, (1, LANES), 1)

        def sweep(b, fn, init):
            def body(j, acc):
                for c, part in enumerate(_lane_cols(sc_ref[j, bands[b], :])):
                    acc = fn(acc, part, j, c)
                return acc
            return lax.fori_loop(0, nch, body, init)

        def sweep_if(active, b, fn, init):
            if active is None:
                return sweep(b, fn, init)
            return lax.cond(active[b] > 0.0, lambda: sweep(b, fn, init), lambda: init)

        def count_above(thrs, active=None):
            thr_bs = [rows_of(v) for v in thrs]
            cnts = [sweep_if(active, b, lambda acc, part, j, c, v=v: acc + jnp.where(part > v, 1.0, 0.0),
                             jnp.zeros(tile, F32)) for b, v in enumerate(thr_bs)]
            return [per_row(c, jnp.sum) for c in cnts]

        def band_extent(lo_fs, hi_fs, active):
            edges = [(rows_of(lo_f), rows_of(hi_f)) for lo_f, hi_f in zip(lo_fs, hi_fs)]

            def fn(acc, part, j, c, lo_b, hi_b):
                top, bot = acc
                return (jnp.maximum(top, jnp.where(part <= hi_b, part, -jnp.inf)),
                        jnp.minimum(bot, jnp.where(part > lo_b, part, jnp.inf)))
            init = (jnp.full(tile, -jnp.inf, F32), jnp.full(tile, jnp.inf, F32))
            ext = [sweep_if(active, b, functools.partial(fn, lo_b=lo_b, hi_b=hi_b), init)
                   for b, (lo_b, hi_b) in enumerate(edges)]
            return [(per_row(top, jnp.max), per_row(bot, jnp.min)) for top, bot in ext]

        kks = [jnp.minimum(qi * tq + b * LANES + lane_row + 1, topk).astype(F32) for b in range(len(bands))]

        def unresolved(b, lo, hi, flo):
            return jnp.logical_and(flo != kks[b], hi - 1 > lo)

        def any_set(flags):
            return functools.reduce(jnp.maximum, [jnp.max(jnp.where(f, 1.0, 0.0)) for f in flags])

        def active_bands(state):
            return tuple(jnp.max(jnp.where(unresolved(b, s[0], s[1], s[2]), 1.0, 0.0))
                         for b, s in enumerate(state))

        def midpoint(it, lo, hi):
            mid_key = (lo >> 1) + (hi >> 1) + (lo & hi & 1)
            mid_val = _float_to_key(0.5 * _key_to_float(lo) + 0.5 * _key_to_float(hi))
            by_value = jnp.logical_and(jnp.logical_and(mid_val > lo, mid_val < hi), it < VALUE_STEPS)
            return jnp.where(by_value, mid_val, mid_key)

        def bisect(it, active, state):
            mids = [midpoint(it, lo, hi) for lo, hi, _, _ in state]
            cnts = count_above([_key_to_float(mid) for mid in mids], active)
            out = []
            for b, (lo, hi, flo, fhi) in enumerate(state):
                mid, cnt = mids[b], cnts[b]
                less = cnt < kks[b]
                up = jnp.logical_and(active[b] > 0.0, jnp.logical_not(less))
                down = jnp.logical_and(active[b] > 0.0, less)
                out.append((jnp.where(up, mid, lo), jnp.where(down, mid, hi),
                            jnp.where(up, cnt, flo), jnp.where(down, cnt, fhi)))
            return tuple(out)

        def tighten(active, state):
            ext = band_extent([_key_to_float(s[0]) for s in state], [_key_to_float(s[1]) for s in state],
                              active)
            out = []
            for b, (lo, hi, flo, fhi) in enumerate(state):
                top, bot = ext[b]
                todo = jnp.logical_and(active[b] > 0.0, unresolved(b, lo, hi, flo))
                new_lo = jnp.maximum(_float_to_key(bot) - 1, lo)
                new_hi = jnp.minimum(_float_to_key(top), hi)
                out.append((jnp.where(todo, new_lo, lo), jnp.where(todo, new_hi, hi), flo, fhi))
            return tuple(out)

        def search_body(st):
            it, active, state = st
            do_tighten = jnp.logical_and(it >= TIGHTEN_START, (it - TIGHTEN_START) % TIGHTEN_EVERY == 0)
            state = lax.cond(do_tighten, functools.partial(tighten, active),
                             functools.partial(bisect, it, active), state)
            return it + 1, active_bands(state), state

        hi0s = [_float_to_key(per_row(max1[rows], jnp.max)) for rows in bands]
        lo0s = [jnp.maximum(_float_to_key(per_row(max2[rows], jnp.min)) - 1, KEY_LO_INIT) for rows in bands]
        flo0s = count_above([_key_to_float(lo0) for lo0 in lo0s])
        state0 = tuple((lo0s[b], hi0s[b], flo0s[b], jnp.zeros((1, LANES), F32)) for b in range(len(bands)))
        _, _, state = lax.while_loop(lambda st: functools.reduce(jnp.maximum, st[1]) > 0.0, search_body,
                                     (jnp.int32(0), active_bands(state0), state0))

        tie_rows = [s[2] > kks[b] for b, s in enumerate(state)]
        stack_rows = lambda vecs: jnp.concatenate([rows_of(v) for v in vecs], axis=0)
        thr_ref[...] = stack_rows([_key_to_float(jnp.where(tie_rows[b], s[1], s[0])) for b, s in enumerate(state)])
        val_ref[...] = stack_rows([_key_to_float(s[1]) for s in state])
        need_ref[...] = stack_rows([jnp.where(tie_rows[b], kks[b] - s[3], 0.0) for b, s in enumerate(state)])
        seen_ref[...] = jnp.zeros(seen_ref.shape, F32)
        for b, tied_rows in enumerate(tie_rows):
            flag_ref[b] = (any_set([tied_rows]) > 0.0).astype(jnp.int32)

    def write_band(rows, with_ties):
        def run(_):
            thr_b = thr_ref[rows]
            seen = seen_ref[rows] if with_ties else None
            for u in range(cps):
                blk = sc_ref[g * cps + u, rows, :]
                for c, part in enumerate(_lane_cols(blk)):
                    hit = part > thr_b
                    if with_ties:
                        tied = part == val_ref[rows]
                        counts = _dot(jnp.where(tied, 1.0, 0.0).astype(BF16), tri_ref[...])
                        rank = seen + counts[:, :LANES]
                        hit = jnp.logical_or(hit, jnp.logical_and(tied, rank <= need_ref[rows]))
                        seen = seen + counts[:, LANES:]
                    o_ref[rows, u * ck + c * LANES:u * ck + (c + 1) * LANES] = jnp.where(hit, 0.0, NEG_INF)
            if with_ties:
                seen_ref[rows] = seen
            return 0
        return run

    for b, rows in enumerate(bands):
        lax.cond(flag_ref[b] > 0, write_band(rows, True), write_band(rows, False), 0)


def _select(iq, ik2, iw, batch, seq, tq, tk, topk):
    ck = SCORE_CHUNK
    assert tq % LANES == 0 and tq % ck == 0 and tk % ck == 0 and seq % tk == 0 and topk <= 2 * LANES
    nq = seq // tq
    i_tab, g_tab = _causal_steps(seq, tq, tk)
    k_idx = jnp.arange(LANES)
    tri = jnp.concatenate([(k_idx[:, None] <= k_idx[None, :]).astype(BF16),
                           jnp.ones((LANES, LANES), BF16)], axis=1)
    q_spec = lambda width: pl.BlockSpec((tq, width), lambda b, t, it, gt: (b * nq + it[t], 0))
    stat = pltpu.VMEM((tq, LANES), F32)
    return pl.pallas_call(
        functools.partial(_select_kernel, tq=tq, tk=tk, topk=topk),
        grid_spec=pltpu.PrefetchScalarGridSpec(
            num_scalar_prefetch=2,
            grid=(batch, i_tab.shape[0]),
            in_specs=[q_spec(BRANCH_WIDTH), q_spec(LANES),
                      pl.BlockSpec((seq, LANES), lambda b, t, it, gt: (b, 0), pipeline_mode=pl.Buffered(1)),
                      pl.BlockSpec(tri.shape, lambda b, t, it, gt: (0, 0))],
            out_specs=pl.BlockSpec((tq, tk), lambda b, t, it, gt: (b * nq + it[t], gt[t])),
            scratch_shapes=[pltpu.VMEM((seq // ck, tq, ck), F32),
                            pltpu.VMEM((IDX_HEADS, tq, LANES), BF16),
                            pltpu.VMEM((IDX_HEADS, tq, LANES), F32),
                            stat, stat, stat, stat,
                            pltpu.SMEM((tq // LANES,), jnp.int32)]),
        out_shape=jax.ShapeDtypeStruct((batch * seq, seq), F32),
        compiler_params=pltpu.CompilerParams(
            dimension_semantics=("arbitrary", "arbitrary"),
            vmem_limit_bytes=VMEM_LIMIT),
        name="select",
    )(i_tab, g_tab, iq, iw, ik2, tri)


def _sparse_attn_kernel(it_ref, gt_ref, q_ref, k_ref, v_ref, bias_ref, o_ref,
                        qm_ref, m_ref, l_ref, acc_ref, *, tq, tk):
    t = pl.program_id(1)
    qi = it_ref[t]
    g = gt_ref[t]
    lane = lax.broadcasted_iota(jnp.int32, (tq, LANES), 1)
    low_half = lane < HEAD_DIM

    @pl.when(g == 0)
    def _():
        for h in range(DSA_HEADS):
            pair = h // 2
            keep = low_half if h % 2 == 0 else jnp.logical_not(low_half)
            q_pair = q_ref[:, pair * LANES:(pair + 1) * LANES]
            qm_ref[h] = jnp.where(keep, q_pair, jnp.zeros_like(q_pair))
        m_ref[...] = jnp.full(m_ref.shape, NEG_INF, F32)
        l_ref[...] = jnp.zeros(l_ref.shape, F32)
        acc_ref[...] = jnp.zeros(acc_ref.shape, F32)

    def attend(rows, n_keys):
        low = lax.broadcasted_iota(jnp.int32, (rows.stop - rows.start, LANES), 1) < HEAD_DIM
        for pair in range(DSA_HEADS // 2):
            kc = k_ref[0:n_keys, pair * LANES:(pair + 1) * LANES]
            vc = v_ref[0:n_keys, pair * LANES:(pair + 1) * LANES]
            alphas, pvs = [], []
            for h in (2 * pair, 2 * pair + 1):
                s = _dot_nt(qm_ref[h, rows], kc)
                cols = [a + b for a, b in zip(_lane_cols(s), _lane_cols(bias_ref[rows, 0:n_keys]))]
                alpha, pv = _softmax_step(cols, vc, m_ref, l_ref, (h, rows))
                alphas.append(alpha)
                pvs.append(pv)
            alpha_pair = jnp.where(low, alphas[0], alphas[1])
            pv_pair = jnp.where(low, pvs[0], pvs[1])
            acc_ref[pair, rows] = alpha_pair * acc_ref[pair, rows] + pv_pair

    @pl.when(g < qi)
    def _():
        attend(slice(0, tq), tk)

    @pl.when(g == qi)
    def _():
        half = tq // 2
        attend(slice(0, half), half)
        attend(slice(half, tq), tk)
        for pair in range(DSA_HEADS // 2):
            l_pair = jnp.where(low_half, jnp.sum(l_ref[2 * pair], axis=-1, keepdims=True),
                               jnp.sum(l_ref[2 * pair + 1], axis=-1, keepdims=True))
            o_ref[:, pair * LANES:(pair + 1) * LANES] = (acc_ref[pair] / l_pair).astype(BF16)


def _sparse_attn(sq, sk, sv, bias, batch, seq, tq, tk):
    assert tq == tk and seq % tk == 0
    nq, nk = seq // tq, seq // tk
    i_tab, g_tab = _causal_steps(seq, tq, tk)
    q_spec = pl.BlockSpec((tq, BRANCH_WIDTH), lambda b, t, it, gt: (b * nq + it[t], 0))
    kv_spec = pl.BlockSpec((tk, BRANCH_WIDTH), lambda b, t, it, gt: (b * nk + gt[t], 0))
    return pl.pallas_call(
        functools.partial(_sparse_attn_kernel, tq=tq, tk=tk),
        grid_spec=pltpu.PrefetchScalarGridSpec(
            num_scalar_prefetch=2,
            grid=(batch, i_tab.shape[0]),
            in_specs=[q_spec, kv_spec, kv_spec,
                      pl.BlockSpec((tq, tk), lambda b, t, it, gt: (b * nq + it[t], gt[t]))],
            out_specs=q_spec,
            scratch_shapes=[pltpu.VMEM((DSA_HEADS, tq, LANES), BF16),
                            pltpu.VMEM((DSA_HEADS, tq, LANES), F32),
                            pltpu.VMEM((DSA_HEADS, tq, LANES), F32),
                            pltpu.VMEM((DSA_HEADS // 2, tq, LANES), F32)]),
        out_shape=jax.ShapeDtypeStruct(sq.shape, BF16),
        compiler_params=pltpu.CompilerParams(
            dimension_semantics=("arbitrary", "arbitrary"),
            vmem_limit_bytes=VMEM_LIMIT),
        name="sparse_attn",
    )(i_tab, g_tab, sq, sk, sv, bias)


def _merge_kernel(*refs, alpha, d_model, n_sub):
    tokens, weights, outs = refs[:4], refs[4:-3], refs[-3:]
    sub = tokens[0].shape[0] // n_sub
    for r in range(n_sub):
        rows = pl.ds(r * sub, sub)
        _merge_rows(*[t.at[rows] for t in tokens], *weights, *[o.at[rows] for o in outs],
                    alpha=alpha, d_model=d_model)


def _merge_rows(x_ref, ya_ref, yb_ref, p_ref, wg_ref, wbd_ref, wbs_ref, wo_ref, g1_ref, b1_ref,
                wr_ref, br_ref, wpg_ref, wp_ref, x1_ref, res_ref, comb_ref, *, alpha, d_model):
    x = x_ref[...]
    gates = _dot(x.astype(BF16), wg_ref[...])
    merged = (_sigmoid(gates[:, :d_model]) * _dot(ya_ref[...], wbd_ref[...])
              + _sigmoid(gates[:, d_model:]) * _dot(yb_ref[...], wbs_ref[...]))
    mix = _dot(merged.astype(BF16), wo_ref[...])
    x1 = _layer_norm(alpha * x + mix, g1_ref[...], b1_ref[...])
    x1b = x1.astype(BF16)
    x1_ref[...] = x1b
    ple = _sigmoid(_dot(x1b, wpg_ref[...])) * _dot(p_ref[...].astype(BF16), wp_ref[...])
    res_ref[...] = alpha * x1 + ple

    logits = _dot(x1b, wr_ref[...]) + br_ref[...]
    lane = lax.broadcasted_iota(jnp.int32, logits.shape, 1)
    is_group = jnp.logical_and(lane >= N_EXPERTS, lane < N_EXPERTS + N_GROUPS)
    gl = jnp.where(is_group, logits, NEG_INF)
    gmax = jnp.max(gl, axis=-1, keepdims=True)
    gsum = jnp.sum(jnp.where(is_group, jnp.exp(gl - gmax), 0.0), axis=-1, keepdims=True)
    g_val = 1.0 / gsum
    g_idx = jnp.min(jnp.where(jnp.logical_and(is_group, gl == gmax), lane, 4 * LANES),
                    axis=-1, keepdims=True) - N_EXPERTS
    first = g_idx * EXPERTS_PER_GROUP
    in_group = jnp.logical_and(lane >= first, lane < first + EXPERTS_PER_GROUP)
    el = jnp.where(in_group, logits, NEG_INF)
    e1 = jnp.max(el, axis=-1, keepdims=True)
    i1 = jnp.min(jnp.where(jnp.logical_and(in_group, el == e1), lane, 4 * LANES), axis=-1, keepdims=True)
    el2 = jnp.where(lane == i1, NEG_INF, el)
    e2 = jnp.max(el2, axis=-1, keepdims=True)
    i2 = jnp.min(jnp.where(jnp.logical_and(in_group, el2 == e2), lane, 4 * LANES), axis=-1, keepdims=True)
    p2 = jnp.exp(e2 - e1)
    w1 = 1.0 / (1.0 + p2)
    w2 = p2 / (1.0 + p2)
    comb_ref[...] = g_val * (jnp.where(lane == i1, w1, 0.0) + jnp.where(lane == i2, w2, 0.0))


def _merge(x2, ya, yb, p2, w_g, w_bd, w_bs, w_o, g1, b1, w_r, b_r, w_pg, w_p, tm, alpha):
    n, d = x2.shape
    tok = lambda width: pl.BlockSpec((tm, width), lambda i: (i, 0))
    full = lambda a: pl.BlockSpec(a.shape, lambda i: (0, 0))
    weights = (w_g, w_bd, w_bs, w_o, g1, b1, w_r, b_r, w_pg, w_p)
    return pl.pallas_call(
        functools.partial(_merge_kernel, alpha=alpha, d_model=d, n_sub=MERGE_SUBTILES),
        grid=(n // tm,),
        in_specs=[tok(d), tok(BRANCH_WIDTH), tok(BRANCH_WIDTH), tok(p2.shape[1])] + [full(w) for w in weights],
        out_specs=[tok(d), tok(d), tok(LANES)],
        out_shape=[jax.ShapeDtypeStruct((n, d), BF16), jax.ShapeDtypeStruct((n, d), F32),
                   jax.ShapeDtypeStruct((n, LANES), F32)],
        compiler_params=pltpu.CompilerParams(dimension_semantics=("arbitrary",),
                                             vmem_limit_bytes=VMEM_LIMIT),
        name="merge",
    )(x2, ya, yb, p2, *weights)


def _moe_kernel(x1_ref, res_ref, comb_ref, wgu_ref, wd_ref, g2_ref, b2_ref, o_ref, h_ref, acc_ref, *,
                d_expert, eps):
    s = pl.program_id(1)
    x = x1_ref[...]
    comb = comb_ref[...]
    lane = lax.broadcasted_iota(jnp.int32, comb.shape, 1)
    for e in range(eps):
        gu = _dot(x, wgu_ref[e])
        ce = jnp.sum(jnp.where(lane == s * eps + e, comb, 0.0), axis=-1, keepdims=True)
        g = gu[:, :d_expert]
        h = g * _sigmoid(g) * gu[:, d_expert:] * ce
        h_ref[:, e * d_expert:(e + 1) * d_expert] = h.astype(BF16)
    y = _dot(h_ref[...], wd_ref[...])

    @pl.when(s == 0)
    def _():
        acc_ref[...] = y

    @pl.when(s > 0)
    def _():
        acc_ref[...] += y

    @pl.when(s == pl.num_programs(1) - 1)
    def _():
        o_ref[...] = _layer_norm(res_ref[...] + acc_ref[...], g2_ref[...], b2_ref[...])


def _moe(x1b, res, comb, w_gu, w_d, g2, b2, tm, eps):
    n, d = x1b.shape
    n_exp, _, two_f = w_gu.shape
    d_expert = two_f // 2
    assert n_exp % eps == 0 and w_d.shape == (n_exp * d_expert, d)
    tok = lambda width: pl.BlockSpec((tm, width), lambda i, s: (i, 0))
    return pl.pallas_call(
        functools.partial(_moe_kernel, d_expert=d_expert, eps=eps),
        grid=(n // tm, n_exp // eps),
        in_specs=[tok(d), tok(d), tok(LANES),
                  pl.BlockSpec((eps, d, two_f), lambda i, s: (s, 0, 0)),
                  pl.BlockSpec((eps * d_expert, d), lambda i, s: (s, 0)),
                  pl.BlockSpec((1, d), lambda i, s: (0, 0)),
                  pl.BlockSpec((1, d), lambda i, s: (0, 0))],
        out_specs=tok(d),
        out_shape=jax.ShapeDtypeStruct((n, d), F32),
        scratch_shapes=[pltpu.VMEM((tm, eps * d_expert), BF16), pltpu.VMEM((tm, d), F32)],
        compiler_params=pltpu.CompilerParams(dimension_semantics=("arbitrary", "arbitrary"),
                                             vmem_limit_bytes=VMEM_LIMIT),
        name="moe",
    )(x1b, res, comb, w_gu, w_d, g2, b2)


def _rope_tables(positions):
    inv_freq = 1.0 / (ROPE_THETA ** (jnp.arange(0, ROT_DIM, 2, dtype=F32) / ROT_DIM))
    ang = positions.astype(F32)[..., None] * inv_freq
    cos, sin = jnp.cos(ang), jnp.sin(ang)
    n = positions.size
    cos, sin = cos.reshape(n, ROT_HALF), sin.reshape(n, ROT_HALF)
    rest = HEAD_DIM - ROT_DIM
    cos_t = jnp.concatenate([cos, cos, jnp.ones((n, rest), F32)], axis=1)
    sin_a = jnp.concatenate([-sin, jnp.zeros((n, rest + ROT_HALF), F32)], axis=1)
    sin_b = jnp.concatenate([jnp.zeros((n, ROT_HALF), F32), sin, jnp.zeros((n, rest), F32)], axis=1)
    rep = LANES // HEAD_DIM
    return tuple(jnp.tile(t, (1, rep)) for t in (cos_t, sin_a, sin_b))


def _tile(n, want):
    t = min(n, want)
    assert n % t == 0, (n, t)
    return t


def kernel(x, p, positions, w_in, diff_lambda, diff_subln_g, w_branch_diff, w_branch_dsa, w_out, ln1_g, ln1_b, w_route_group, b_route_group, w_route_expert, b_route_expert, w_exp_gate, w_exp_up, w_exp_down, w_ple, w_ple_gate, ln2_g, ln2_b):
    batch, seq, d = x.shape
    depth = w_in.shape[0]
    n = batch * seq
    alpha = (2 * depth) ** 0.25
    topk = min(TOPK_MAX, seq // 4)
    d_expert = w_exp_gate.shape[-1]
    assert BRANCH_WIDTH == DIFF_HEADS * 2 * HEAD_DIM == DSA_HEADS * HEAD_DIM == IDX_HEADS * HEAD_DIM
    n_qkv = 7 * BRANCH_WIDTH
    o_ik, o_iw = n_qkv, n_qkv + HEAD_DIM
    o_ga = o_iw + IDX_HEADS
    assert w_in.shape[2] == o_ga + 2 * d

    tm_proj = _tile(n, 512)
    tq_diff = _tile(seq, 1024)
    tq_select = _tile(seq, 512)
    tk_dsa = _tile(seq, 1024)
    tm_merge = _tile(n, 1024)
    tm_moe = _tile(n, 1024)

    cos_t, sin_a, sin_b = _rope_tables(positions)
    scale = HEAD_DIM ** -0.5
    col_scale = jnp.ones((n_qkv,), F32)
    for seg in (0, 3, 6):
        col_scale = col_scale.at[seg * BRANCH_WIDTH:(seg + 1) * BRANCH_WIDTH].set(scale)

    x2 = x.reshape(n, d)
    for i in range(depth):
        lam_init = 0.8 - 0.6 * math.exp(-0.3 * i)
        wi = w_in[i]
        w_ik = wi[:, o_ik:o_ik + HEAD_DIM]
        w_iw = jnp.pad(wi[:, o_iw:o_iw + IDX_HEADS], ((0, 0), (0, LANES - IDX_HEADS)))
        w_a = jnp.concatenate([wi[:, :n_qkv] * col_scale, w_ik, w_ik, w_iw], axis=1).astype(BF16)
        w_g = wi[:, o_ga:].astype(BF16)
        w_r = jnp.pad(jnp.concatenate([w_route_expert[i], w_route_group[i]], axis=1),
                      ((0, 0), (0, LANES - N_EXPERTS - N_GROUPS))).astype(BF16)
        b_r = jnp.pad(jnp.concatenate([b_route_expert[i], b_route_group[i]]),
                      (0, LANES - N_EXPERTS - N_GROUPS)).reshape(1, LANES)
        w_gu = jnp.concatenate([w_exp_gate[i], w_exp_up[i]], axis=-1).reshape(
            N_EXPERTS, d, 2 * d_expert).astype(BF16)
        w_d = w_exp_down[i].reshape(N_EXPERTS * d_expert, d).astype(BF16)

        dq, dk, dv, sq, sk, sv, iq, ik2, iw = _inproj(x2, w_a, cos_t, sin_a, sin_b, tm_proj)
        ya = _diff_attn(dq, dk, dv, diff_lambda[i], diff_subln_g[i], batch, seq, tq_diff, lam_init)
        bias = _select(iq, ik2, iw, batch, seq, tq_select, tk_dsa, topk)
        yb = _sparse_attn(sq, sk, sv, bias, batch, seq, tk_dsa, tk_dsa)
        x1b, res, comb = _merge(
            x2, ya, yb, p[i].reshape(n, -1), w_g,
            w_branch_diff[i].astype(BF16), w_branch_dsa[i].astype(BF16), w_out[i].astype(BF16),
            ln1_g[i].reshape(1, d), ln1_b[i].reshape(1, d), w_r, b_r,
            w_ple_gate[i].astype(BF16), w_ple[i].astype(BF16), tm_merge, alpha)
        x2 = _moe(x1b, res, comb, w_gu, w_d, ln2_g[i].reshape(1, d), ln2_b[i].reshape(1, d), tm_moe,
                   MOE_EXPERTS_PER_STEP)
    return x2.reshape(batch, seq, d)
```

```python
import functools
import math

import jax
import jax.numpy as jnp
import numpy as np
from jax import lax
from jax.experimental import pallas as pl
from jax.experimental.pallas import tpu as pltpu

F32 = jnp.float32
BF16 = jnp.bfloat16

HEAD_DIM = 64
DIFF_HEADS = 4
DSA_HEADS = 8
IDX_HEADS = 8
TOPK_MAX = 256
N_GROUPS = 4
EXPERTS_PER_GROUP = 8
N_EXPERTS = N_GROUPS * EXPERTS_PER_GROUP
MOE_EXPERTS_PER_STEP = 8
MERGE_SUBTILES = 2
ROPE_THETA = 500000.0
ROT_DIM = HEAD_DIM // 4
ROT_HALF = ROT_DIM // 2
LN_EPS = 1e-5
NEG_INF = -1e30
LOG2E = math.log2(math.e)
LANES = 128
BRANCH_WIDTH = 512
VMEM_LIMIT = 62 * 1024 * 1024


SIGN_BIT = np.int32(-2 ** 31)
MAGNITUDE_BITS = np.int32(2 ** 31 - 1)
SUBNORMAL_BITS = np.int32(2 ** 23 - 1)


def _float_key(v):
    bits = int(np.float32(v).view(np.int32))
    mag = max((bits & int(MAGNITUDE_BITS)) - int(SUBNORMAL_BITS), 0)
    return -mag if bits < 0 else mag


KEY_LO_INIT = _float_key(NEG_INF)
VALUE_STEPS = 24
TIGHTEN_START = 13
TIGHTEN_EVERY = 4
SCORE_CHUNK = 512
UNKNOWN_COUNT = float(2 ** 30)


def _dot_nt(a, b):
    return lax.dot_general(a, b, (((1,), (1,)), ((), ())), preferred_element_type=F32)


def _dot(a, b):
    return jnp.dot(a, b, preferred_element_type=F32)


def _sigmoid(v):
    return 1.0 / (1.0 + jnp.exp(-v))


def _layer_norm(v, g, b):
    mu = jnp.mean(v, axis=-1, keepdims=True)
    d = v - mu
    var = jnp.mean(d * d, axis=-1, keepdims=True)
    return d * lax.rsqrt(var + LN_EPS) * g + b


def _inproj_kernel(x_ref, w_ref, c_ref, sa_ref, sb_ref,
                   dq_ref, dk_ref, dv_ref, sq_ref, sk_ref, sv_ref, iq_ref, ik_ref, iw_ref):
    xb = x_ref[...].astype(BF16)
    cos_t = c_ref[...]
    sin_a = sa_ref[...]
    sin_b = sb_ref[...]

    def rope(h):
        cols = []
        for c in range(h.shape[1] // LANES):
            hc = h[:, c * LANES:(c + 1) * LANES]
            cols.append(hc * cos_t
                        + pltpu.roll(hc, LANES - ROT_HALF, 1) * sin_a
                        + pltpu.roll(hc, ROT_HALF, 1) * sin_b)
        return cols[0] if len(cols) == 1 else jnp.concatenate(cols, axis=1)

    outs = ((dq_ref, True, LOG2E), (dk_ref, True, None), (dv_ref, False, None), (sq_ref, True, LOG2E),
            (sk_ref, True, None), (sv_ref, False, None), (iq_ref, True, None))
    for i, (ref, rotary, mult) in enumerate(outs):
        h = _dot(xb, w_ref[:, i * BRANCH_WIDTH:(i + 1) * BRANCH_WIDTH])
        if mult is not None:
            h = h * mult
        if rotary:
            h = rope(h)
        ref[...] = h.astype(BF16)
    base = len(outs) * BRANCH_WIDTH
    h = _dot(xb, w_ref[:, base:base + 2 * LANES])
    ik_ref[...] = rope(h[:, :LANES]).astype(BF16)
    iw_ref[...] = h[:, LANES:] * (IDX_HEADS ** -0.5)


def _inproj(x2, w_a, cos_t, sin_a, sin_b, tm):
    n, d = x2.shape
    wc = w_a.shape[1]
    tok = lambda width: pl.BlockSpec((tm, width), lambda i: (i, 0))
    out_shape = [jax.ShapeDtypeStruct((n, BRANCH_WIDTH), BF16)] * 7 + [
        jax.ShapeDtypeStruct((n, LANES), BF16), jax.ShapeDtypeStruct((n, LANES), F32)]
    return pl.pallas_call(
        _inproj_kernel,
        grid=(n // tm,),
        in_specs=[tok(d), pl.BlockSpec((d, wc), lambda i: (0, 0)), tok(LANES), tok(LANES), tok(LANES)],
        out_specs=[tok(BRANCH_WIDTH)] * 7 + [tok(LANES), tok(LANES)],
        out_shape=out_shape,
        compiler_params=pltpu.CompilerParams(dimension_semantics=("arbitrary",),
                                             vmem_limit_bytes=VMEM_LIMIT),
        name="inproj",
    )(x2, w_a, cos_t, sin_a, sin_b)


def _lane_cols(s):
    return [s[:, c * LANES:(c + 1) * LANES] for c in range(s.shape[1] // LANES)]


def _softmax_step(cols, vc, m_ref, l_ref, idx):
    mx = functools.reduce(jnp.maximum, cols)
    m_prev = m_ref[idx]
    m_new = jnp.maximum(m_prev, jnp.max(mx, axis=-1, keepdims=True))
    alpha = jnp.exp2(m_prev - m_new)
    ps = [jnp.exp2(c - m_new) for c in cols]
    l_ref[idx] = alpha * l_ref[idx] + functools.reduce(jnp.add, ps)
    m_ref[idx] = m_new
    p = ps[0] if len(ps) == 1 else jnp.concatenate(ps, axis=1)
    return alpha, _dot(p.astype(BF16), vc)


def _diff_kernel(q_ref, k_ref, v_ref, lam_ref, g_ref, o_ref, qm_ref, m_ref, l_ref, acc_ref, *,
                 tq, lam_init):
    qi = pl.program_id(2)
    q = q_ref[...]
    lane = lax.broadcasted_iota(jnp.int32, q.shape, 1)
    zero = jnp.zeros_like(q)
    qm_ref[0] = jnp.where(lane < HEAD_DIM, q, zero)
    qm_ref[1] = jnp.where(lane >= HEAD_DIM, q, zero)
    m_ref[...] = jnp.full(m_ref.shape, NEG_INF, F32)
    l_ref[...] = jnp.zeros(l_ref.shape, F32)
    acc_ref[...] = jnp.zeros(acc_ref.shape, F32)

    def tile(rows, start, n_keys, causal):
        kc = k_ref[pl.ds(start, n_keys), :]
        vc = v_ref[pl.ds(start, n_keys), :]
        for c in range(2):
            s = _dot_nt(qm_ref[c, rows], kc)
            if causal:
                row = qi * tq + rows.start + lax.broadcasted_iota(jnp.int32, s.shape, 0)
                col = start + lax.broadcasted_iota(jnp.int32, s.shape, 1)
                s = jnp.where(col <= row, s, NEG_INF)
            alpha, pv = _softmax_step(_lane_cols(s), vc, m_ref, l_ref, (c, rows))
            acc_ref[c, rows] = alpha * acc_ref[c, rows] + pv

    def body(j, carry):
        tile(slice(0, tq), pl.multiple_of(j * tq, tq), tq, False)
        return carry

    lax.fori_loop(0, qi, body, 0)
    diag = pl.multiple_of(qi * tq, tq)
    half = tq // 2
    tile(slice(0, half), diag, half, True)
    tile(slice(half, tq), diag, tq, True)

    lp = lam_ref[...]
    lam = (jnp.exp(jnp.sum(lp[0:1] * lp[1:2], axis=-1, keepdims=True))
           - jnp.exp(jnp.sum(lp[2:3] * lp[3:4], axis=-1, keepdims=True)) + lam_init)
    l0 = jnp.sum(l_ref[0], axis=-1, keepdims=True)
    l1 = jnp.sum(l_ref[1], axis=-1, keepdims=True)
    o = acc_ref[0] / l0 - lam * (acc_ref[1] / l1)
    o = o * lax.rsqrt(jnp.mean(o * o, axis=-1, keepdims=True) + LN_EPS)
    o_ref[...] = (o * g_ref[...] * (1.0 - lam_init)).astype(BF16)


def _diff_attn(dq, dk, dv, lam_params, subln_g, batch, seq, tq, lam_init):
    assert seq % tq == 0 and tq % (2 * LANES) == 0
    nq = seq // tq
    hw = 2 * HEAD_DIM
    kv_spec = pl.BlockSpec((seq, hw), lambda b, h, i: (b, h))
    q_spec = pl.BlockSpec((tq, hw), lambda b, h, i: (b * nq + i, h))
    return pl.pallas_call(
        functools.partial(_diff_kernel, tq=tq, lam_init=lam_init),
        grid=(batch, DIFF_HEADS, nq),
        in_specs=[q_spec, kv_spec, kv_spec,
                  pl.BlockSpec(lam_params.shape, lambda b, h, i: (0, 0)),
                  pl.BlockSpec((1, hw), lambda b, h, i: (0, 0))],
        out_specs=q_spec,
        out_shape=jax.ShapeDtypeStruct(dq.shape, BF16),
        scratch_shapes=[pltpu.VMEM((2, tq, hw), BF16), pltpu.VMEM((2, tq, LANES), F32),
                        pltpu.VMEM((2, tq, LANES), F32), pltpu.VMEM((2, tq, hw), F32)],
        compiler_params=pltpu.CompilerParams(
            dimension_semantics=("arbitrary", "arbitrary", "arbitrary"),
            vmem_limit_bytes=VMEM_LIMIT),
        name="diff_attn",
    )(dq, dk, dv, lam_params, subln_g.reshape(1, hw))


def _key_to_float(key):
    mag = jnp.abs(key)
    bits = jnp.where(mag > 0, mag + SUBNORMAL_BITS, 0)
    return lax.bitcast_convert_type(jnp.where(key < 0, bits | SIGN_BIT, bits), F32)


def _float_to_key(v):
    bits = lax.bitcast_convert_type(v, jnp.int32)
    mag = jnp.maximum((bits & MAGNITUDE_BITS) - SUBNORMAL_BITS, 0)
    return jnp.where(bits < 0, -mag, mag)


def _causal_steps(seq, tq, tk):
    pairs = [(i, g) for i in range(seq // tq) for g in range(-(-(i + 1) * tq // tk))]
    return (jnp.asarray([p[0] for p in pairs], jnp.int32), jnp.asarray([p[1] for p in pairs], jnp.int32))


def _select_kernel(it_ref, gt_ref, iq_ref, iw_ref, ik_ref, tri_ref, o_ref,
                   sc_ref, iqm_ref, wb_ref, thr_ref, val_ref, need_ref, seen_ref, flag_ref, *,
                   tq, tk, topk):
    t = pl.program_id(1)
    qi = it_ref[t]
    g = gt_ref[t]
    ck = SCORE_CHUNK
    cps = tk // ck
    bands = [slice(b * LANES, (b + 1) * LANES) for b in range(tq // LANES)]
    tile = (LANES, LANES)

    def rows_of(vec):
        return jnp.broadcast_to(vec, tile).T

    def per_row(mat, reduce):
        return reduce(mat.T, axis=0, keepdims=True)

    @pl.when(g == 0)
    def _search():
        lane = lax.broadcasted_iota(jnp.int32, (tq, LANES), 1)
        low_half = lane < HEAD_DIM
        nch = (qi + 1) * tq // ck
        n_plain = qi * tq // ck

        iw = iw_ref[...]
        for h in range(IDX_HEADS):
            pair = h // 2
            keep = low_half if h % 2 == 0 else jnp.logical_not(low_half)
            iq_pair = iq_ref[:, pair * LANES:(pair + 1) * LANES]
            iqm_ref[h] = jnp.where(keep, iq_pair, jnp.zeros_like(iq_pair))
            wb_ref[h] = jnp.broadcast_to(iw[:, h:h + 1], (tq, LANES))

        def score_chunk(j, masked, top2):
            start = pl.multiple_of(j * ck, ck)
            kc = ik_ref[pl.ds(start, ck), :]
            parts = [jnp.zeros((tq, LANES), F32) for _ in range(ck // LANES)]
            for h in range(IDX_HEADS):
                d = _dot_nt(iqm_ref[h], kc)
                w = wb_ref[h]
                for c in range(ck // LANES):
                    parts[c] = parts[c] + w * jnp.maximum(d[:, c * LANES:(c + 1) * LANES], 0.0)
            sc = jnp.concatenate(parts, axis=1) + 0.0
            if masked:
                row = qi * tq + lax.broadcasted_iota(jnp.int32, (tq, ck), 0)
                col = start + lax.broadcasted_iota(jnp.int32, (tq, ck), 1)
                sc = jnp.where(col <= row, sc, NEG_INF)
            sc_ref[j] = sc
            max1, max2 = top2
            for part in _lane_cols(sc):
                max2 = jnp.maximum(max2, jnp.minimum(max1, part))
                max1 = jnp.maximum(max1, part)
            return max1, max2

        neg = jnp.full((tq, LANES), NEG_INF, F32)
        top2 = lax.fori_loop(0, n_plain, lambda j, s: score_chunk(j, False, s), (neg, neg))
        max1, max2 = lax.fori_loop(n_plain, nch, lambda j, s: score_chunk(j, True, s), top2)

        n_steps = (nch + cps - 1) // cps

        def fill(j, carry):
            sc_ref[j] = jnp.full((tq, ck), NEG_INF, F32)
            return carry
        lax.fori_loop(nch, n_steps * cps, fill, 0)

        lane_row = lax.broadcasted_iota(jnp.int32, (1, LANES), 1)

        def sweep(b, fn, init):
            def body(j, acc):
                for c, part in enumerate(_lane_cols(sc_ref[j, bands[b], :])):
                    acc = fn(acc, part, j, c)
                return acc
            return lax.fori_loop(0, nch, body, init)

        def sweep_if(active, b, fn, init):
            if active is None:
                return sweep(b, fn, init)
            return lax.cond(active[b] > 0.0, lambda: sweep(b, fn, init), lambda: init)

        def count_above(thrs, active=None):
            thr_bs = [rows_of(v) for v in thrs]
            cnts = [sweep_if(active, b, lambda acc, part, j, c, v=v: acc + jnp.where(part > v, 1.0, 0.0),
                             jnp.zeros(tile, F32)) for b, v in enumerate(thr_bs)]
            return [per_row(c, jnp.sum) for c in cnts]

        def band_extent(lo_fs, hi_fs, active):
            edges = [(rows_of(lo_f), rows_of(hi_f)) for lo_f, hi_f in zip(lo_fs, hi_fs)]

            def fn(acc, part, j, c, lo_b, hi_b):
                top, bot = acc
                return (jnp.maximum(top, jnp.where(part <= hi_b, part, -jnp.inf)),
                        jnp.minimum(bot, jnp.where(part > lo_b, part, jnp.inf)))
            init = (jnp.full(tile, -jnp.inf, F32), jnp.full(tile, jnp.inf, F32))
            ext = [sweep_if(active, b, functools.partial(fn, lo_b=lo_b, hi_b=hi_b), init)
                   for b, (lo_b, hi_b) in enumerate(edges)]
            return [(per_row(top, jnp.max), per_row(bot, jnp.min)) for top, bot in ext]

        kks = [jnp.minimum(qi * tq + b * LANES + lane_row + 1, topk).astype(F32) for b in range(len(bands))]

        def unresolved(b, lo, hi, flo):
            return jnp.logical_and(flo != kks[b], hi - 1 > lo)

        def any_set(flags):
            return functools.reduce(jnp.maximum, [jnp.max(jnp.where(f, 1.0, 0.0)) for f in flags])

        def active_bands(state):
            return tuple(jnp.max(jnp.where(unresolved(b, s[0], s[1], s[2]), 1.0, 0.0))
                         for b, s in enumerate(state))

        def midpoint(it, lo, hi):
            mid_key = (lo >> 1) + (hi >> 1) + (lo & hi & 1)
            mid_val = _float_to_key(0.5 * _key_to_float(lo) + 0.5 * _key_to_float(hi))
            by_value = jnp.logical_and(jnp.logical_and(mid_val > lo, mid_val < hi), it < VALUE_STEPS)
            return jnp.where(by_value, mid_val, mid_key)

        def bisect(it, active, state):
            mids = [midpoint(it, lo, hi) for lo, hi, _, _ in state]
            cnts = count_above([_key_to_float(mid) for mid in mids], active)
            out = []
            for b, (lo, hi, flo, fhi) in enumerate(state):
                mid, cnt = mids[b], cnts[b]
                less = cnt < kks[b]
                up = jnp.logical_and(active[b] > 0.0, jnp.logical_not(less))
                down = jnp.logical_and(active[b] > 0.0, less)
                out.append((jnp.where(up, mid, lo), jnp.where(down, mid, hi),
                            jnp.where(up, cnt, flo), jnp.where(down, cnt, fhi)))
            return tuple(out)

        def tighten(active, state):
            ext = band_extent([_key_to_float(s[0]) for s in state], [_key_to_float(s[1]) for s in state],
                              active)
            out = []
            for b, (lo, hi, flo, fhi) in enumerate(state):
                top, bot = ext[b]
                todo = jnp.logical_and(active[b] > 0.0, unresolved(b, lo, hi, flo))
                new_lo = jnp.maximum(_float_to_key(bot) - 1, lo)
                new_hi = jnp.minimum(_float_to_key(top), hi)
                out.append((jnp.where(todo, new_lo, lo), jnp.where(todo, new_hi, hi), flo, fhi))
            return tuple(out)

        def search_body(st):
            it, active, state = st
            do_tighten = jnp.logical_and(it >= TIGHTEN_START, (it - TIGHTEN_START) % TIGHTEN_EVERY == 0)
            state = lax.cond(do_tighten, functools.partial(tighten, active),
                             functools.partial(bisect, it, active), state)
            return it + 1, active_bands(state), state

        hi0s = [_float_to_key(per_row(max1[rows], jnp.max)) for rows in bands]
        lo0s = [jnp.maximum(_float_to_key(per_row(max2[rows], jnp.min)) - 1, KEY_LO_INIT) for rows in bands]
        flo0s = [jnp.where(qi * tq + b * LANES + lane_row < topk, kks[b], UNKNOWN_COUNT)
                 for b in range(len(bands))]
        state0 = tuple((lo0s[b], hi0s[b], flo0s[b], jnp.zeros((1, LANES), F32)) for b in range(len(bands)))
        _, _, state = lax.while_loop(lambda st: functools.reduce(jnp.maximum, st[1]) > 0.0, search_body,
                                     (jnp.int32(0), active_bands(state0), state0))

        tie_rows = [s[2] > kks[b] for b, s in enumerate(state)]
        stack_rows = lambda vecs: jnp.concatenate([rows_of(v) for v in vecs], axis=0)
        thr_ref[...] = stack_rows([_key_to_float(jnp.where(tie_rows[b], s[1], s[0])) for b, s in enumerate(state)])
        val_ref[...] = stack_rows([_key_to_float(s[1]) for s in state])
        need_ref[...] = stack_rows([jnp.where(tie_rows[b], kks[b] - s[3], 0.0) for b, s in enumerate(state)])
        seen_ref[...] = jnp.zeros(seen_ref.shape, F32)
        for b, tied_rows in enumerate(tie_rows):
            flag_ref[b] = (any_set([tied_rows]) > 0.0).astype(jnp.int32)

    def write_band(rows, with_ties):
        def run(_):
            thr_b = thr_ref[rows]
            seen = seen_ref[rows] if with_ties else None
            for u in range(cps):
                blk = sc_ref[g * cps + u, rows, :]
                for c, part in enumerate(_lane_cols(blk)):
                    hit = part > thr_b
                    if with_ties:
                        tied = part == val_ref[rows]
                        counts = _dot(jnp.where(tied, 1.0, 0.0).astype(BF16), tri_ref[...])
                        rank = seen + counts[:, :LANES]
                        hit = jnp.logical_or(hit, jnp.logical_and(tied, rank <= need_ref[rows]))
                        seen = seen + counts[:, LANES:]
                    o_ref[rows, u * ck + c * LANES:u * ck + (c + 1) * LANES] = jnp.where(hit, 0.0, NEG_INF)
            if with_ties:
                seen_ref[rows] = seen
            return 0
        return run

    for b, rows in enumerate(bands):
        lax.cond(flag_ref[b] > 0, write_band(rows, True), write_band(rows, False), 0)


def _select(iq, ik2, iw, batch, seq, tq, tk, topk):
    ck = SCORE_CHUNK
    assert tq % LANES == 0 and tq % ck == 0 and tk % ck == 0 and seq % tk == 0 and topk <= 2 * LANES
    nq = seq // tq
    i_tab, g_tab = _causal_steps(seq, tq, tk)
    k_idx = jnp.arange(LANES)
    tri = jnp.concatenate([(k_idx[:, None] <= k_idx[None, :]).astype(BF16),
                           jnp.ones((LANES, LANES), BF16)], axis=1)
    q_spec = lambda width: pl.BlockSpec((tq, width), lambda b, t, it, gt: (b * nq + it[t], 0))
    stat = pltpu.VMEM((tq, LANES), F32)
    return pl.pallas_call(
        functools.partial(_select_kernel, tq=tq, tk=tk, topk=topk),
        grid_spec=pltpu.PrefetchScalarGridSpec(
            num_scalar_prefetch=2,
            grid=(batch, i_tab.shape[0]),
            in_specs=[q_spec(BRANCH_WIDTH), q_spec(LANES),
                      pl.BlockSpec((seq, LANES), lambda b, t, it, gt: (b, 0), pipeline_mode=pl.Buffered(1)),
                      pl.BlockSpec(tri.shape, lambda b, t, it, gt: (0, 0))],
            out_specs=pl.BlockSpec((tq, tk), lambda b, t, it, gt: (b * nq + it[t], gt[t])),
            scratch_shapes=[pltpu.VMEM((seq // ck, tq, ck), F32),
                            pltpu.VMEM((IDX_HEADS, tq, LANES), BF16),
                            pltpu.VMEM((IDX_HEADS, tq, LANES), F32),
                            stat, stat, stat, stat,
                            pltpu.SMEM((tq // LANES,), jnp.int32)]),
        out_shape=jax.ShapeDtypeStruct((batch * seq, seq), F32),
        compiler_params=pltpu.CompilerParams(
            dimension_semantics=("arbitrary", "arbitrary"),
            vmem_limit_bytes=VMEM_LIMIT),
        name="select",
    )(i_tab, g_tab, iq, iw, ik2, tri)


def _sparse_attn_kernel(it_ref, gt_ref, q_ref, k_ref, v_ref, bias_ref, o_ref,
                        qm_ref, m_ref, l_ref, acc_ref, *, tq, tk):
    t = pl.program_id(1)
    qi = it_ref[t]
    g = gt_ref[t]
    lane = lax.broadcasted_iota(jnp.int32, (tq, LANES), 1)
    low_half = lane < HEAD_DIM

    @pl.when(g == 0)
    def _():
        for h in range(DSA_HEADS):
            pair = h // 2
            keep = low_half if h % 2 == 0 else jnp.logical_not(low_half)
            q_pair = q_ref[:, pair * LANES:(pair + 1) * LANES]
            qm_ref[h] = jnp.where(keep, q_pair, jnp.zeros_like(q_pair))
        m_ref[...] = jnp.full(m_ref.shape, NEG_INF, F32)
        l_ref[...] = jnp.zeros(l_ref.shape, F32)
        acc_ref[...] = jnp.zeros(acc_ref.shape, F32)

    def attend(rows, n_keys):
        low = lax.broadcasted_iota(jnp.int32, (rows.stop - rows.start, LANES), 1) < HEAD_DIM
        for pair in range(DSA_HEADS // 2):
            kc = k_ref[0:n_keys, pair * LANES:(pair + 1) * LANES]
            vc = v_ref[0:n_keys, pair * LANES:(pair + 1) * LANES]
            alphas, pvs = [], []
            for h in (2 * pair, 2 * pair + 1):
                s = _dot_nt(qm_ref[h, rows], kc)
                cols = [a + b for a, b in zip(_lane_cols(s), _lane_cols(bias_ref[rows, 0:n_keys]))]
                alpha, pv = _softmax_step(cols, vc, m_ref, l_ref, (h, rows))
                alphas.append(alpha)
                pvs.append(pv)
            alpha_pair = jnp.where(low, alphas[0], alphas[1])
            pv_pair = jnp.where(low, pvs[0], pvs[1])
            acc_ref[pair, rows] = alpha_pair * acc_ref[pair, rows] + pv_pair

    @pl.when(g < qi)
    def _():
        attend(slice(0, tq), tk)

    @pl.when(g == qi)
    def _():
        half = tq // 2
        attend(slice(0, half), half)
        attend(slice(half, tq), tk)
        for pair in range(DSA_HEADS // 2):
            l_pair = jnp.where(low_half, jnp.sum(l_ref[2 * pair], axis=-1, keepdims=True),
                               jnp.sum(l_ref[2 * pair + 1], axis=-1, keepdims=True))
            o_ref[:, pair * LANES:(pair + 1) * LANES] = (acc_ref[pair] / l_pair).astype(BF16)


def _sparse_attn(sq, sk, sv, bias, batch, seq, tq, tk):
    assert tq == tk and seq % tk == 0
    nq, nk = seq // tq, seq // tk
    i_tab, g_tab = _causal_steps(seq, tq, tk)
    q_spec = pl.BlockSpec((tq, BRANCH_WIDTH), lambda b, t, it, gt: (b * nq + it[t], 0))
    kv_spec = pl.BlockSpec((tk, BRANCH_WIDTH), lambda b, t, it, gt: (b * nk + gt[t], 0))
    return pl.pallas_call(
        functools.partial(_sparse_attn_kernel, tq=tq, tk=tk),
        grid_spec=pltpu.PrefetchScalarGridSpec(
            num_scalar_prefetch=2,
            grid=(batch, i_tab.shape[0]),
            in_specs=[q_spec, kv_spec, kv_spec,
                      pl.BlockSpec((tq, tk), lambda b, t, it, gt: (b * nq + it[t], gt[t]))],
            out_specs=q_spec,
            scratch_shapes=[pltpu.VMEM((DSA_HEADS, tq, LANES), BF16),
                            pltpu.VMEM((DSA_HEADS, tq, LANES), F32),
                            pltpu.VMEM((DSA_HEADS, tq, LANES), F32),
                            pltpu.VMEM((DSA_HEADS // 2, tq, LANES), F32)]),
        out_shape=jax.ShapeDtypeStruct(sq.shape, BF16),
        compiler_params=pltpu.CompilerParams(
            dimension_semantics=("arbitrary", "arbitrary"),
            vmem_limit_bytes=VMEM_LIMIT),
        name="sparse_attn",
    )(i_tab, g_tab, sq, sk, sv, bias)


def _merge_kernel(*refs, alpha, d_model, n_sub):
    tokens, weights, outs = refs[:4], refs[4:-3], refs[-3:]
    sub = tokens[0].shape[0] // n_sub
    for r in range(n_sub):
        rows = pl.ds(r * sub, sub)
        _merge_rows(*[t.at[rows] for t in tokens], *weights, *[o.at[rows] for o in outs],
                    alpha=alpha, d_model=d_model)


def _merge_rows(x_ref, ya_ref, yb_ref, p_ref, wg_ref, wbd_ref, wbs_ref, wo_ref, g1_ref, b1_ref,
                wr_ref, br_ref, wpg_ref, wp_ref, x1_ref, res_ref, comb_ref, *, alpha, d_model):
    x = x_ref[...]
    gates = _dot(x.astype(BF16), wg_ref[...])
    merged = (_sigmoid(gates[:, :d_model]) * _dot(ya_ref[...], wbd_ref[...])
              + _sigmoid(gates[:, d_model:]) * _dot(yb_ref[...], wbs_ref[...]))
    mix = _dot(merged.astype(BF16), wo_ref[...])
    x1 = _layer_norm(alpha * x + mix, g1_ref[...], b1_ref[...])
    x1b = x1.astype(BF16)
    x1_ref[...] = x1b
    ple = _sigmoid(_dot(x1b, wpg_ref[...])) * _dot(p_ref[...].astype(BF16), wp_ref[...])
    res_ref[...] = alpha * x1 + ple

    logits = _dot(x1b, wr_ref[...]) + br_ref[...]
    lane = lax.broadcasted_iota(jnp.int32, logits.shape, 1)
    is_group = jnp.logical_and(lane >= N_EXPERTS, lane < N_EXPERTS + N_GROUPS)
    gl = jnp.where(is_group, logits, NEG_INF)
    gmax = jnp.max(gl, axis=-1, keepdims=True)
    gsum = jnp.sum(jnp.where(is_group, jnp.exp(gl - gmax), 0.0), axis=-1, keepdims=True)
    g_val = 1.0 / gsum
    g_idx = jnp.min(jnp.where(jnp.logical_and(is_group, gl == gmax), lane, 4 * LANES),
                    axis=-1, keepdims=True) - N_EXPERTS
    first = g_idx * EXPERTS_PER_GROUP
    in_group = jnp.logical_and(lane >= first, lane < first + EXPERTS_PER_GROUP)
    el = jnp.where(in_group, logits, NEG_INF)
    e1 = jnp.max(el, axis=-1, keepdims=True)
    i1 = jnp.min(jnp.where(jnp.logical_and(in_group, el == e1), lane, 4 * LANES), axis=-1, keepdims=True)
    el2 = jnp.where(lane == i1, NEG_INF, el)
    e2 = jnp.max(el2, axis=-1, keepdims=True)
    i2 = jnp.min(jnp.where(jnp.logical_and(in_group, el2 == e2), lane, 4 * LANES), axis=-1, keepdims=True)
    p2 = jnp.exp(e2 - e1)
    w1 = 1.0 / (1.0 + p2)
    w2 = p2 / (1.0 + p2)
    comb_ref[...] = g_val * (jnp.where(lane == i1, w1, 0.0) + jnp.where(lane == i2, w2, 0.0))


def _merge(x2, ya, yb, p2, w_g, w_bd, w_bs, w_o, g1, b1, w_r, b_r, w_pg, w_p, tm, alpha):
    n, d = x2.shape
    tok = lambda width: pl.BlockSpec((tm, width), lambda i: (i, 0))
    full = lambda a: pl.BlockSpec(a.shape, lambda i: (0, 0))
    weights = (w_g, w_bd, w_bs, w_o, g1, b1, w_r, b_r, w_pg, w_p)
    return pl.pallas_call(
        functools.partial(_merge_kernel, alpha=alpha, d_model=d, n_sub=MERGE_SUBTILES),
        grid=(n // tm,),
        in_specs=[tok(d), tok(BRANCH_WIDTH), tok(BRANCH_WIDTH), tok(p2.shape[1])] + [full(w) for w in weights],
        out_specs=[tok(d), tok(d), tok(LANES)],
        out_shape=[jax.ShapeDtypeStruct((n, d), BF16), jax.ShapeDtypeStruct((n, d), F32),
                   jax.ShapeDtypeStruct((n, LANES), F32)],
        compiler_params=pltpu.CompilerParams(dimension_semantics=("arbitrary",),
                                             vmem_limit_bytes=VMEM_LIMIT),
        name="merge",
    )(x2, ya, yb, p2, *weights)


def _moe_kernel(x1_ref, res_ref, comb_ref, wgu_ref, wd_ref, g2_ref, b2_ref, o_ref, h_ref, acc_ref, *,
                d_expert, eps):
    s = pl.program_id(1)
    x = x1_ref[...]
    comb = comb_ref[...]
    lane = lax.broadcasted_iota(jnp.int32, comb.shape, 1)
    for e in range(eps):
        gu = _dot(x, wgu_ref[e])
        ce = jnp.sum(jnp.where(lane == s * eps + e, comb, 0.0), axis=-1, keepdims=True)
        g = gu[:, :d_expert]
        h = g * _sigmoid(g) * gu[:, d_expert:] * ce
        h_ref[:, e * d_expert:(e + 1) * d_expert] = h.astype(BF16)
    y = _dot(h_ref[...], wd_ref[...])

    @pl.when(s == 0)
    def _():
        acc_ref[...] = y

    @pl.when(s > 0)
    def _():
        acc_ref[...] += y

    @pl.when(s == pl.num_programs(1) - 1)
    def _():
        o_ref[...] = _layer_norm(res_ref[...] + acc_ref[...], g2_ref[...], b2_ref[...])


def _moe(x1b, res, comb, w_gu, w_d, g2, b2, tm, eps):
    n, d = x1b.shape
    n_exp, _, two_f = w_gu.shape
    d_expert = two_f // 2
    assert n_exp % eps == 0 and w_d.shape == (n_exp * d_expert, d)
    tok = lambda width: pl.BlockSpec((tm, width), lambda i, s: (i, 0))
    return pl.pallas_call(
        functools.partial(_moe_kernel, d_expert=d_expert, eps=eps),
        grid=(n // tm, n_exp // eps),
        in_specs=[tok(d), tok(d), tok(LANES),
                  pl.BlockSpec((eps, d, two_f), lambda i, s: (s, 0, 0)),
                  pl.BlockSpec((eps * d_expert, d), lambda i, s: (s, 0)),
                  pl.BlockSpec((1, d), lambda i, s: (0, 0)),
                  pl.BlockSpec((1, d), lambda i, s: (0, 0))],
        out_specs=tok(d),
        out_shape=jax.ShapeDtypeStruct((n, d), F32),
        scratch_shapes=[pltpu.VMEM((tm, eps * d_expert), BF16), pltpu.VMEM((tm, d), F32)],
        compiler_params=pltpu.CompilerParams(dimension_semantics=("arbitrary", "arbitrary"),
                                             vmem_limit_bytes=VMEM_LIMIT),
        name="moe",
    )(x1b, res, comb, w_gu, w_d, g2, b2)


def _rope_tables(positions):
    inv_freq = 1.0 / (ROPE_THETA ** (jnp.arange(0, ROT_DIM, 2, dtype=F32) / ROT_DIM))
    ang = positions.astype(F32)[..., None] * inv_freq
    cos, sin = jnp.cos(ang), jnp.sin(ang)
    n = positions.size
    cos, sin = cos.reshape(n, ROT_HALF), sin.reshape(n, ROT_HALF)
    rest = HEAD_DIM - ROT_DIM
    cos_t = jnp.concatenate([cos, cos, jnp.ones((n, rest), F32)], axis=1)
    sin_a = jnp.concatenate([-sin, jnp.zeros((n, rest + ROT_HALF), F32)], axis=1)
    sin_b = jnp.concatenate([jnp.zeros((n, ROT_HALF), F32), sin, jnp.zeros((n, rest), F32)], axis=1)
    rep = LANES // HEAD_DIM
    return tuple(jnp.tile(t, (1, rep)) for t in (cos_t, sin_a, sin_b))


def _tile(n, want):
    t = min(n, want)
    assert n % t == 0, (n, t)
    return t


def kernel(x, p, positions, w_in, diff_lambda, diff_subln_g, w_branch_diff, w_branch_dsa, w_out, ln1_g, ln1_b, w_route_group, b_route_group, w_route_expert, b_route_expert, w_exp_gate, w_exp_up, w_exp_down, w_ple, w_ple_gate, ln2_g, ln2_b):
    batch, seq, d = x.shape
    depth = w_in.shape[0]
    n = batch * seq
    alpha = (2 * depth) ** 0.25
    topk = min(TOPK_MAX, seq // 4)
    d_expert = w_exp_gate.shape[-1]
    assert BRANCH_WIDTH == DIFF_HEADS * 2 * HEAD_DIM == DSA_HEADS * HEAD_DIM == IDX_HEADS * HEAD_DIM
    n_qkv = 7 * BRANCH_WIDTH
    o_ik, o_iw = n_qkv, n_qkv + HEAD_DIM
    o_ga = o_iw + IDX_HEADS
    assert w_in.shape[2] == o_ga + 2 * d

    tm_proj = _tile(n, 512)
    tq_diff = _tile(seq, 1024)
    tq_select = _tile(seq, 512)
    tk_dsa = _tile(seq, 1024)
    tm_merge = _tile(n, 1024)
    tm_moe = _tile(n, 1024)

    cos_t, sin_a, sin_b = _rope_tables(positions)
    scale = HEAD_DIM ** -0.5
    col_scale = jnp.ones((n_qkv,), F32)
    for seg in (0, 3, 6):
        col_scale = col_scale.at[seg * BRANCH_WIDTH:(seg + 1) * BRANCH_WIDTH].set(scale)

    x2 = x.reshape(n, d)
    for i in range(depth):
        lam_init = 0.8 - 0.6 * math.exp(-0.3 * i)
        wi = w_in[i]
        w_ik = wi[:, o_ik:o_ik + HEAD_DIM]
        w_iw = jnp.pad(wi[:, o_iw:o_iw + IDX_HEADS], ((0, 0), (0, LANES - IDX_HEADS)))
        w_a = jnp.concatenate([wi[:, :n_qkv] * col_scale, w_ik, w_ik, w_iw], axis=1).astype(BF16)
        w_g = wi[:, o_ga:].astype(BF16)
        w_r = jnp.pad(jnp.concatenate([w_route_expert[i], w_route_group[i]], axis=1),
                      ((0, 0), (0, LANES - N_EXPERTS - N_GROUPS))).astype(BF16)
        b_r = jnp.pad(jnp.concatenate([b_route_expert[i], b_route_group[i]]),
                      (0, LANES - N_EXPERTS - N_GROUPS)).reshape(1, LANES)
        w_gu = jnp.concatenate([w_exp_gate[i], w_exp_up[i]], axis=-1).reshape(
            N_EXPERTS, d, 2 * d_expert).astype(BF16)
        w_d = w_exp_down[i].reshape(N_EXPERTS * d_expert, d).astype(BF16)

        dq, dk, dv, sq, sk, sv, iq, ik2, iw = _inproj(x2, w_a, cos_t, sin_a, sin_b, tm_proj)
        ya = _diff_attn(dq, dk, dv, diff_lambda[i], diff_subln_g[i], batch, seq, tq_diff, lam_init)
        bias = _select(iq, ik2, iw, batch, seq, tq_select, tk_dsa, topk)
        yb = _sparse_attn(sq, sk, sv, bias, batch, seq, tk_dsa, tk_dsa)
        x1b, res, comb = _merge(
            x2, ya, yb, p[i].reshape(n, -1), w_g,
            w_branch_diff[i].astype(BF16), w_branch_dsa[i].astype(BF16), w_out[i].astype(BF16),
            ln1_g[i].reshape(1, d), ln1_b[i].reshape(1, d), w_r, b_r,
            w_ple_gate[i].astype(BF16), w_ple[i].astype(BF16), tm_merge, alpha)
        x2 = _moe(x1b, res, comb, w_gu, w_d, ln2_g[i].reshape(1, d), ln2_b[i].reshape(1, d), tm_moe,
                   MOE_EXPERTS_PER_STEP)
    return x2.reshape(batch, seq, d)
```

```python
import functools
import math

import jax
import jax.numpy as jnp
import numpy as np
from jax import lax
from jax.experimental import pallas as pl
from jax.experimental.pallas import tpu as pltpu

F32 = jnp.float32
BF16 = jnp.bfloat16

HEAD_DIM = 64
DIFF_HEADS = 4
DSA_HEADS = 8
IDX_HEADS = 8
TOPK_MAX = 256
N_GROUPS = 4
EXPERTS_PER_GROUP = 8
N_EXPERTS = N_GROUPS * EXPERTS_PER_GROUP
MOE_EXPERTS_PER_STEP = 8
MERGE_SUBTILES = 2
ROPE_THETA = 500000.0
ROT_DIM = HEAD_DIM // 4
ROT_HALF = ROT_DIM // 2
LN_EPS = 1e-5
NEG_INF = -1e30
LOG2E = math.log2(math.e)
LANES = 128
BRANCH_WIDTH = 512
VMEM_LIMIT = 62 * 1024 * 1024


SIGN_BIT = np.int32(-2 ** 31)
MAGNITUDE_BITS = np.int32(2 ** 31 - 1)
SUBNORMAL_BITS = np.int32(2 ** 23 - 1)


def _float_key(v):
    bits = int(np.float32(v).view(np.int32))
    mag = max((bits & int(MAGNITUDE_BITS)) - int(SUBNORMAL_BITS), 0)
    return -mag if bits < 0 else mag


KEY_LO_INIT = _float_key(NEG_INF)
VALUE_STEPS = 24
TIGHTEN_START = 13
TIGHTEN_EVERY = 4
SCORE_CHUNK = 512
UNKNOWN_COUNT = float(2 ** 30)


def _dot_nt(a, b):
    return lax.dot_general(a, b, (((1,), (1,)), ((), ())), preferred_element_type=F32)


def _dot(a, b):
    return jnp.dot(a, b, preferred_element_type=F32)


def _sigmoid(v):
    return 1.0 / (1.0 + jnp.exp(-v))


def _layer_norm(v, g, b):
    mu = jnp.mean(v, axis=-1, keepdims=True)
    d = v - mu
    var = jnp.mean(d * d, axis=-1, keepdims=True)
    return d * lax.rsqrt(var + LN_EPS) * g + b


def _inproj_kernel(x_ref, w_ref, c_ref, sa_ref, sb_ref,
                   dq_ref, dk_ref, dv_ref, sq_ref, sk_ref, sv_ref, iq_ref, ik_ref, iw_ref):
    xb = x_ref[...].astype(BF16)
    cos_t = c_ref[...]
    sin_a = sa_ref[...]
    sin_b = sb_ref[...]

    def rope(h):
        cols = []
        for c in range(h.shape[1] // LANES):
            hc = h[:, c * LANES:(c + 1) * LANES]
            cols.append(hc * cos_t
                        + pltpu.roll(hc, LANES - ROT_HALF, 1) * sin_a
                        + pltpu.roll(hc, ROT_HALF, 1) * sin_b)
        return cols[0] if len(cols) == 1 else jnp.concatenate(cols, axis=1)

    outs = ((dq_ref, True, LOG2E), (dk_ref, True, None), (dv_ref, False, None), (sq_ref, True, LOG2E),
            (sk_ref, True, None), (sv_ref, False, None), (iq_ref, True, None))
    for i, (ref, rotary, mult) in enumerate(outs):
        h = _dot(xb, w_ref[:, i * BRANCH_WIDTH:(i + 1) * BRANCH_WIDTH])
        if mult is not None:
            h = h * mult
        if rotary:
            h = rope(h)
        ref[...] = h.astype(BF16)
    base = len(outs) * BRANCH_WIDTH
    h = _dot(xb, w_ref[:, base:base + 2 * LANES])
    ik_ref[...] = rope(h[:, :LANES]).astype(BF16)
    iw_ref[...] = h[:, LANES:] * (IDX_HEADS ** -0.5)


def _inproj(x2, w_a, cos_t, sin_a, sin_b, tm):
    n, d = x2.shape
    wc = w_a.shape[1]
    tok = lambda width: pl.BlockSpec((tm, width), lambda i: (i, 0))
    out_shape = [jax.ShapeDtypeStruct((n, BRANCH_WIDTH), BF16)] * 7 + [
        jax.ShapeDtypeStruct((n, LANES), BF16), jax.ShapeDtypeStruct((n, LANES), F32)]
    return pl.pallas_call(
        _inproj_kernel,
        grid=(n // tm,),
        in_specs=[tok(d), pl.BlockSpec((d, wc), lambda i: (0, 0)), tok(LANES), tok(LANES), tok(LANES)],
        out_specs=[tok(BRANCH_WIDTH)] * 7 + [tok(LANES), tok(LANES)],
        out_shape=out_shape,
        compiler_params=pltpu.CompilerParams(dimension_semantics=("arbitrary",),
                                             vmem_limit_bytes=VMEM_LIMIT),
        name="inproj",
    )(x2, w_a, cos_t, sin_a, sin_b)


def _lane_cols(s):
    return [s[:, c * LANES:(c + 1) * LANES] for c in range(s.shape[1] // LANES)]


def _softmax_step(cols, vc, m_ref, l_ref, idx, transposed=False):
    mx = functools.reduce(jnp.maximum, cols)
    m_prev = m_ref[idx]
    m_new = jnp.maximum(m_prev, jnp.max(mx, axis=-1, keepdims=True))
    alpha = jnp.exp2(m_prev - m_new)
    ps = [jnp.exp2(c - m_new) for c in cols]
    l_ref[idx] = alpha * l_ref[idx] + functools.reduce(jnp.add, ps)
    m_ref[idx] = m_new
    p = (ps[0] if len(ps) == 1 else jnp.concatenate(ps, axis=1)).astype(BF16)
    if transposed:
        return alpha, lax.dot_general(vc, p, (((0,), (1,)), ((), ())), preferred_element_type=F32)
    return alpha, _dot(p, vc)


def _diff_kernel(q_ref, k_ref, v_ref, lam_ref, g_ref, o_ref, qm_ref, m_ref, l_ref, acc_ref, *,
                 tq, lam_init):
    qi = pl.program_id(2)
    q = q_ref[...]
    lane = lax.broadcasted_iota(jnp.int32, q.shape, 1)
    zero = jnp.zeros_like(q)
    qm_ref[0] = jnp.where(lane < HEAD_DIM, q, zero)
    qm_ref[1] = jnp.where(lane >= HEAD_DIM, q, zero)
    m_ref[...] = jnp.full(m_ref.shape, NEG_INF, F32)
    l_ref[...] = jnp.zeros(l_ref.shape, F32)
    acc_ref[...] = jnp.zeros(acc_ref.shape, F32)

    def tile(rows, start, n_keys, causal):
        kc = k_ref[pl.ds(start, n_keys), :]
        vc = v_ref[pl.ds(start, n_keys), :]
        for c in range(2):
            s = _dot_nt(qm_ref[c, rows], kc)
            if causal:
                row = qi * tq + rows.start + lax.broadcasted_iota(jnp.int32, s.shape, 0)
                col = start + lax.broadcasted_iota(jnp.int32, s.shape, 1)
                s = jnp.where(col <= row, s, NEG_INF)
            alpha, pv_t = _softmax_step(_lane_cols(s), vc, m_ref, l_ref, (c, rows), transposed=True)
            acc_ref[c, :, rows] = alpha.T * acc_ref[c, :, rows] + pv_t

    def body(j, carry):
        tile(slice(0, tq), pl.multiple_of(j * tq, tq), tq, False)
        return carry

    lax.fori_loop(0, qi, body, 0)
    diag = pl.multiple_of(qi * tq, tq)
    half = tq // 2
    tile(slice(0, half), diag, half, True)
    tile(slice(half, tq), diag, tq, True)

    lp = lam_ref[...]
    lam = (jnp.exp(jnp.sum(lp[0:1] * lp[1:2], axis=-1, keepdims=True))
           - jnp.exp(jnp.sum(lp[2:3] * lp[3:4], axis=-1, keepdims=True)) + lam_init)
    l0 = jnp.sum(l_ref[0], axis=-1, keepdims=True)
    l1 = jnp.sum(l_ref[1], axis=-1, keepdims=True)
    o = acc_ref[0].T / l0 - lam * (acc_ref[1].T / l1)
    o = o * lax.rsqrt(jnp.mean(o * o, axis=-1, keepdims=True) + LN_EPS)
    o_ref[...] = (o * g_ref[...] * (1.0 - lam_init)).astype(BF16)


def _diff_attn(dq, dk, dv, lam_params, subln_g, batch, seq, tq, lam_init):
    assert seq % tq == 0 and tq % (2 * LANES) == 0
    nq = seq // tq
    hw = 2 * HEAD_DIM
    kv_spec = pl.BlockSpec((seq, hw), lambda b, h, i: (b, h))
    q_spec = pl.BlockSpec((tq, hw), lambda b, h, i: (b * nq + i, h))
    return pl.pallas_call(
        functools.partial(_diff_kernel, tq=tq, lam_init=lam_init),
        grid=(batch, DIFF_HEADS, nq),
        in_specs=[q_spec, kv_spec, kv_spec,
                  pl.BlockSpec(lam_params.shape, lambda b, h, i: (0, 0)),
                  pl.BlockSpec((1, hw), lambda b, h, i: (0, 0))],
        out_specs=q_spec,
        out_shape=jax.ShapeDtypeStruct(dq.shape, BF16),
        scratch_shapes=[pltpu.VMEM((2, tq, hw), BF16), pltpu.VMEM((2, tq, LANES), F32),
                        pltpu.VMEM((2, tq, LANES), F32), pltpu.VMEM((2, hw, tq), F32)],
        compiler_params=pltpu.CompilerParams(
            dimension_semantics=("arbitrary", "arbitrary", "arbitrary"),
            vmem_limit_bytes=VMEM_LIMIT),
        name="diff_attn",
    )(dq, dk, dv, lam_params, subln_g.reshape(1, hw))


def _key_to_float(key):
    mag = jnp.abs(key)
    bits = jnp.where(mag > 0, mag + SUBNORMAL_BITS, 0)
    return lax.bitcast_convert_type(jnp.where(key < 0, bits | SIGN_BIT, bits), F32)


def _float_to_key(v):
    bits = lax.bitcast_convert_type(v, jnp.int32)
    mag = jnp.maximum((bits & MAGNITUDE_BITS) - SUBNORMAL_BITS, 0)
    return jnp.where(bits < 0, -mag, mag)


def _causal_steps(seq, tq, tk):
    pairs = [(i, g) for i in range(seq // tq) for g in range(-(-(i + 1) * tq // tk))]
    return (jnp.asarray([p[0] for p in pairs], jnp.int32), jnp.asarray([p[1] for p in pairs], jnp.int32))


def _select_kernel(it_ref, gt_ref, iq_ref, iw_ref, ik_ref, tri_ref, o_ref,
                   sc_ref, iqm_ref, wb_ref, thr_ref, val_ref, need_ref, seen_ref, flag_ref, *,
                   tq, tk, topk):
    t = pl.program_id(1)
    qi = it_ref[t]
    g = gt_ref[t]
    ck = SCORE_CHUNK
    cps = tk // ck
    bands = [slice(b * LANES, (b + 1) * LANES) for b in range(tq // LANES)]
    tile = (LANES, LANES)

    def rows_of(vec):
        return jnp.broadcast_to(vec, tile).T

    def per_row(mat, reduce):
        return reduce(mat.T, axis=0, keepdims=True)

    @pl.when(g == 0)
    def _search():
        lane = lax.broadcasted_iota(jnp.int32, (tq, LANES), 1)
        low_half = lane < HEAD_DIM
        nch = (qi + 1) * tq // ck
        n_plain = qi * tq // ck

        iw = iw_ref[...]
        for h in range(IDX_HEADS):
            pair = h // 2
            keep = low_half if h % 2 == 0 else jnp.logical_not(low_half)
            iq_pair = iq_ref[:, pair * LANES:(pair + 1) * LANES]
            iqm_ref[h] = jnp.where(keep, iq_pair, jnp.zeros_like(iq_pair))
            wb_ref[h] = jnp.broadcast_to(iw[:, h:h + 1], (tq, LANES))

        def score_chunk(j, masked, top2):
            start = pl.multiple_of(j * ck, ck)
            kc = ik_ref[pl.ds(start, ck), :]
            parts = [jnp.zeros((tq, LANES), F32) for _ in range(ck // LANES)]
            for h in range(IDX_HEADS):
                d = _dot_nt(iqm_ref[h], kc)
                w = wb_ref[h]
                for c in range(ck // LANES):
                    parts[c] = parts[c] + w * jnp.maximum(d[:, c * LANES:(c + 1) * LANES], 0.0)
            sc = jnp.concatenate(parts, axis=1) + 0.0
            if masked:
                row = qi * tq + lax.broadcasted_iota(jnp.int32, (tq, ck), 0)
                col = start + lax.broadcasted_iota(jnp.int32, (tq, ck), 1)
                sc = jnp.where(col <= row, sc, NEG_INF)
            sc_ref[j] = sc
            max1, max2 = top2
            for part in _lane_cols(sc):
                max2 = jnp.maximum(max2, jnp.minimum(max1, part))
                max1 = jnp.maximum(max1, part)
            return max1, max2

        neg = jnp.full((tq, LANES), NEG_INF, F32)
        top2 = lax.fori_loop(0, n_plain, lambda j, s: score_chunk(j, False, s), (neg, neg))
        max1, max2 = lax.fori_loop(n_plain, nch, lambda j, s: score_chunk(j, True, s), top2)

        n_steps = (nch + cps - 1) // cps

        def fill(j, carry):
            sc_ref[j] = jnp.full((tq, ck), NEG_INF, F32)
            return carry
        lax.fori_loop(nch, n_steps * cps, fill, 0)

        lane_row = lax.broadcasted_iota(jnp.int32, (1, LANES), 1)

        def sweep(b, fn, init):
            def body(j, acc):
                for c, part in enumerate(_lane_cols(sc_ref[j, bands[b], :])):
                    acc = fn(acc, part, j, c)
                return acc
            return lax.fori_loop(0, nch, body, init)

        def sweep_if(active, b, fn, init):
            if active is None:
                return sweep(b, fn, init)
            return lax.cond(active[b] > 0.0, lambda: sweep(b, fn, init), lambda: init)

        def count_above(thrs, active=None):
            thr_bs = [rows_of(v) for v in thrs]
            cnts = [sweep_if(active, b, lambda acc, part, j, c, v=v: acc + jnp.where(part > v, 1.0, 0.0),
                             jnp.zeros(tile, F32)) for b, v in enumerate(thr_bs)]
            return [per_row(c, jnp.sum) for c in cnts]

        def band_extent(lo_fs, hi_fs, active):
            edges = [(rows_of(lo_f), rows_of(hi_f)) for lo_f, hi_f in zip(lo_fs, hi_fs)]

            def fn(acc, part, j, c, lo_b, hi_b):
                top, bot = acc
                return (jnp.maximum(top, jnp.where(part <= hi_b, part, -jnp.inf)),
                        jnp.minimum(bot, jnp.where(part > lo_b, part, jnp.inf)))
            init = (jnp.full(tile, -jnp.inf, F32), jnp.full(tile, jnp.inf, F32))
            ext = [sweep_if(active, b, functools.partial(fn, lo_b=lo_b, hi_b=hi_b), init)
                   for b, (lo_b, hi_b) in enumerate(edges)]
            return [(per_row(top, jnp.max), per_row(bot, jnp.min)) for top, bot in ext]

        kks = [jnp.minimum(qi * tq + b * LANES + lane_row + 1, topk).astype(F32) for b in range(len(bands))]

        def unresolved(b, lo, hi, flo):
            return jnp.logical_and(flo != kks[b], hi - 1 > lo)

        def any_set(flags):
            return functools.reduce(jnp.maximum, [jnp.max(jnp.where(f, 1.0, 0.0)) for f in flags])

        def active_bands(state):
            return tuple(jnp.max(jnp.where(unresolved(b, s[0], s[1], s[2]), 1.0, 0.0))
                         for b, s in enumerate(state))

        def midpoint(it, lo, hi):
            mid_key = (lo >> 1) + (hi >> 1) + (lo & hi & 1)
            mid_val = _float_to_key(0.5 * _key_to_float(lo) + 0.5 * _key_to_float(hi))
            by_value = jnp.logical_and(jnp.logical_and(mid_val > lo, mid_val < hi), it < VALUE_STEPS)
            return jnp.where(by_value, mid_val, mid_key)

        def bisect(it, active, state):
            mids = [midpoint(it, lo, hi) for lo, hi, _, _ in state]
            cnts = count_above([_key_to_float(mid) for mid in mids], active)
            out = []
            for b, (lo, hi, flo, fhi) in enumerate(state):
                mid, cnt = mids[b], cnts[b]
                less = cnt < kks[b]
                up = jnp.logical_and(active[b] > 0.0, jnp.logical_not(less))
                down = jnp.logical_and(active[b] > 0.0, less)
                out.append((jnp.where(up, mid, lo), jnp.where(down, mid, hi),
                            jnp.where(up, cnt, flo), jnp.where(down, cnt, fhi)))
            return tuple(out)

        def tighten(active, state):
            ext = band_extent([_key_to_float(s[0]) for s in state], [_key_to_float(s[1]) for s in state],
                              active)
            out = []
            for b, (lo, hi, flo, fhi) in enumerate(state):
                top, bot = ext[b]
                todo = jnp.logical_and(active[b] > 0.0, unresolved(b, lo, hi, flo))
                new_lo = jnp.maximum(_float_to_key(bot) - 1, lo)
                new_hi = jnp.minimum(_float_to_key(top), hi)
                out.append((jnp.where(todo, new_lo, lo), jnp.where(todo, new_hi, hi), flo, fhi))
            return tuple(out)

        def search_body(st):
            it, active, state = st
            do_tighten = jnp.logical_and(it >= TIGHTEN_START, (it - TIGHTEN_START) % TIGHTEN_EVERY == 0)
            state = lax.cond(do_tighten, functools.partial(tighten, active),
                             functools.partial(bisect, it, active), state)
            return it + 1, active_bands(state), state

        hi0s = [_float_to_key(per_row(max1[rows], jnp.max)) for rows in bands]
        lo0s = [jnp.maximum(_float_to_key(per_row(max2[rows], jnp.min)) - 1, KEY_LO_INIT) for rows in bands]
        flo0s = [jnp.where(qi * tq + b * LANES + lane_row < topk, kks[b], UNKNOWN_COUNT)
                 for b in range(len(bands))]
        state0 = tuple((lo0s[b], hi0s[b], flo0s[b], jnp.zeros((1, LANES), F32)) for b in range(len(bands)))
        _, _, state = lax.while_loop(lambda st: functools.reduce(jnp.maximum, st[1]) > 0.0, search_body,
                                     (jnp.int32(0), active_bands(state0), state0))

        tie_rows = [s[2] > kks[b] for b, s in enumerate(state)]
        stack_rows = lambda vecs: jnp.concatenate([rows_of(v) for v in vecs], axis=0)
        thr_ref[...] = stack_rows([_key_to_float(jnp.where(tie_rows[b], s[1], s[0])) for b, s in enumerate(state)])
        val_ref[...] = stack_rows([_key_to_float(s[1]) for s in state])
        need_ref[...] = stack_rows([jnp.where(tie_rows[b], kks[b] - s[3], 0.0) for b, s in enumerate(state)])
        seen_ref[...] = jnp.zeros(seen_ref.shape, F32)
        for b, tied_rows in enumerate(tie_rows):
            flag_ref[b] = (any_set([tied_rows]) > 0.0).astype(jnp.int32)

    def write_band(rows, with_ties):
        def run(_):
            thr_b = thr_ref[rows]
            seen = seen_ref[rows] if with_ties else None
            for u in range(cps):
                blk = sc_ref[g * cps + u, rows, :]
                for c, part in enumerate(_lane_cols(blk)):
                    hit = part > thr_b
                    if with_ties:
                        tied = part == val_ref[rows]
                        counts = _dot(jnp.where(tied, 1.0, 0.0).astype(BF16), tri_ref[...])
                        rank = seen + counts[:, :LANES]
                        hit = jnp.logical_or(hit, jnp.logical_and(tied, rank <= need_ref[rows]))
                        seen = seen + counts[:, LANES:]
                    o_ref[rows, u * ck + c * LANES:u * ck + (c + 1) * LANES] = jnp.where(hit, 0.0, NEG_INF)
            if with_ties:
                seen_ref[rows] = seen
            return 0
        return run

    for b, rows in enumerate(bands):
        lax.cond(flag_ref[b] > 0, write_band(rows, True), write_band(rows, False), 0)


def _select(iq, ik2, iw, batch, seq, tq, tk, topk):
    ck = SCORE_CHUNK
    assert tq % LANES == 0 and tq % ck == 0 and tk % ck == 0 and seq % tk == 0 and topk <= 2 * LANES
    nq = seq // tq
    i_tab, g_tab = _causal_steps(seq, tq, tk)
    k_idx = jnp.arange(LANES)
    tri = jnp.concatenate([(k_idx[:, None] <= k_idx[None, :]).astype(BF16),
                           jnp.ones((LANES, LANES), BF16)], axis=1)
    q_spec = lambda width: pl.BlockSpec((tq, width), lambda b, t, it, gt: (b * nq + it[t], 0))
    stat = pltpu.VMEM((tq, LANES), F32)
    return pl.pallas_call(
        functools.partial(_select_kernel, tq=tq, tk=tk, topk=topk),
        grid_spec=pltpu.PrefetchScalarGridSpec(
            num_scalar_prefetch=2,
            grid=(batch, i_tab.shape[0]),
            in_specs=[q_spec(BRANCH_WIDTH), q_spec(LANES),
                      pl.BlockSpec((seq, LANES), lambda b, t, it, gt: (b, 0), pipeline_mode=pl.Buffered(1)),
                      pl.BlockSpec(tri.shape, lambda b, t, it, gt: (0, 0))],
            out_specs=pl.BlockSpec((tq, tk), lambda b, t, it, gt: (b * nq + it[t], gt[t])),
            scratch_shapes=[pltpu.VMEM((seq // ck, tq, ck), F32),
                            pltpu.VMEM((IDX_HEADS, tq, LANES), BF16),
                            pltpu.VMEM((IDX_HEADS, tq, LANES), F32),
                            stat, stat, stat, stat,
                            pltpu.SMEM((tq // LANES,), jnp.int32)]),
        out_shape=jax.ShapeDtypeStruct((batch * seq, seq), F32),
        compiler_params=pltpu.CompilerParams(
            dimension_semantics=("arbitrary", "arbitrary"),
            vmem_limit_bytes=VMEM_LIMIT),
        name="select",
    )(i_tab, g_tab, iq, iw, ik2, tri)


def _sparse_attn_kernel(it_ref, gt_ref, q_ref, k_ref, v_ref, bias_ref, o_ref,
                        qm_ref, m_ref, l_ref, acc_ref, *, tq, tk):
    t = pl.program_id(1)
    qi = it_ref[t]
    g = gt_ref[t]
    lane = lax.broadcasted_iota(jnp.int32, (tq, LANES), 1)
    low_half = lane < HEAD_DIM

    @pl.when(g == 0)
    def _():
        for h in range(DSA_HEADS):
            pair = h // 2
            keep = low_half if h % 2 == 0 else jnp.logical_not(low_half)
            q_pair = q_ref[:, pair * LANES:(pair + 1) * LANES]
            qm_ref[h] = jnp.where(keep, q_pair, jnp.zeros_like(q_pair))
        m_ref[...] = jnp.full(m_ref.shape, NEG_INF, F32)
        l_ref[...] = jnp.zeros(l_ref.shape, F32)
        acc_ref[...] = jnp.zeros(acc_ref.shape, F32)

    def attend(rows, n_keys):
        low = lax.broadcasted_iota(jnp.int32, (rows.stop - rows.start, LANES), 1) < HEAD_DIM
        for pair in range(DSA_HEADS // 2):
            kc = k_ref[0:n_keys, pair * LANES:(pair + 1) * LANES]
            vc = v_ref[0:n_keys, pair * LANES:(pair + 1) * LANES]
            alphas, pvs = [], []
            for h in (2 * pair, 2 * pair + 1):
                s = _dot_nt(qm_ref[h, rows], kc)
                cols = [a + b for a, b in zip(_lane_cols(s), _lane_cols(bias_ref[rows, 0:n_keys]))]
                alpha, pv = _softmax_step(cols, vc, m_ref, l_ref, (h, rows))
                alphas.append(alpha)
                pvs.append(pv)
            alpha_pair = jnp.where(low, alphas[0], alphas[1])
            pv_pair = jnp.where(low, pvs[0], pvs[1])
            acc_ref[pair, rows] = alpha_pair * acc_ref[pair, rows] + pv_pair

    @pl.when(g < qi)
    def _():
        attend(slice(0, tq), tk)

    @pl.when(g == qi)
    def _():
        half = tq // 2
        attend(slice(0, half), half)
        attend(slice(half, tq), tk)
        for pair in range(DSA_HEADS // 2):
            l_pair = jnp.where(low_half, jnp.sum(l_ref[2 * pair], axis=-1, keepdims=True),
                               jnp.sum(l_ref[2 * pair + 1], axis=-1, keepdims=True))
            o_ref[:, pair * LANES:(pair + 1) * LANES] = (acc_ref[pair] / l_pair).astype(BF16)


def _sparse_attn(sq, sk, sv, bias, batch, seq, tq, tk):
    assert tq == tk and seq % tk == 0
    nq, nk = seq // tq, seq // tk
    i_tab, g_tab = _causal_steps(seq, tq, tk)
    q_spec = pl.BlockSpec((tq, BRANCH_WIDTH), lambda b, t, it, gt: (b * nq + it[t], 0))
    kv_spec = pl.BlockSpec((tk, BRANCH_WIDTH), lambda b, t, it, gt: (b * nk + gt[t], 0))
    return pl.pallas_call(
        functools.partial(_sparse_attn_kernel, tq=tq, tk=tk),
        grid_spec=pltpu.PrefetchScalarGridSpec(
            num_scalar_prefetch=2,
            grid=(batch, i_tab.shape[0]),
            in_specs=[q_spec, kv_spec, kv_spec,
                      pl.BlockSpec((tq, tk), lambda b, t, it, gt: (b * nq + it[t], gt[t]))],
            out_specs=q_spec,
            scratch_shapes=[pltpu.VMEM((DSA_HEADS, tq, LANES), BF16),
                            pltpu.VMEM((DSA_HEADS, tq, LANES), F32),
                            pltpu.VMEM((DSA_HEADS, tq, LANES), F32),
                            pltpu.VMEM((DSA_HEADS // 2, tq, LANES), F32)]),
        out_shape=jax.ShapeDtypeStruct(sq.shape, BF16),
        compiler_params=pltpu.CompilerParams(
            dimension_semantics=("arbitrary", "arbitrary"),
            vmem_limit_bytes=VMEM_LIMIT),
        name="sparse_attn",
    )(i_tab, g_tab, sq, sk, sv, bias)


def _merge_kernel(*refs, alpha, d_model, n_sub):
    tokens, weights, outs = refs[:4], refs[4:-3], refs[-3:]
    sub = tokens[0].shape[0] // n_sub
    for r in range(n_sub):
        rows = pl.ds(r * sub, sub)
        _merge_rows(*[t.at[rows] for t in tokens], *weights, *[o.at[rows] for o in outs],
                    alpha=alpha, d_model=d_model)


def _merge_rows(x_ref, ya_ref, yb_ref, p_ref, wg_ref, wbd_ref, wbs_ref, wo_ref, g1_ref, b1_ref,
                wr_ref, br_ref, wpg_ref, wp_ref, x1_ref, res_ref, comb_ref, *, alpha, d_model):
    x = x_ref[...]
    gates = _dot(x.astype(BF16), wg_ref[...])
    merged = (_sigmoid(gates[:, :d_model]) * _dot(ya_ref[...], wbd_ref[...])
              + _sigmoid(gates[:, d_model:]) * _dot(yb_ref[...], wbs_ref[...]))
    mix = _dot(merged.astype(BF16), wo_ref[...])
    x1 = _layer_norm(alpha * x + mix, g1_ref[...], b1_ref[...])
    x1b = x1.astype(BF16)
    x1_ref[...] = x1b
    ple = _sigmoid(_dot(x1b, wpg_ref[...])) * _dot(p_ref[...].astype(BF16), wp_ref[...])
    res_ref[...] = alpha * x1 + ple

    logits = _dot(x1b, wr_ref[...]) + br_ref[...]
    lane = lax.broadcasted_iota(jnp.int32, logits.shape, 1)
    is_group = jnp.logical_and(lane >= N_EXPERTS, lane < N_EXPERTS + N_GROUPS)
    gl = jnp.where(is_group, logits, NEG_INF)
    gmax = jnp.max(gl, axis=-1, keepdims=True)
    gsum = jnp.sum(jnp.where(is_group, jnp.exp(gl - gmax), 0.0), axis=-1, keepdims=True)
    g_val = 1.0 / gsum
    g_idx = jnp.min(jnp.where(jnp.logical_and(is_group, gl == gmax), lane, 4 * LANES),
                    axis=-1, keepdims=True) - N_EXPERTS
    first = g_idx * EXPERTS_PER_GROUP
    in_group = jnp.logical_and(lane >= first, lane < first + EXPERTS_PER_GROUP)
    el = jnp.where(in_group, logits, NEG_INF)
    e1 = jnp.max(el, axis=-1, keepdims=True)
    i1 = jnp.min(jnp.where(jnp.logical_and(in_group, el == e1), lane, 4 * LANES), axis=-1, keepdims=True)
    el2 = jnp.where(lane == i1, NEG_INF, el)
    e2 = jnp.max(el2, axis=-1, keepdims=True)
    i2 = jnp.min(jnp.where(jnp.logical_and(in_group, el2 == e2), lane, 4 * LANES), axis=-1, keepdims=True)
    p2 = jnp.exp(e2 - e1)
    w1 = 1.0 / (1.0 + p2)
    w2 = p2 / (1.0 + p2)
    comb_ref[...] = g_val * (jnp.where(lane == i1, w1, 0.0) + jnp.where(lane == i2, w2, 0.0))


def _merge(x2, ya, yb, p2, w_g, w_bd, w_bs, w_o, g1, b1, w_r, b_r, w_pg, w_p, tm, alpha):
    n, d = x2.shape
    tok = lambda width: pl.BlockSpec((tm, width), lambda i: (i, 0))
    full = lambda a: pl.BlockSpec(a.shape, lambda i: (0, 0))
    weights = (w_g, w_bd, w_bs, w_o, g1, b1, w_r, b_r, w_pg, w_p)
    return pl.pallas_call(
        functools.partial(_merge_kernel, alpha=alpha, d_model=d, n_sub=MERGE_SUBTILES),
        grid=(n // tm,),
        in_specs=[tok(d), tok(BRANCH_WIDTH), tok(BRANCH_WIDTH), tok(p2.shape[1])] + [full(w) for w in weights],
        out_specs=[tok(d), tok(d), tok(LANES)],
        out_shape=[jax.ShapeDtypeStruct((n, d), BF16), jax.ShapeDtypeStruct((n, d), F32),
                   jax.ShapeDtypeStruct((n, LANES), F32)],
        compiler_params=pltpu.CompilerParams(dimension_semantics=("arbitrary",),
                                             vmem_limit_bytes=VMEM_LIMIT),
        name="merge",
    )(x2, ya, yb, p2, *weights)


def _moe_kernel(x1_ref, res_ref, comb_ref, wgu_ref, wd_ref, g2_ref, b2_ref, o_ref, h_ref, acc_ref, *,
                d_expert, eps):
    s = pl.program_id(1)
    x = x1_ref[...]
    comb = comb_ref[...]
    lane = lax.broadcasted_iota(jnp.int32, comb.shape, 1)
    for e in range(eps):
        gu = _dot(x, wgu_ref[e])
        ce = jnp.sum(jnp.where(lane == s * eps + e, comb, 0.0), axis=-1, keepdims=True)
        g = gu[:, :d_expert]
        h = g * _sigmoid(g) * gu[:, d_expert:] * ce
        h_ref[:, e * d_expert:(e + 1) * d_expert] = h.astype(BF16)
    y = _dot(h_ref[...], wd_ref[...])

    @pl.when(s == 0)
    def _():
        acc_ref[...] = y

    @pl.when(s > 0)
    def _():
        acc_ref[...] += y

    @pl.when(s == pl.num_programs(1) - 1)
    def _():
        o_ref[...] = _layer_norm(res_ref[...] + acc_ref[...], g2_ref[...], b2_ref[...])


def _moe(x1b, res, comb, w_gu, w_d, g2, b2, tm, eps):
    n, d = x1b.shape
    n_exp, _, two_f = w_gu.shape
    d_expert = two_f // 2
    assert n_exp % eps == 0 and w_d.shape == (n_exp * d_expert, d)
    tok = lambda width: pl.BlockSpec((tm, width), lambda i, s: (i, 0))
    return pl.pallas_call(
        functools.partial(_moe_kernel, d_expert=d_expert, eps=eps),
        grid=(n // tm, n_exp // eps),
        in_specs=[tok(d), tok(d), tok(LANES),
                  pl.BlockSpec((eps, d, two_f), lambda i, s: (s, 0, 0)),
                  pl.BlockSpec((eps * d_expert, d), lambda i, s: (s, 0)),
                  pl.BlockSpec((1, d), lambda i, s: (0, 0)),
                  pl.BlockSpec((1, d), lambda i, s: (0, 0))],
        out_specs=tok(d),
        out_shape=jax.ShapeDtypeStruct((n, d), F32),
        scratch_shapes=[pltpu.VMEM((tm, eps * d_expert), BF16), pltpu.VMEM((tm, d), F32)],
        compiler_params=pltpu.CompilerParams(dimension_semantics=("arbitrary", "arbitrary"),
                                             vmem_limit_bytes=VMEM_LIMIT),
        name="moe",
    )(x1b, res, comb, w_gu, w_d, g2, b2)


def _rope_tables(positions):
    inv_freq = 1.0 / (ROPE_THETA ** (jnp.arange(0, ROT_DIM, 2, dtype=F32) / ROT_DIM))
    ang = positions.astype(F32)[..., None] * inv_freq
    cos, sin = jnp.cos(ang), jnp.sin(ang)
    n = positions.size
    cos, sin = cos.reshape(n, ROT_HALF), sin.reshape(n, ROT_HALF)
    rest = HEAD_DIM - ROT_DIM
    cos_t = jnp.concatenate([cos, cos, jnp.ones((n, rest), F32)], axis=1)
    sin_a = jnp.concatenate([-sin, jnp.zeros((n, rest + ROT_HALF), F32)], axis=1)
    sin_b = jnp.concatenate([jnp.zeros((n, ROT_HALF), F32), sin, jnp.zeros((n, rest), F32)], axis=1)
    rep = LANES // HEAD_DIM
    return tuple(jnp.tile(t, (1, rep)) for t in (cos_t, sin_a, sin_b))


def _tile(n, want):
    t = min(n, want)
    assert n % t == 0, (n, t)
    return t


def kernel(x, p, positions, w_in, diff_lambda, diff_subln_g, w_branch_diff, w_branch_dsa, w_out, ln1_g, ln1_b, w_route_group, b_route_group, w_route_expert, b_route_expert, w_exp_gate, w_exp_up, w_exp_down, w_ple, w_ple_gate, ln2_g, ln2_b):
    batch, seq, d = x.shape
    depth = w_in.shape[0]
    n = batch * seq
    alpha = (2 * depth) ** 0.25
    topk = min(TOPK_MAX, seq // 4)
    d_expert = w_exp_gate.shape[-1]
    assert BRANCH_WIDTH == DIFF_HEADS * 2 * HEAD_DIM == DSA_HEADS * HEAD_DIM == IDX_HEADS * HEAD_DIM
    n_qkv = 7 * BRANCH_WIDTH
    o_ik, o_iw = n_qkv, n_qkv + HEAD_DIM
    o_ga = o_iw + IDX_HEADS
    assert w_in.shape[2] == o_ga + 2 * d

    tm_proj = _tile(n, 512)
    tq_diff = _tile(seq, 1024)
    tq_select = _tile(seq, 512)
    tk_dsa = _tile(seq, 1024)
    tm_merge = _tile(n, 1024)
    tm_moe = _tile(n, 1024)

    cos_t, sin_a, sin_b = _rope_tables(positions)
    scale = HEAD_DIM ** -0.5
    col_scale = jnp.ones((n_qkv,), F32)
    for seg in (0, 3, 6):
        col_scale = col_scale.at[seg * BRANCH_WIDTH:(seg + 1) * BRANCH_WIDTH].set(scale)

    x2 = x.reshape(n, d)
    for i in range(depth):
        lam_init = 0.8 - 0.6 * math.exp(-0.3 * i)
        wi = w_in[i]
        w_ik = wi[:, o_ik:o_ik + HEAD_DIM]
        w_iw = jnp.pad(wi[:, o_iw:o_iw + IDX_HEADS], ((0, 0), (0, LANES - IDX_HEADS)))
        w_a = jnp.concatenate([wi[:, :n_qkv] * col_scale, w_ik, w_ik, w_iw], axis=1).astype(BF16)
        w_g = wi[:, o_ga:].astype(BF16)
        w_r = jnp.pad(jnp.concatenate([w_route_expert[i], w_route_group[i]], axis=1),
                      ((0, 0), (0, LANES - N_EXPERTS - N_GROUPS))).astype(BF16)
        b_r = jnp.pad(jnp.concatenate([b_route_expert[i], b_route_group[i]]),
                      (0, LANES - N_EXPERTS - N_GROUPS)).reshape(1, LANES)
        w_gu = jnp.concatenate([w_exp_gate[i], w_exp_up[i]], axis=-1).reshape(
            N_EXPERTS, d, 2 * d_expert).astype(BF16)
        w_d = w_exp_down[i].reshape(N_EXPERTS * d_expert, d).astype(BF16)

        dq, dk, dv, sq, sk, sv, iq, ik2, iw = _inproj(x2, w_a, cos_t, sin_a, sin_b, tm_proj)
        ya = _diff_attn(dq, dk, dv, diff_lambda[i], diff_subln_g[i], batch, seq, tq_diff, lam_init)
        bias = _select(iq, ik2, iw, batch, seq, tq_select, tk_dsa, topk)
        yb = _sparse_attn(sq, sk, sv, bias, batch, seq, tk_dsa, tk_dsa)
        x1b, res, comb = _merge(
            x2, ya, yb, p[i].reshape(n, -1), w_g,
            w_branch_diff[i].astype(BF16), w_branch_dsa[i].astype(BF16), w_out[i].astype(BF16),
            ln1_g[i].reshape(1, d), ln1_b[i].reshape(1, d), w_r, b_r,
            w_ple_gate[i].astype(BF16), w_ple[i].astype(BF16), tm_merge, alpha)
        x2 = _moe(x1b, res, comb, w_gu, w_d, ln2_g[i].reshape(1, d), ln2_b[i].reshape(1, d), tm_moe,
                   MOE_EXPERTS_PER_STEP)
    return x2.reshape(batch, seq, d)
```
